```python
import math
import jax, jax.numpy as jnp
from jax import lax
import numpy as np

D_MODEL = 2048
BATCH = 8
SEQ = 4096
DEPTH = 4

N_MIXERS = 3
EPS = 1e-6
POOL_WINDOWS = (2, 4, 8, 16)
N_POOL_GROUPS = len(POOL_WINDOWS)
POOL_GROUP = D_MODEL // N_POOL_GROUPS
SWA_HEADS = 32
SWA_KV_HEADS = 4
SWA_GROUP = SWA_HEADS // SWA_KV_HEADS
SWA_HEAD_DIM = D_MODEL // SWA_HEADS
SWA_WINDOW = 128
BLOCK = 128
MLA_HEADS = 16
MLA_NOPE = 128
MLA_ROPE = 64
MLA_V = 128
MLA_Q_RANK = 512
MLA_KV_RANK = 512
ROPE_THETA = 10000.0
D_FF = 5632
CONV_W = 3

kernel_name = "hybrid_pool_swa_mla_convglu_encoder"


def rmsnorm(x, g):
    xf = x.astype(jnp.float32)
    y = xf * lax.rsqrt(jnp.mean(xf * xf, axis=-1, keepdims=True) + EPS)
    return (y * g.astype(jnp.float32)).astype(x.dtype)


def alibi_slopes(n):
    return jnp.asarray(2.0 ** (-8.0 * np.arange(1, n + 1) / n), dtype=jnp.float32)


def rope_tables(positions, dim):
    inv = ROPE_THETA ** (-jnp.arange(0, dim, 2, dtype=jnp.float32) / dim)
    ang = positions.astype(jnp.float32)[:, None] * inv[None, :]
    return jnp.cos(ang), jnp.sin(ang)


def apply_rope(x, cos, sin):
    x1, x2 = jnp.split(x.astype(jnp.float32), 2, axis=-1)
    return jnp.concatenate([x1 * cos - x2 * sin, x2 * cos + x1 * sin], axis=-1).astype(x.dtype)


def pool_mixer(h, w_groups, scale):
    B, S, D = h.shape
    hf = h.astype(jnp.float32).reshape(B, S, N_POOL_GROUPS, POOL_GROUP)
    csum = jnp.concatenate(
        [jnp.zeros((B, 1, N_POOL_GROUPS, POOL_GROUP), jnp.float32), jnp.cumsum(hf, axis=1)],
        axis=1)
    left = np.array([w // 2 for w in POOL_WINDOWS], dtype=np.int32)
    right = np.array([w - 1 - w // 2 for w in POOL_WINDOWS], dtype=np.int32)
    t = jnp.arange(S, dtype=jnp.int32)[:, None]
    hi = jnp.clip(t + right[None, :] + 1, 0, S)
    lo = jnp.clip(t - left[None, :], 0, S)
    g_idx = jnp.arange(N_POOL_GROUPS)[None, :]
    win_sum = csum[:, hi, g_idx] - csum[:, lo, g_idx]
    count = (hi - lo).astype(jnp.float32)[None, :, :, None]
    pooled = (win_sum / count - hf).astype(h.dtype)
    y = jnp.einsum('bsgc,gcd->bsgd', pooled, w_groups).reshape(B, S, D)
    return y * scale


def swa_mixer(h, positions, w_qkv, q_gain, k_gain, sinks, w_o):
    B, S, D = h.shape
    nq = SWA_HEADS * SWA_HEAD_DIM
    nkv = SWA_KV_HEADS * SWA_HEAD_DIM
    qkv = h @ w_qkv
    q = qkv[..., :nq].reshape(B, S, SWA_KV_HEADS, SWA_GROUP, SWA_HEAD_DIM)
    k = qkv[..., nq:nq + nkv].reshape(B, S, SWA_KV_HEADS, SWA_HEAD_DIM)
    v = qkv[..., nq + nkv:].reshape(B, S, SWA_KV_HEADS, SWA_HEAD_DIM)
    q = rmsnorm(q, q_gain) * (SWA_HEAD_DIM ** -0.5)
    k = rmsnorm(k, k_gain)
    pad = SWA_WINDOW
    span = BLOCK + 2 * SWA_WINDOW
    k_pad = jnp.pad(k, ((0, 0), (pad, pad), (0, 0), (0, 0)))
    v_pad = jnp.pad(v, ((0, 0), (pad, pad), (0, 0), (0, 0)))
    pos_pad = jnp.pad(positions, (pad, pad))
    valid_pad = jnp.pad(jnp.ones((S,), dtype=bool), (pad, pad))
    slopes = alibi_slopes(SWA_HEADS).reshape(SWA_KV_HEADS, SWA_GROUP)
    sink = sinks.astype(jnp.float32).reshape(SWA_KV_HEADS, SWA_GROUP)[None, :, :, None]

    def block(j):
        start = j * BLOCK
        qb = lax.dynamic_slice_in_dim(q, start, BLOCK, axis=1)
        kb = lax.dynamic_slice_in_dim(k_pad, start, span, axis=1)
        vb = lax.dynamic_slice_in_dim(v_pad, start, span, axis=1)
        pq = lax.dynamic_slice_in_dim(positions, start, BLOCK)
        pk = lax.dynamic_slice_in_dim(pos_pad, start, span)
        ok = lax.dynamic_slice_in_dim(valid_pad, start, span)
        qi = start + jnp.arange(BLOCK)
        ki = start - SWA_WINDOW + jnp.arange(span)
        in_win = (jnp.abs(qi[:, None] - ki[None, :]) <= SWA_WINDOW) & ok[None, :]
        s = jnp.einsum('bqkgd,bskd->bkgqs', qb, kb).astype(jnp.float32)
        dist = jnp.abs(pq[:, None] - pk[None, :]).astype(jnp.float32)
        s = s - slopes[:, :, None, None] * dist
        s = jnp.where(in_win, s, -jnp.inf)
        m = jnp.maximum(jnp.max(s, axis=-1), sink)
        p = jnp.exp(s - m[..., None])
        denom = jnp.sum(p, axis=-1) + jnp.exp(sink - m)
        p = (p / denom[..., None]).astype(vb.dtype)
        return jnp.einsum('bkgqs,bskd->bqkgd', p, vb)

    o = lax.map(block, jnp.arange(S // BLOCK))
    o = jnp.transpose(o, (1, 0, 2, 3, 4, 5)).reshape(B, S, nq)
    return o @ w_o


def mla_mixer(h, positions, w_down, q_a_gain, kv_a_gain, w_uq, w_ukv,
              qn_gain, qr_gain, kn_gain, kr_gain, w_o):
    B, S, D = h.shape
    d = h @ w_down
    cq = rmsnorm(d[..., :MLA_Q_RANK], q_a_gain)
    ckv = rmsnorm(d[..., MLA_Q_RANK:MLA_Q_RANK + MLA_KV_RANK], kv_a_gain)
    k_pe = d[..., MLA_Q_RANK + MLA_KV_RANK:]
    q = (cq @ w_uq).reshape(B, S, MLA_HEADS, MLA_NOPE + MLA_ROPE)
    kv = (ckv @ w_ukv).reshape(B, S, MLA_HEADS, MLA_NOPE + MLA_V)
    q_nope = rmsnorm(q[..., :MLA_NOPE], qn_gain)
    q_pe = rmsnorm(q[..., MLA_NOPE:], qr_gain)
    k_nope = rmsnorm(kv[..., :MLA_NOPE], kn_gain)
    v = kv[..., MLA_NOPE:]
    k_pe = rmsnorm(k_pe, kr_gain)
    cos, sin = rope_tables(positions, MLA_ROPE)
    q_pe = apply_rope(q_pe, cos[:, None, :], sin[:, None, :])
    k_pe = apply_rope(k_pe, cos, sin)
    scale = (MLA_NOPE + MLA_ROPE) ** -0.5
    q_nope = q_nope * scale
    q_pe = q_pe * scale

    def block(j):
        start = j * BLOCK
        qn = lax.dynamic_slice_in_dim(q_nope, start, BLOCK, axis=1)
        qp = lax.dynamic_slice_in_dim(q_pe, start, BLOCK, axis=1)
        s = (jnp.einsum('bqhd,bshd->bhqs', qn, k_nope).astype(jnp.float32)
             + jnp.einsum('bqhr,bsr->bhqs', qp, k_pe).astype(jnp.float32))
        p = jax.nn.softmax(s, axis=-1).astype(v.dtype)
        return jnp.einsum('bhqs,bshd->bqhd', p, v)

    o = lax.map(block, jnp.arange(S // BLOCK))
    o = jnp.transpose(o, (1, 0, 2, 3, 4)).reshape(B, S, MLA_HEADS * MLA_V)
    return o @ w_o


def conv_glu(h, w_in, conv_w, conv_b, w_out):
    u = h @ w_in
    g, val = u[..., :D_FF], u[..., D_FF:]
    gp = jnp.pad(g, ((0, 0), (1, 1), (0, 0)))
    g = gp[:, :-2] * conv_w[0] + gp[:, 1:-1] * conv_w[1] + gp[:, 2:] * conv_w[2] + conv_b
    return (jax.nn.silu(g) * val) @ w_out


def _fwd_setup_inputs(seed: int = 0) -> dict:
    key = jax.random.key(seed)
    ks = iter(jax.random.split(key, 40))
    f32 = jnp.float32
    n_pool = (DEPTH + 2) // 3
    n_swa = (DEPTH + 1) // 3
    n_mla = DEPTH // 3
    res = (2.0 * DEPTH) ** -0.5

    def w(shape, fan_in, gain=1.0):
        return jax.random.normal(next(ks), shape, f32) * (gain * fan_in ** -0.5)

    def gain(shape):
        return 1.0 + 0.02 * jax.random.normal(next(ks), shape, f32)

    x = jax.random.normal(next(ks), (BATCH, SEQ, D_MODEL), f32)
    positions = jnp.arange(SEQ, dtype=jnp.int32)
    norm_mix_g = gain((DEPTH, D_MODEL))
    norm_ffn_g = gain((DEPTH, D_MODEL))
    pool_w = w((n_pool, N_POOL_GROUPS, POOL_GROUP, POOL_GROUP), POOL_GROUP, res)
    pool_scale = 1.0 + 0.1 * jax.random.normal(next(ks), (n_pool, D_MODEL), f32)
    n_qkv = SWA_HEADS * SWA_HEAD_DIM + 2 * SWA_KV_HEADS * SWA_HEAD_DIM
    swa_w_qkv = w((n_swa, D_MODEL, n_qkv), D_MODEL)
    swa_q_gain = gain((n_swa, SWA_HEAD_DIM))
    swa_k_gain = gain((n_swa, SWA_HEAD_DIM))
    swa_sinks = 0.5 * jax.random.normal(next(ks), (n_swa, SWA_HEADS), f32)
    swa_w_o = w((n_swa, SWA_HEADS * SWA_HEAD_DIM, D_MODEL), SWA_HEADS * SWA_HEAD_DIM, res)
    mla_w_down = w((n_mla, D_MODEL, MLA_Q_RANK + MLA_KV_RANK + MLA_ROPE), D_MODEL)
    mla_q_a_gain = gain((n_mla, MLA_Q_RANK))
    mla_kv_a_gain = gain((n_mla, MLA_KV_RANK))
    mla_w_uq = w((n_mla, MLA_Q_RANK, MLA_HEADS * (MLA_NOPE + MLA_ROPE)), MLA_Q_RANK)
    mla_w_ukv = w((n_mla, MLA_KV_RANK, MLA_HEADS * (MLA_NOPE + MLA_V)), MLA_KV_RANK)
    mla_qn_gain = gain((n_mla, MLA_NOPE))
    mla_qr_gain = gain((n_mla, MLA_ROPE))
    mla_kn_gain = gain((n_mla, MLA_NOPE))
    mla_kr_gain = gain((n_mla, MLA_ROPE))
    mla_w_o = w((n_mla, MLA_HEADS * MLA_V, D_MODEL), MLA_HEADS * MLA_V, res)
    ffn_w_in = w((DEPTH, D_MODEL, 2 * D_FF), D_MODEL)
    ffn_conv_w = w((DEPTH, CONV_W, D_FF), CONV_W)
    ffn_conv_b = 0.02 * jax.random.normal(next(ks), (DEPTH, D_FF), f32)
    ffn_w_out = w((DEPTH, D_FF, D_MODEL), D_FF, res)
    return {
        "x": x, "positions": positions,
        "norm_mix_g": norm_mix_g, "norm_ffn_g": norm_ffn_g,
        "pool_w": pool_w, "pool_scale": pool_scale,
        "swa_w_qkv": swa_w_qkv, "swa_q_gain": swa_q_gain, "swa_k_gain": swa_k_gain,
        "swa_sinks": swa_sinks, "swa_w_o": swa_w_o,
        "mla_w_down": mla_w_down, "mla_q_a_gain": mla_q_a_gain, "mla_kv_a_gain": mla_kv_a_gain,
        "mla_w_uq": mla_w_uq, "mla_w_ukv": mla_w_ukv,
        "mla_qn_gain": mla_qn_gain, "mla_qr_gain": mla_qr_gain,
        "mla_kn_gain": mla_kn_gain, "mla_kr_gain": mla_kr_gain, "mla_w_o": mla_w_o,
        "ffn_w_in": ffn_w_in, "ffn_conv_w": ffn_conv_w, "ffn_conv_b": ffn_conv_b,
        "ffn_w_out": ffn_w_out,
    }


def _fwd_reference(x, positions, norm_mix_g, norm_ffn_g, pool_w, pool_scale,
              swa_w_qkv, swa_q_gain, swa_k_gain, swa_sinks, swa_w_o,
              mla_w_down, mla_q_a_gain, mla_kv_a_gain, mla_w_uq, mla_w_ukv,
              mla_qn_gain, mla_qr_gain, mla_kn_gain, mla_kr_gain, mla_w_o,
              ffn_w_in, ffn_conv_w, ffn_conv_b, ffn_w_out):
    for i in range(DEPTH):
        kind = i % N_MIXERS
        j = i // N_MIXERS
        h = rmsnorm(x, norm_mix_g[i])
        if kind == 0:
            y = pool_mixer(h, pool_w[j], pool_scale[j])
        elif kind == 1:
            y = swa_mixer(h, positions, swa_w_qkv[j], swa_q_gain[j], swa_k_gain[j],
                          swa_sinks[j], swa_w_o[j])
        else:
            y = mla_mixer(h, positions, mla_w_down[j], mla_q_a_gain[j], mla_kv_a_gain[j],
                          mla_w_uq[j], mla_w_ukv[j], mla_qn_gain[j], mla_qr_gain[j],
                          mla_kn_gain[j], mla_kr_gain[j], mla_w_o[j])
        x = x + y
        h = rmsnorm(x, norm_ffn_g[i])
        x = x + conv_glu(h, ffn_w_in[i], ffn_conv_w[i], ffn_conv_b[i], ffn_w_out[i])
    return x


import jax as _jax
import jax.numpy as _jnp

TWIN_FORMAT = 'train_step'
FWD_PARAMS = ['x', 'positions', 'norm_mix_g', 'norm_ffn_g', 'pool_w', 'pool_scale', 'swa_w_qkv', 'swa_q_gain', 'swa_k_gain', 'swa_sinks', 'swa_w_o', 'mla_w_down', 'mla_q_a_gain', 'mla_kv_a_gain', 'mla_w_uq', 'mla_w_ukv', 'mla_qn_gain', 'mla_qr_gain', 'mla_kn_gain', 'mla_kr_gain', 'mla_w_o', 'ffn_w_in', 'ffn_conv_w', 'ffn_conv_b', 'ffn_w_out']
TWIN_WEIGHTS = ['norm_mix_g', 'norm_ffn_g', 'pool_w', 'pool_scale', 'swa_w_qkv', 'swa_q_gain', 'swa_k_gain', 'swa_sinks', 'swa_w_o', 'mla_w_down', 'mla_q_a_gain', 'mla_kv_a_gain', 'mla_w_uq', 'mla_w_ukv', 'mla_qn_gain', 'mla_qr_gain', 'mla_kn_gain', 'mla_kr_gain', 'mla_w_o', 'ffn_w_in', 'ffn_conv_w', 'ffn_conv_b', 'ffn_w_out']
TWIN_DIFF_INPUT = 'x'
TWIN_INPUTS = ['x', 'positions', 'norm_mix_g', 'norm_ffn_g', 'pool_w', 'pool_scale', 'swa_w_qkv', 'swa_q_gain', 'swa_k_gain', 'swa_sinks', 'swa_w_o', 'mla_w_down', 'mla_q_a_gain', 'mla_kv_a_gain', 'mla_w_uq', 'mla_w_ukv', 'mla_qn_gain', 'mla_qr_gain', 'mla_kn_gain', 'mla_kr_gain', 'mla_w_o', 'ffn_w_in', 'ffn_conv_w', 'ffn_conv_b', 'ffn_w_out', 'loss_target', 'm_norm_mix_g', 'm_norm_ffn_g', 'm_pool_w', 'm_pool_scale', 'm_swa_w_qkv', 'm_swa_q_gain', 'm_swa_k_gain', 'm_swa_sinks', 'm_swa_w_o', 'm_mla_w_down', 'm_mla_q_a_gain', 'm_mla_kv_a_gain', 'm_mla_w_uq', 'm_mla_w_ukv', 'm_mla_qn_gain', 'm_mla_qr_gain', 'm_mla_kn_gain', 'm_mla_kr_gain', 'm_mla_w_o', 'm_ffn_w_in', 'm_ffn_conv_w', 'm_ffn_conv_b', 'm_ffn_w_out', 'v_norm_mix_g', 'v_norm_ffn_g', 'v_pool_w', 'v_pool_scale', 'v_swa_w_qkv', 'v_swa_q_gain', 'v_swa_k_gain', 'v_swa_sinks', 'v_swa_w_o', 'v_mla_w_down', 'v_mla_q_a_gain', 'v_mla_kv_a_gain', 'v_mla_w_uq', 'v_mla_w_ukv', 'v_mla_qn_gain', 'v_mla_qr_gain', 'v_mla_kn_gain', 'v_mla_kr_gain', 'v_mla_w_o', 'v_ffn_w_in', 'v_ffn_conv_w', 'v_ffn_conv_b', 'v_ffn_w_out']
TWIN_OUTPUTS = ['loss', 'grad_x', 'grad_norm_mix_g', 'grad_norm_ffn_g', 'grad_pool_w', 'grad_pool_scale', 'grad_swa_w_qkv', 'grad_swa_q_gain', 'grad_swa_k_gain', 'grad_swa_sinks', 'grad_swa_w_o', 'grad_mla_w_down', 'grad_mla_q_a_gain', 'grad_mla_kv_a_gain', 'grad_mla_w_uq', 'grad_mla_w_ukv', 'grad_mla_qn_gain', 'grad_mla_qr_gain', 'grad_mla_kn_gain', 'grad_mla_kr_gain', 'grad_mla_w_o', 'grad_ffn_w_in', 'grad_ffn_conv_w', 'grad_ffn_conv_b', 'grad_ffn_w_out', 'delta_norm_mix_g', 'delta_norm_ffn_g', 'delta_pool_w', 'delta_pool_scale', 'delta_swa_w_qkv', 'delta_swa_q_gain', 'delta_swa_k_gain', 'delta_swa_sinks', 'delta_swa_w_o', 'delta_mla_w_down', 'delta_mla_q_a_gain', 'delta_mla_kv_a_gain', 'delta_mla_w_uq', 'delta_mla_w_ukv', 'delta_mla_qn_gain', 'delta_mla_qr_gain', 'delta_mla_kn_gain', 'delta_mla_kr_gain', 'delta_mla_w_o', 'delta_ffn_w_in', 'delta_ffn_conv_w', 'delta_ffn_conv_b', 'delta_ffn_w_out', 'new_m_norm_mix_g', 'new_m_norm_ffn_g', 'new_m_pool_w', 'new_m_pool_scale', 'new_m_swa_w_qkv', 'new_m_swa_q_gain', 'new_m_swa_k_gain', 'new_m_swa_sinks', 'new_m_swa_w_o', 'new_m_mla_w_down', 'new_m_mla_q_a_gain', 'new_m_mla_kv_a_gain', 'new_m_mla_w_uq', 'new_m_mla_w_ukv', 'new_m_mla_qn_gain', 'new_m_mla_qr_gain', 'new_m_mla_kn_gain', 'new_m_mla_kr_gain', 'new_m_mla_w_o', 'new_m_ffn_w_in', 'new_m_ffn_conv_w', 'new_m_ffn_conv_b', 'new_m_ffn_w_out', 'new_v_norm_mix_g', 'new_v_norm_ffn_g', 'new_v_pool_w', 'new_v_pool_scale', 'new_v_swa_w_qkv', 'new_v_swa_q_gain', 'new_v_swa_k_gain', 'new_v_swa_sinks', 'new_v_swa_w_o', 'new_v_mla_w_down', 'new_v_mla_q_a_gain', 'new_v_mla_kv_a_gain', 'new_v_mla_w_uq', 'new_v_mla_w_ukv', 'new_v_mla_qn_gain', 'new_v_mla_qr_gain', 'new_v_mla_kn_gain', 'new_v_mla_kr_gain', 'new_v_mla_w_o', 'new_v_ffn_w_in', 'new_v_ffn_conv_w', 'new_v_ffn_conv_b', 'new_v_ffn_w_out']
TWIN_LEAF_KINDS = {'loss': 'loss', 'grad_x': 'grad_x', 'grad_norm_mix_g': 'grad_w', 'grad_norm_ffn_g': 'grad_w', 'grad_pool_w': 'grad_w', 'grad_pool_scale': 'grad_w', 'grad_swa_w_qkv': 'grad_w', 'grad_swa_q_gain': 'grad_w', 'grad_swa_k_gain': 'grad_w', 'grad_swa_sinks': 'grad_w', 'grad_swa_w_o': 'grad_w', 'grad_mla_w_down': 'grad_w', 'grad_mla_q_a_gain': 'grad_w', 'grad_mla_kv_a_gain': 'grad_w', 'grad_mla_w_uq': 'grad_w', 'grad_mla_w_ukv': 'grad_w', 'grad_mla_qn_gain': 'grad_w', 'grad_mla_qr_gain': 'grad_w', 'grad_mla_kn_gain': 'grad_w', 'grad_mla_kr_gain': 'grad_w', 'grad_mla_w_o': 'grad_w', 'grad_ffn_w_in': 'grad_w', 'grad_ffn_conv_w': 'grad_w', 'grad_ffn_conv_b': 'grad_w', 'grad_ffn_w_out': 'grad_w', 'delta_norm_mix_g': 'delta_w', 'delta_norm_ffn_g': 'delta_w', 'delta_pool_w': 'delta_w', 'delta_pool_scale': 'delta_w', 'delta_swa_w_qkv': 'delta_w', 'delta_swa_q_gain': 'delta_w', 'delta_swa_k_gain': 'delta_w', 'delta_swa_sinks': 'delta_w', 'delta_swa_w_o': 'delta_w', 'delta_mla_w_down': 'delta_w', 'delta_mla_q_a_gain': 'delta_w', 'delta_mla_kv_a_gain': 'delta_w', 'delta_mla_w_uq': 'delta_w', 'delta_mla_w_ukv': 'delta_w', 'delta_mla_qn_gain': 'delta_w', 'delta_mla_qr_gain': 'delta_w', 'delta_mla_kn_gain': 'delta_w', 'delta_mla_kr_gain': 'delta_w', 'delta_mla_w_o': 'delta_w', 'delta_ffn_w_in': 'delta_w', 'delta_ffn_conv_w': 'delta_w', 'delta_ffn_conv_b': 'delta_w', 'delta_ffn_w_out': 'delta_w', 'new_m_norm_mix_g': 'new_m', 'new_m_norm_ffn_g': 'new_m', 'new_m_pool_w': 'new_m', 'new_m_pool_scale': 'new_m', 'new_m_swa_w_qkv': 'new_m', 'new_m_swa_q_gain': 'new_m', 'new_m_swa_k_gain': 'new_m', 'new_m_swa_sinks': 'new_m', 'new_m_swa_w_o': 'new_m', 'new_m_mla_w_down': 'new_m', 'new_m_mla_q_a_gain': 'new_m', 'new_m_mla_kv_a_gain': 'new_m', 'new_m_mla_w_uq': 'new_m', 'new_m_mla_w_ukv': 'new_m', 'new_m_mla_qn_gain': 'new_m', 'new_m_mla_qr_gain': 'new_m', 'new_m_mla_kn_gain': 'new_m', 'new_m_mla_kr_gain': 'new_m', 'new_m_mla_w_o': 'new_m', 'new_m_ffn_w_in': 'new_m', 'new_m_ffn_conv_w': 'new_m', 'new_m_ffn_conv_b': 'new_m', 'new_m_ffn_w_out': 'new_m', 'new_v_norm_mix_g': 'new_v', 'new_v_norm_ffn_g': 'new_v', 'new_v_pool_w': 'new_v', 'new_v_pool_scale': 'new_v', 'new_v_swa_w_qkv': 'new_v', 'new_v_swa_q_gain': 'new_v', 'new_v_swa_k_gain': 'new_v', 'new_v_swa_sinks': 'new_v', 'new_v_swa_w_o': 'new_v', 'new_v_mla_w_down': 'new_v', 'new_v_mla_q_a_gain': 'new_v', 'new_v_mla_kv_a_gain': 'new_v', 'new_v_mla_w_uq': 'new_v', 'new_v_mla_w_ukv': 'new_v', 'new_v_mla_qn_gain': 'new_v', 'new_v_mla_qr_gain': 'new_v', 'new_v_mla_kn_gain': 'new_v', 'new_v_mla_kr_gain': 'new_v', 'new_v_mla_w_o': 'new_v', 'new_v_ffn_w_in': 'new_v', 'new_v_ffn_conv_w': 'new_v', 'new_v_ffn_conv_b': 'new_v', 'new_v_ffn_w_out': 'new_v'}


def _forward(args):
    return _fwd_reference(*[args[k] for k in FWD_PARAMS])


def _output_shape():
    def fwd():
        inp = _fwd_setup_inputs(0)
        return _fwd_reference(*[inp[k] for k in FWD_PARAMS])
    out = _jax.eval_shape(fwd)
    return out.shape, out.dtype

N_MICROBATCH = 1
ADAM_LR = 0.001
ADAM_B1 = 0.9
ADAM_B2 = 0.999
ADAM_EPS = 1e-08
ADAM_WD = 0.01
ADAM_STEP = 10
PER_EXAMPLE_BATCH_AXIS = {'x': 0, 'loss_target': 0}
SHARED_INPUTS = ['positions']
_WEIGHT_DTYPES = {'norm_mix_g': _jnp.float32, 'norm_ffn_g': _jnp.float32, 'pool_w': _jnp.float32, 'pool_scale': _jnp.float32, 'swa_w_qkv': _jnp.float32, 'swa_q_gain': _jnp.float32, 'swa_k_gain': _jnp.float32, 'swa_sinks': _jnp.float32, 'swa_w_o': _jnp.float32, 'mla_w_down': _jnp.float32, 'mla_q_a_gain': _jnp.float32, 'mla_kv_a_gain': _jnp.float32, 'mla_w_uq': _jnp.float32, 'mla_w_ukv': _jnp.float32, 'mla_qn_gain': _jnp.float32, 'mla_qr_gain': _jnp.float32, 'mla_kn_gain': _jnp.float32, 'mla_kr_gain': _jnp.float32, 'mla_w_o': _jnp.float32, 'ffn_w_in': _jnp.float32, 'ffn_conv_w': _jnp.float32, 'ffn_conv_b': _jnp.float32, 'ffn_w_out': _jnp.float32}
MOMENT_SCALE = {'norm_mix_g': 1.162852e+00, 'norm_ffn_g': 1.619458e+00, 'pool_w': 6.280429e-01, 'pool_scale': 1.634697e+00, 'swa_w_qkv': 2.292100e-02, 'swa_q_gain': 2.039374e+00, 'swa_k_gain': 2.043286e+00, 'swa_sinks': 1.130654e+00, 'swa_w_o': 4.937812e-02, 'mla_w_down': 1.402329e-02, 'mla_q_a_gain': 1.136022e-02, 'mla_kv_a_gain': 2.881457e-02, 'mla_w_uq': 4.998824e-03, 'mla_w_ukv': 5.784557e-03, 'mla_qn_gain': 6.340135e-02, 'mla_qr_gain': 4.938290e-02, 'mla_kn_gain': 6.325950e-02, 'mla_kr_gain': 4.939620e-02, 'mla_w_o': 1.777373e-02, 'ffn_w_in': 2.438114e-02, 'ffn_conv_w': 1.765916e-01, 'ffn_conv_b': 2.188212e-01, 'ffn_w_out': 1.210673e-01}


def _to_microbatches(a, axis):
    t = _jnp.moveaxis(a, axis, 0)
    t = t.reshape((N_MICROBATCH, t.shape[0] // N_MICROBATCH) + t.shape[1:])
    return _jnp.moveaxis(t, 1, axis + 1)


def setup_inputs(seed: int = 0) -> dict:
    inp = _fwd_setup_inputs(seed)
    key = _jax.random.fold_in(_jax.random.key(seed), 7919)
    shape, _ = _output_shape()
    out = dict(inp)
    out["loss_target"] = _jax.random.normal(_jax.random.fold_in(key, 0), shape, _jnp.float32)
    for i, name in enumerate(TWIN_WEIGHTS):
        w = inp[name].astype(_jnp.float32)
        if MOMENT_SCALE is None:
            s = _jnp.sqrt(_jnp.mean(_jnp.square(w)) + 1e-30)
        else:
            s = MOMENT_SCALE[name]
        km, kv = _jax.random.split(_jax.random.fold_in(key, i + 1))
        out[name] = w
        out["m_" + name] = s * _jax.random.normal(km, w.shape, _jnp.float32)
        out["v_" + name] = (s * s) * _jax.random.uniform(kv, w.shape, _jnp.float32, 0.5, 1.5)
    if N_MICROBATCH > 1:
        for name, axis in PER_EXAMPLE_BATCH_AXIS.items():
            out[name] = _to_microbatches(out[name], axis)
    return {'x': out['x'], 'positions': out['positions'], 'norm_mix_g': out['norm_mix_g'], 'norm_ffn_g': out['norm_ffn_g'], 'pool_w': out['pool_w'], 'pool_scale': out['pool_scale'], 'swa_w_qkv': out['swa_w_qkv'], 'swa_q_gain': out['swa_q_gain'], 'swa_k_gain': out['swa_k_gain'], 'swa_sinks': out['swa_sinks'], 'swa_w_o': out['swa_w_o'], 'mla_w_down': out['mla_w_down'], 'mla_q_a_gain': out['mla_q_a_gain'], 'mla_kv_a_gain': out['mla_kv_a_gain'], 'mla_w_uq': out['mla_w_uq'], 'mla_w_ukv': out['mla_w_ukv'], 'mla_qn_gain': out['mla_qn_gain'], 'mla_qr_gain': out['mla_qr_gain'], 'mla_kn_gain': out['mla_kn_gain'], 'mla_kr_gain': out['mla_kr_gain'], 'mla_w_o': out['mla_w_o'], 'ffn_w_in': out['ffn_w_in'], 'ffn_conv_w': out['ffn_conv_w'], 'ffn_conv_b': out['ffn_conv_b'], 'ffn_w_out': out['ffn_w_out'], 'loss_target': out['loss_target'], 'm_norm_mix_g': out['m_norm_mix_g'], 'm_norm_ffn_g': out['m_norm_ffn_g'], 'm_pool_w': out['m_pool_w'], 'm_pool_scale': out['m_pool_scale'], 'm_swa_w_qkv': out['m_swa_w_qkv'], 'm_swa_q_gain': out['m_swa_q_gain'], 'm_swa_k_gain': out['m_swa_k_gain'], 'm_swa_sinks': out['m_swa_sinks'], 'm_swa_w_o': out['m_swa_w_o'], 'm_mla_w_down': out['m_mla_w_down'], 'm_mla_q_a_gain': out['m_mla_q_a_gain'], 'm_mla_kv_a_gain': out['m_mla_kv_a_gain'], 'm_mla_w_uq': out['m_mla_w_uq'], 'm_mla_w_ukv': out['m_mla_w_ukv'], 'm_mla_qn_gain': out['m_mla_qn_gain'], 'm_mla_qr_gain': out['m_mla_qr_gain'], 'm_mla_kn_gain': out['m_mla_kn_gain'], 'm_mla_kr_gain': out['m_mla_kr_gain'], 'm_mla_w_o': out['m_mla_w_o'], 'm_ffn_w_in': out['m_ffn_w_in'], 'm_ffn_conv_w': out['m_ffn_conv_w'], 'm_ffn_conv_b': out['m_ffn_conv_b'], 'm_ffn_w_out': out['m_ffn_w_out'], 'v_norm_mix_g': out['v_norm_mix_g'], 'v_norm_ffn_g': out['v_norm_ffn_g'], 'v_pool_w': out['v_pool_w'], 'v_pool_scale': out['v_pool_scale'], 'v_swa_w_qkv': out['v_swa_w_qkv'], 'v_swa_q_gain': out['v_swa_q_gain'], 'v_swa_k_gain': out['v_swa_k_gain'], 'v_swa_sinks': out['v_swa_sinks'], 'v_swa_w_o': out['v_swa_w_o'], 'v_mla_w_down': out['v_mla_w_down'], 'v_mla_q_a_gain': out['v_mla_q_a_gain'], 'v_mla_kv_a_gain': out['v_mla_kv_a_gain'], 'v_mla_w_uq': out['v_mla_w_uq'], 'v_mla_w_ukv': out['v_mla_w_ukv'], 'v_mla_qn_gain': out['v_mla_qn_gain'], 'v_mla_qr_gain': out['v_mla_qr_gain'], 'v_mla_kn_gain': out['v_mla_kn_gain'], 'v_mla_kr_gain': out['v_mla_kr_gain'], 'v_mla_w_o': out['v_mla_w_o'], 'v_ffn_w_in': out['v_ffn_w_in'], 'v_ffn_conv_w': out['v_ffn_conv_w'], 'v_ffn_conv_b': out['v_ffn_conv_b'], 'v_ffn_w_out': out['v_ffn_w_out']}


def _loss(weights, diff, rest, loss_target):
    with _jax.named_scope("forward"):
        args = {**rest, TWIN_DIFF_INPUT: diff, **{k: w.astype(_WEIGHT_DTYPES[k]) for k, w in weights.items()}}
        y = _forward(args)
    with _jax.named_scope("loss_head"):
        err = _jnp.square(y.astype(_jnp.float32) - loss_target)
        return 0.5 * _jnp.sum(_jnp.mean(err, axis=-1)) if err.ndim else 0.5 * err


def _adamw(w, g, m, v):
    m = ADAM_B1 * m + (1.0 - ADAM_B1) * g
    v = ADAM_B2 * v + (1.0 - ADAM_B2) * _jnp.square(g)
    m_hat = m / (1.0 - ADAM_B1 ** ADAM_STEP)
    v_hat = v / (1.0 - ADAM_B2 ** ADAM_STEP)
    delta = -ADAM_LR * (m_hat / (_jnp.sqrt(v_hat) + ADAM_EPS) + ADAM_WD * w)
    return delta, m, v


def reference(x, positions, norm_mix_g, norm_ffn_g, pool_w, pool_scale, swa_w_qkv, swa_q_gain, swa_k_gain, swa_sinks, swa_w_o, mla_w_down, mla_q_a_gain, mla_kv_a_gain, mla_w_uq, mla_w_ukv, mla_qn_gain, mla_qr_gain, mla_kn_gain, mla_kr_gain, mla_w_o, ffn_w_in, ffn_conv_w, ffn_conv_b, ffn_w_out, loss_target, m_norm_mix_g, m_norm_ffn_g, m_pool_w, m_pool_scale, m_swa_w_qkv, m_swa_q_gain, m_swa_k_gain, m_swa_sinks, m_swa_w_o, m_mla_w_down, m_mla_q_a_gain, m_mla_kv_a_gain, m_mla_w_uq, m_mla_w_ukv, m_mla_qn_gain, m_mla_qr_gain, m_mla_kn_gain, m_mla_kr_gain, m_mla_w_o, m_ffn_w_in, m_ffn_conv_w, m_ffn_conv_b, m_ffn_w_out, v_norm_mix_g, v_norm_ffn_g, v_pool_w, v_pool_scale, v_swa_w_qkv, v_swa_q_gain, v_swa_k_gain, v_swa_sinks, v_swa_w_o, v_mla_w_down, v_mla_q_a_gain, v_mla_kv_a_gain, v_mla_w_uq, v_mla_w_ukv, v_mla_qn_gain, v_mla_qr_gain, v_mla_kn_gain, v_mla_kr_gain, v_mla_w_o, v_ffn_w_in, v_ffn_conv_w, v_ffn_conv_b, v_ffn_w_out):
    given = dict(x=x, positions=positions, norm_mix_g=norm_mix_g, norm_ffn_g=norm_ffn_g, pool_w=pool_w, pool_scale=pool_scale, swa_w_qkv=swa_w_qkv, swa_q_gain=swa_q_gain, swa_k_gain=swa_k_gain, swa_sinks=swa_sinks, swa_w_o=swa_w_o, mla_w_down=mla_w_down, mla_q_a_gain=mla_q_a_gain, mla_kv_a_gain=mla_kv_a_gain, mla_w_uq=mla_w_uq, mla_w_ukv=mla_w_ukv, mla_qn_gain=mla_qn_gain, mla_qr_gain=mla_qr_gain, mla_kn_gain=mla_kn_gain, mla_kr_gain=mla_kr_gain, mla_w_o=mla_w_o, ffn_w_in=ffn_w_in, ffn_conv_w=ffn_conv_w, ffn_conv_b=ffn_conv_b, ffn_w_out=ffn_w_out, loss_target=loss_target, m_norm_mix_g=m_norm_mix_g, m_norm_ffn_g=m_norm_ffn_g, m_pool_w=m_pool_w, m_pool_scale=m_pool_scale, m_swa_w_qkv=m_swa_w_qkv, m_swa_q_gain=m_swa_q_gain, m_swa_k_gain=m_swa_k_gain, m_swa_sinks=m_swa_sinks, m_swa_w_o=m_swa_w_o, m_mla_w_down=m_mla_w_down, m_mla_q_a_gain=m_mla_q_a_gain, m_mla_kv_a_gain=m_mla_kv_a_gain, m_mla_w_uq=m_mla_w_uq, m_mla_w_ukv=m_mla_w_ukv, m_mla_qn_gain=m_mla_qn_gain, m_mla_qr_gain=m_mla_qr_gain, m_mla_kn_gain=m_mla_kn_gain, m_mla_kr_gain=m_mla_kr_gain, m_mla_w_o=m_mla_w_o, m_ffn_w_in=m_ffn_w_in, m_ffn_conv_w=m_ffn_conv_w, m_ffn_conv_b=m_ffn_conv_b, m_ffn_w_out=m_ffn_w_out, v_norm_mix_g=v_norm_mix_g, v_norm_ffn_g=v_norm_ffn_g, v_pool_w=v_pool_w, v_pool_scale=v_pool_scale, v_swa_w_qkv=v_swa_w_qkv, v_swa_q_gain=v_swa_q_gain, v_swa_k_gain=v_swa_k_gain, v_swa_sinks=v_swa_sinks, v_swa_w_o=v_swa_w_o, v_mla_w_down=v_mla_w_down, v_mla_q_a_gain=v_mla_q_a_gain, v_mla_kv_a_gain=v_mla_kv_a_gain, v_mla_w_uq=v_mla_w_uq, v_mla_w_ukv=v_mla_w_ukv, v_mla_qn_gain=v_mla_qn_gain, v_mla_qr_gain=v_mla_qr_gain, v_mla_kn_gain=v_mla_kn_gain, v_mla_kr_gain=v_mla_kr_gain, v_mla_w_o=v_mla_w_o, v_ffn_w_in=v_ffn_w_in, v_ffn_conv_w=v_ffn_conv_w, v_ffn_conv_b=v_ffn_conv_b, v_ffn_w_out=v_ffn_w_out)
    weights = {n: given[n] for n in TWIN_WEIGHTS}
    shared = {n: given[n] for n in SHARED_INPUTS}
    per_example = {n: given[n] for n in ['x']}
    grad_fn = _jax.value_and_grad(_loss, argnums=(0, 1))

    def one_microbatch(ex, loss_target):
        ex = dict(ex)
        diff = ex.pop(TWIN_DIFF_INPUT)
        return grad_fn(weights, diff, {**shared, **ex}, loss_target)

    if N_MICROBATCH == 1:
        loss, (grad_w, grad_x) = one_microbatch(per_example, given["loss_target"])
    else:
        def body(carry, xs):
            loss_sum, grad_sum = carry
            l_k, (gw_k, gx_k) = one_microbatch(xs[0], xs[1])
            with _jax.named_scope("update"):
                return (loss_sum + l_k, _jax.tree.map(_jnp.add, grad_sum, gw_k)), gx_k

        init = (_jnp.zeros((), _jnp.float32), _jax.tree.map(_jnp.zeros_like, weights))
        (loss, grad_w), grad_x = _jax.lax.scan(body, init, (per_example, given["loss_target"]))
    with _jax.named_scope("update"):
        delta_w, new_m, new_v = {}, {}, {}
        for n in TWIN_WEIGHTS:
            delta_w[n], new_m[n], new_v[n] = _adamw(weights[n], grad_w[n], given["m_" + n], given["v_" + n])
    return (loss, grad_x, *[grad_w[n] for n in TWIN_WEIGHTS], *[delta_w[n] for n in TWIN_WEIGHTS],
            *[new_m[n] for n in TWIN_WEIGHTS], *[new_v[n] for n in TWIN_WEIGHTS])
```

```python
import functools
import math

import numpy as np
import jax
import jax.numpy as jnp
from jax import lax
from jax.experimental import pallas as pl
from jax.experimental.pallas import tpu as pltpu

F32 = jnp.float32
BF16 = jnp.bfloat16

D_MODEL = 2048
SEQ = 4096
DEPTH = 4
EPS = 1e-6
POOL_WINDOWS = (2, 4, 8, 16)
SWA_HEADS = 32
SWA_KV_HEADS = 4
SWA_HEAD_DIM = 64
SWA_WINDOW = 128
MLA_HEADS = 16
MLA_NOPE = 128
MLA_ROPE = 64
MLA_V = 128
MLA_Q_RANK = 512
MLA_KV_RANK = 512
ROPE_THETA = 10000.0
D_FF = 5632
ADAM_LR = 0.001
ADAM_B1 = 0.9
ADAM_B2 = 0.999
ADAM_EPS = 1e-08
ADAM_WD = 0.01
ADAM_STEP = 10

N_CHIPS = 4
N_DEV = 8
MESH = pl.DeviceIdType.MESH
VMEM_LIMIT = 48 << 20
LANES = 128
NEG = -1e30

WEIGHTS = ['norm_mix_g', 'norm_ffn_g', 'pool_w', 'pool_scale', 'swa_w_qkv', 'swa_q_gain', 'swa_k_gain', 'swa_sinks',
           'swa_w_o', 'mla_w_down', 'mla_q_a_gain', 'mla_kv_a_gain', 'mla_w_uq', 'mla_w_ukv', 'mla_qn_gain',
           'mla_qr_gain', 'mla_kn_gain', 'mla_kr_gain', 'mla_w_o', 'ffn_w_in', 'ffn_conv_w', 'ffn_conv_b', 'ffn_w_out']
SMALL = {'norm_mix_g': None, 'norm_ffn_g': None, 'pool_scale': 1, 'swa_q_gain': None, 'swa_k_gain': None,
         'swa_sinks': None, 'mla_q_a_gain': 1, 'mla_kv_a_gain': 1, 'mla_qn_gain': None, 'mla_qr_gain': None,
         'mla_kn_gain': None, 'mla_kr_gain': None, 'ffn_conv_w': 2, 'ffn_conv_b': None}


def _cparams(sem):
    return pltpu.CompilerParams(dimension_semantics=sem, vmem_limit_bytes=VMEM_LIMIT)


def _tile(n, cands):
    for c in cands:
        if c <= n and n % c == 0:
            return c
    return n


WIDE = (1408, 1024, 768, 640, 512, 384, 256, 128)


def _dims(arr):
    if len(arr.shape) == 2:
        return arr.shape[0], arr.shape[1], arr.shape[1]
    return arr.shape[1], arr.shape[0] * arr.shape[2], arr.shape[2]


def _pspec(arr, tr, tc, fn):
    if len(arr.shape) == 2:
        return pl.BlockSpec((tr, tc), fn)
    per = arr.shape[2] // tc

    def im(*g):
        r, c = fn(*g)
        return (c // per, r, c % per)

    return pl.BlockSpec((None, tr, tc), im)


def _mm_nn(a, b, out_dtype, name, res=None, out_panels=1):
    M, K, ka = _dims(a)
    K2, N, nb = _dims(b)
    assert K == K2
    no = N // out_panels
    tm = _tile(M, (1024, 512, 256, 128))
    tk = _tile(ka, (512, 256, 128))
    tn = _tile(math.gcd(nb, no), WIDE)
    nk = K // tk
    o_sds = jax.ShapeDtypeStruct((M, N) if out_panels == 1 else (out_panels, M, no), out_dtype)

    def body(*refs):
        a_ref, b_ref = refs[:2]
        o_ref, acc = refs[-2:]
        k = pl.program_id(2)

        @pl.when(k == 0)
        def _():
            acc[...] = jnp.zeros_like(acc)

        acc[...] += jnp.dot(a_ref[...].astype(BF16), b_ref[...].astype(BF16), preferred_element_type=F32)

        @pl.when(k == nk - 1)
        def _():
            r = acc[...]
            if res is not None:
                r = r + refs[2][...]
            o_ref[...] = r.astype(o_ref.dtype)

    in_specs = [_pspec(a, tm, tk, lambda i, j, k: (i, k)), _pspec(b, tk, tn, lambda i, j, k: (k, j))]
    args = [a, b]
    if res is not None:
        in_specs.append(_pspec(res, tm, tn, lambda i, j, k: (i, j)))
        args.append(res)
    return pl.pallas_call(
        body, grid=(M // tm, N // tn, nk), in_specs=in_specs, out_specs=_pspec(o_sds, tm, tn, lambda i, j, k: (i, j)),
        out_shape=o_sds, scratch_shapes=[pltpu.VMEM((tm, tn), F32)],
        compiler_params=_cparams(("parallel", "parallel", "arbitrary")), name=name)(*args)


def _mm_nt(a, b, out_dtype, name):
    M, N, na = _dims(a)
    K, N2, nb = _dims(b)
    assert N == N2
    tm = _tile(M, (1024, 512, 256, 128))
    to = _tile(K, (1024, 512, 256, 128))
    tc = _tile(math.gcd(na, nb), WIDE)
    nc = N // tc

    def body(a_ref, b_ref, o_ref, acc):
        c = pl.program_id(2)

        @pl.when(c == 0)
        def _():
            acc[...] = jnp.zeros_like(acc)

        acc[...] += lax.dot_general(a_ref[...].astype(BF16), b_ref[...].astype(BF16), (((1,), (1,)), ((), ())),
                                    preferred_element_type=F32)

        @pl.when(c == nc - 1)
        def _():
            o_ref[...] = acc[...].astype(o_ref.dtype)

    return pl.pallas_call(
        body, grid=(M // tm, K // to, nc),
        in_specs=[_pspec(a, tm, tc, lambda i, j, c: (i, c)), _pspec(b, to, tc, lambda i, j, c: (j, c))],
        out_specs=pl.BlockSpec((tm, to), lambda i, j, c: (i, j)), out_shape=jax.ShapeDtypeStruct((M, K), out_dtype),
        scratch_shapes=[pltpu.VMEM((tm, to), F32)],
        compiler_params=_cparams(("parallel", "parallel", "arbitrary")), name=name)(a, b)


def _mm_tn(a, b, out_dtype, name, out_panels=1):
    M, K, ka = _dims(a)
    M2, N, nb = _dims(b)
    assert M == M2
    no = N // out_panels
    tk = _tile(ka, (1024, 512, 256, 128))
    tn = _tile(math.gcd(nb, no), WIDE)
    tm = _tile(M, (512, 256, 128))
    nm = M // tm
    o_sds = jax.ShapeDtypeStruct((K, N) if out_panels == 1 else (out_panels, K, no), out_dtype)

    def body(a_ref, b_ref, o_ref, acc):
        m = pl.program_id(2)

        @pl.when(m == 0)
        def _():
            acc[...] = jnp.zeros_like(acc)

        acc[...] += lax.dot_general(a_ref[...].astype(BF16), b_ref[...].astype(BF16), (((0,), (0,)), ((), ())),
                                    preferred_element_type=F32)

        @pl.when(m == nm - 1)
        def _():
            o_ref[...] = acc[...].astype(o_ref.dtype)

    return pl.pallas_call(
        body, grid=(K // tk, N // tn, nm),
        in_specs=[_pspec(a, tm, tk, lambda i, j, m: (m, i)), _pspec(b, tm, tn, lambda i, j, m: (m, j))],
        out_specs=_pspec(o_sds, tk, tn, lambda i, j, m: (i, j)), out_shape=o_sds,
        scratch_shapes=[pltpu.VMEM((tk, tn), F32)],
        compiler_params=_cparams(("parallel", "parallel", "arbitrary")), name=name)(a, b)


def _swap_halves(y):
    h = y.shape[-1] // 2
    return jnp.concatenate([y[:, h:], y[:, :h]], axis=1)


def _row_tile(R, d, limit=None):
    cap = max(8, (1 << 19) // d)
    cands = [c for c in (4096, 2048, 1024, 512, 256, 128, 64, 32, 16, 8) if c <= cap]
    if limit is not None:
        cands = [c for c in cands if limit % c == 0]
    return _tile(R, cands)


def _rms_fwd(x, g, out_dtype, name, scale=1.0, rope=None):
    R, d = x.shape
    tr = _row_tile(R, d, None if rope is None else rope[0].shape[0])

    def body(*refs):
        x_ref, g_ref = refs[:2]
        o_ref = refs[-1]
        xv = x_ref[...].astype(F32)
        y = xv * lax.rsqrt(jnp.mean(xv * xv, axis=-1, keepdims=True) + EPS)
        y = y * g_ref[...]
        if rope is not None:
            y = y * refs[2][...] + _swap_halves(y) * refs[3][...]
        if scale != 1.0:
            y = y * scale
        o_ref[...] = y.astype(o_ref.dtype)

    in_specs = [pl.BlockSpec((tr, d), lambda i: (i, 0)), pl.BlockSpec((1, d), lambda i: (0, 0))]
    args = [x, g]
    if rope is not None:
        nrt = rope[0].shape[0] // tr
        in_specs += [pl.BlockSpec((tr, d), lambda i: (i % nrt, 0))] * 2
        args += list(rope)
    return pl.pallas_call(
        body, grid=(R // tr,), in_specs=in_specs, out_specs=pl.BlockSpec((tr, d), lambda i: (i, 0)),
        out_shape=jax.ShapeDtypeStruct((R, d), out_dtype), compiler_params=_cparams(("parallel",)), name=name)(*args)


def _rms_bwd(x, g, dy, name, scale=1.0, rope=None, res=None):
    R, d = x.shape
    tr = _row_tile(R, d, None if rope is None else rope[0].shape[0])

    def body(*refs):
        x_ref, g_ref, dy_ref = refs[:3]
        dx_ref, dg_ref = refs[-2:]
        i = pl.program_id(0)
        xv = x_ref[...].astype(F32)
        r = lax.rsqrt(jnp.mean(xv * xv, axis=-1, keepdims=True) + EPS)
        xhat = xv * r
        dyv = dy_ref[...].astype(F32)
        if scale != 1.0:
            dyv = dyv * scale
        if rope is not None:
            dyv = dyv * refs[3][...] + _swap_halves(dyv * refs[4][...])

        @pl.when(i == 0)
        def _():
            dg_ref[...] = jnp.zeros_like(dg_ref)

        dg_ref[...] += jnp.sum(dyv * xhat, axis=0, keepdims=True)
        dxh = dyv * g_ref[...]
        dx = r * (dxh - xhat * jnp.mean(dxh * xhat, axis=-1, keepdims=True))
        if res is not None:
            dx = dx + refs[-3][...]
        dx_ref[...] = dx

    row = pl.BlockSpec((tr, d), lambda i: (i, 0))
    vec = pl.BlockSpec((1, d), lambda i: (0, 0))
    in_specs = [row, vec, row]
    args = [x, g, dy]
    if rope is not None:
        nrt = rope[0].shape[0] // tr
        in_specs += [pl.BlockSpec((tr, d), lambda i: (i % nrt, 0))] * 2
        args += list(rope)
    if res is not None:
        in_specs.append(row)
        args.append(res)
    return pl.pallas_call(
        body, grid=(R // tr,), in_specs=in_specs, out_specs=[row, vec],
        out_shape=[jax.ShapeDtypeStruct((R, d), F32), jax.ShapeDtypeStruct((1, d), F32)],
        compiler_params=_cparams(("arbitrary",)), name=name)(*args)


PAD = 16


def _pool_win(x, fwd, out_dtype, name):
    S, D = x.shape
    G = len(POOL_WINDOWS)
    dg = D // G
    tc = _tile(dg, (128,))
    nt = dg // tc
    rc = _tile(S, (512,))

    def body(*refs):
        x_refs, o_refs, scr = refs[:G], refs[G:2 * G], refs[2 * G]
        zeros = jnp.zeros((PAD, tc), F32)
        scr[pl.ds(0, PAD), :] = zeros
        scr[pl.ds(PAD + S, PAD), :] = zeros
        for gi, w in enumerate(POOL_WINDOWS):
            left, right = w // 2, w - 1 - w // 2

            def count(r0):
                t = r0 + lax.broadcasted_iota(jnp.int32, (rc, 1), 0)
                return (jnp.minimum(t + right + 1, S) - jnp.maximum(t - left, 0)).astype(F32)

            for r0 in range(0, S, rc):
                xv = x_refs[gi][pl.ds(r0, rc), :]
                scr[pl.ds(PAD + r0, rc), :] = xv if fwd else xv / count(r0)
            lo, hi = (left, right) if fwd else (right, left)
            for r0 in range(0, S, rc):
                acc = scr[pl.ds(PAD + r0 - lo, rc), :]
                for o in range(-lo + 1, hi + 1):
                    acc = acc + scr[pl.ds(PAD + r0 + o, rc), :]
                xv = x_refs[gi][pl.ds(r0, rc), :]
                out = acc / count(r0) - xv if fwd else acc - xv
                o_refs[gi][pl.ds(r0, rc), :] = out.astype(out_dtype)

    in_specs = [pl.BlockSpec((S, tc), functools.partial(lambda j, gi: (0, gi * nt + j), gi=gi)) for gi in range(G)]
    out_specs = [pl.BlockSpec((S, tc), lambda j: (0, j)) for _ in range(G)]
    return pl.pallas_call(
        body, grid=(nt,), in_specs=in_specs, out_specs=out_specs,
        out_shape=[jax.ShapeDtypeStruct((S, dg), out_dtype) for _ in range(G)],
        scratch_shapes=[pltpu.VMEM((S + 2 * PAD, tc), F32)],
        compiler_params=_cparams(("parallel",)), name=name)(*([x] * G))


def _scale_res(x, y, scale, name):
    S, D = x.shape
    tr = _row_tile(S, D)
    row = pl.BlockSpec((tr, D), lambda i: (i, 0))

    def body(x_ref, y_ref, s_ref, o_ref):
        o_ref[...] = x_ref[...] + y_ref[...] * s_ref[...]

    return pl.pallas_call(body, grid=(S // tr,), in_specs=[row, row, pl.BlockSpec((1, D), lambda i: (0, 0))],
                          out_specs=row, out_shape=jax.ShapeDtypeStruct((S, D), F32),
                          compiler_params=_cparams(("parallel",)), name=name)(x, y, scale)


def _scale_bwd(dy, y, scale, name):
    S, D = dy.shape
    tr = _row_tile(S, D)
    row = pl.BlockSpec((tr, D), lambda i: (i, 0))
    vec = pl.BlockSpec((1, D), lambda i: (0, 0))

    def body(dy_ref, y_ref, s_ref, o_ref, ds_ref):
        @pl.when(pl.program_id(0) == 0)
        def _():
            ds_ref[...] = jnp.zeros_like(ds_ref)

        d = dy_ref[...]
        ds_ref[...] += jnp.sum(d * y_ref[...], axis=0, keepdims=True)
        o_ref[...] = (d * s_ref[...]).astype(BF16)

    return pl.pallas_call(body, grid=(S // tr,), in_specs=[row, row, vec], out_specs=[row, vec],
                          out_shape=[jax.ShapeDtypeStruct((S, D), BF16), jax.ShapeDtypeStruct((1, D), F32)],
                          compiler_params=_cparams(("arbitrary",)), name=name)(dy, y, scale)


GPAD = 8


def _sigmoid(z):
    return 1.0 / (1.0 + jnp.exp(-z))


def _glu_fwd(u, cw, cb, name):
    _, S, F = u.shape
    tc = _tile(F, (128,))
    rc = _tile(S, (512,))

    def body(u_ref, w_ref, b_ref, o_ref, scr):
        zeros = jnp.zeros((GPAD, tc), F32)
        scr[pl.ds(0, GPAD), :] = zeros
        scr[pl.ds(GPAD + S, GPAD), :] = zeros
        for r0 in range(0, S, rc):
            scr[pl.ds(GPAD + r0, rc), :] = u_ref[0, pl.ds(r0, rc), :]
        w0, w1, w2, b = w_ref[0:1, :], w_ref[1:2, :], w_ref[2:3, :], b_ref[...]
        for r0 in range(0, S, rc):
            gc = (scr[pl.ds(GPAD + r0 - 1, rc), :] * w0 + scr[pl.ds(GPAD + r0, rc), :] * w1
                  + scr[pl.ds(GPAD + r0 + 1, rc), :] * w2 + b)
            o_ref[pl.ds(r0, rc), :] = (gc * _sigmoid(gc) * u_ref[1, pl.ds(r0, rc), :]).astype(BF16)

    return pl.pallas_call(
        body, grid=(F // tc,),
        in_specs=[pl.BlockSpec((2, S, tc), lambda j: (0, 0, j)), pl.BlockSpec((3, tc), lambda j: (0, j)),
                  pl.BlockSpec((1, tc), lambda j: (0, j))],
        out_specs=pl.BlockSpec((S, tc), lambda j: (0, j)), out_shape=jax.ShapeDtypeStruct((S, F), BF16),
        scratch_shapes=[pltpu.VMEM((S + 2 * GPAD, tc), F32)],
        compiler_params=_cparams(("parallel",)), name=name)(u, cw, cb)


def _glu_bwd(u, cw, cb, dact, name):
    _, S, F = u.shape
    tc = _tile(F, (128,))
    rc = _tile(S, (512,))

    def body(u_ref, w_ref, b_ref, da_ref, du_ref, dw_ref, db_ref, scr_g, scr_d):
        zeros = jnp.zeros((GPAD, tc), F32)
        for scr in (scr_g, scr_d):
            scr[pl.ds(0, GPAD), :] = zeros
            scr[pl.ds(GPAD + S, GPAD), :] = zeros
        for r0 in range(0, S, rc):
            scr_g[pl.ds(GPAD + r0, rc), :] = u_ref[0, pl.ds(r0, rc), :]
        w0, w1, w2, b = w_ref[0:1, :], w_ref[1:2, :], w_ref[2:3, :], b_ref[...]
        sums = [jnp.zeros((1, tc), F32) for _ in range(4)]
        for r0 in range(0, S, rc):
            gp = scr_g[pl.ds(GPAD + r0 - 1, rc), :]
            g0 = scr_g[pl.ds(GPAD + r0, rc), :]
            gn = scr_g[pl.ds(GPAD + r0 + 1, rc), :]
            gc = gp * w0 + g0 * w1 + gn * w2 + b
            sig = _sigmoid(gc)
            da = da_ref[pl.ds(r0, rc), :]
            du_ref[1, pl.ds(r0, rc), :] = (da * (gc * sig)).astype(BF16)
            dgc = da * u_ref[1, pl.ds(r0, rc), :] * (sig * (1.0 + gc * (1.0 - sig)))
            scr_d[pl.ds(GPAD + r0, rc), :] = dgc
            for n, t in enumerate((dgc * gp, dgc * g0, dgc * gn, dgc)):
                sums[n] = sums[n] + jnp.sum(t, axis=0, keepdims=True)
        dw_ref[...] = jnp.concatenate(sums[:3], axis=0)
        db_ref[...] = sums[3]
        for r0 in range(0, S, rc):
            dg = (scr_d[pl.ds(GPAD + r0 + 1, rc), :] * w0 + scr_d[pl.ds(GPAD + r0, rc), :] * w1
                  + scr_d[pl.ds(GPAD + r0 - 1, rc), :] * w2)
            du_ref[0, pl.ds(r0, rc), :] = dg.astype(BF16)

    col = pl.BlockSpec((S, tc), lambda j: (0, j))
    return pl.pallas_call(
        body, grid=(F // tc,),
        in_specs=[pl.BlockSpec((2, S, tc), lambda j: (0, 0, j)), pl.BlockSpec((3, tc), lambda j: (0, j)),
                  pl.BlockSpec((1, tc), lambda j: (0, j)), col],
        out_specs=[pl.BlockSpec((2, S, tc), lambda j: (0, 0, j)), pl.BlockSpec((3, tc), lambda j: (0, j)),
                   pl.BlockSpec((1, tc), lambda j: (0, j))],
        out_shape=[jax.ShapeDtypeStruct((2, S, F), BF16), jax.ShapeDtypeStruct((3, F), F32),
                   jax.ShapeDtypeStruct((1, F), F32)],
        scratch_shapes=[pltpu.VMEM((S + 2 * GPAD, tc), F32), pltpu.VMEM((S + 2 * GPAD, tc), F32)],
        compiler_params=_cparams(("parallel",)), name=name)(u, cw, cb, dact)


def _dot_nt(a, b):
    return lax.dot_general(a, b, (((1,), (1,)), ((), ())), preferred_element_type=F32)


def _dot_tn(a, b):
    return lax.dot_general(a, b, (((0,), (0,)), ((), ())), preferred_element_type=F32)


def _swa_specs(S, nq, G, hd, bq):
    prev = lambda j: jnp.maximum(j - 1, 0)
    nxt = lambda j: jnp.minimum(j + 1, nq - 1)
    kv = [pl.BlockSpec((None, bq, hd), functools.partial(lambda kh, j, f: (kh, f(j), 0), f=f))
          for f in (prev, lambda j: j, nxt)]
    pk = [pl.BlockSpec((1, bq), functools.partial(lambda kh, j, f: (0, f(j)), f=f)) for f in (prev, lambda j: j, nxt)]
    smem = pl.BlockSpec(memory_space=pltpu.SMEM)
    return ([pl.BlockSpec((bq, G * hd), lambda kh, j: (j, kh))] + kv + kv
            + [pl.BlockSpec((bq, 1), lambda kh, j: (j, 0))] + pk + [smem, smem])


def _swa_scores(j, kh, g, S, bq, G, hd, q_ref, kspan, dist, slopes, sinks):
    qi = j * bq + lax.broadcasted_iota(jnp.int32, (bq, 1), 0)
    ki = (j - 1) * bq + lax.broadcasted_iota(jnp.int32, (1, 3 * bq), 1)
    ok = (jnp.abs(qi - ki) <= SWA_WINDOW) & (ki >= 0) & (ki < S)
    qg = q_ref[:, g * hd:(g + 1) * hd]
    s = _dot_nt(qg, kspan) - slopes[kh * G + g] * dist
    return qg, jnp.where(ok, s, NEG), sinks[kh * G + g]


def _swa_fwd_attn(qn, kT, vT, posq, posk, slopes, sinks, name):
    Hkv, S, hd = kT.shape
    G = qn.shape[1] // (Hkv * hd)
    bq = SWA_WINDOW
    nq = S // bq

    def body(q_ref, k0, k1, k2, v0, v1, v2, pq, p0, p1, p2, slopes_ref, sinks_ref, o_ref, l_ref):
        kh, j = pl.program_id(0), pl.program_id(1)
        kspan = jnp.concatenate([k0[...], k1[...], k2[...]], axis=0)
        vspan = jnp.concatenate([v0[...], v1[...], v2[...]], axis=0)
        dist = jnp.abs(pq[...] - jnp.concatenate([p0[...], p1[...], p2[...]], axis=1))
        outs, lses = [], []
        for g in range(G):
            _, s, sink = _swa_scores(j, kh, g, S, bq, G, hd, q_ref, kspan, dist, slopes_ref, sinks_ref)
            m = jnp.maximum(jnp.max(s, axis=-1, keepdims=True), sink)
            p = jnp.exp(s - m)
            den = jnp.sum(p, axis=-1, keepdims=True) + jnp.exp(sink - m)
            outs.append(jnp.dot((p / den).astype(BF16), vspan, preferred_element_type=F32))
            lses.append(m + jnp.log(den))
        o_ref[...] = jnp.concatenate(outs, axis=1)
        l_ref[...] = jnp.concatenate(lses, axis=1)

    return pl.pallas_call(
        body, grid=(Hkv, nq), in_specs=_swa_specs(S, nq, G, hd, bq),
        out_specs=[pl.BlockSpec((bq, G * hd), lambda kh, j: (j, kh)), pl.BlockSpec((None, bq, G), lambda kh, j: (kh, j, 0))],
        out_shape=[jax.ShapeDtypeStruct(qn.shape, F32), jax.ShapeDtypeStruct((Hkv, S, G), F32)],
        compiler_params=_cparams(("parallel", "parallel")), name=name)(
            qn, kT, kT, kT, vT, vT, vT, posq, posk, posk, posk, slopes, sinks)


def _swa_bwd_attn(qn, kT, vT, posq, posk, slopes, sinks, o, lse, do, name):
    Hkv, S, hd = kT.shape
    G = qn.shape[1] // (Hkv * hd)
    bq = SWA_WINDOW
    nq = S // bq

    def body(q_ref, k0, k1, k2, v0, v1, v2, pq, p0, p1, p2, slopes_ref, sinks_ref, o_ref, l_ref, do_ref,
             dq_ref, dk_ref, dv_ref, ds_ref):
        kh, j = pl.program_id(0), pl.program_id(1)
        kspan = jnp.concatenate([k0[...], k1[...], k2[...]], axis=0)
        vspan = jnp.concatenate([v0[...], v1[...], v2[...]], axis=0)
        dist = jnp.abs(pq[...] - jnp.concatenate([p0[...], p1[...], p2[...]], axis=1))
        dk = jnp.zeros((3 * bq, hd), F32)
        dv = jnp.zeros((3 * bq, hd), F32)
        dqs, dsinks = [], []
        for g in range(G):
            qg, s, sink = _swa_scores(j, kh, g, S, bq, G, hd, q_ref, kspan, dist, slopes_ref, sinks_ref)
            lg = l_ref[:, g:g + 1]
            p = jnp.exp(s - lg)
            dog = do_ref[:, g * hd:(g + 1) * hd]
            delta = jnp.sum(dog * o_ref[:, g * hd:(g + 1) * hd], axis=-1, keepdims=True)
            dog = dog.astype(BF16)
            dsc = (p * (_dot_nt(dog, vspan) - delta)).astype(BF16)
            dqs.append(jnp.dot(dsc, kspan, preferred_element_type=F32))
            dk = dk + _dot_tn(dsc, qg)
            dv = dv + _dot_tn(p.astype(BF16), dog)
            dsinks.append(jnp.sum(-jnp.exp(sink - lg) * delta, axis=0, keepdims=True))
        dq_ref[...] = jnp.concatenate(dqs, axis=1)
        dk_ref[...] = dk
        dv_ref[...] = dv

        @pl.when(j == 0)
        def _():
            ds_ref[...] = jnp.zeros_like(ds_ref)

        ds_ref[...] += jnp.concatenate(dsinks, axis=1)

    qblk = pl.BlockSpec((bq, G * hd), lambda kh, j: (j, kh))
    span = pl.BlockSpec((None, None, 3 * bq, hd), lambda kh, j: (kh, j, 0, 0))
    return pl.pallas_call(
        body, grid=(Hkv, nq),
        in_specs=_swa_specs(S, nq, G, hd, bq) + [qblk, pl.BlockSpec((None, bq, G), lambda kh, j: (kh, j, 0)), qblk],
        out_specs=[qblk, span, span, pl.BlockSpec((None, 1, G), lambda kh, j: (kh, 0, 0))],
        out_shape=[jax.ShapeDtypeStruct(qn.shape, F32), jax.ShapeDtypeStruct((Hkv, nq, 3 * bq, hd), F32),
                   jax.ShapeDtypeStruct((Hkv, nq, 3 * bq, hd), F32), jax.ShapeDtypeStruct((Hkv, 1, G), F32)],
        compiler_params=_cparams(("parallel", "arbitrary")), name=name)(
            qn, kT, kT, kT, vT, vT, vT, posq, posk, posk, posk, slopes, sinks, o, lse, do)


def _overlap3(spans, name):
    Hkv, nq, bq3, hd = spans.shape
    bq = bq3 // 3
    sp = spans.reshape(Hkv, nq, 3, bq, hd)

    def body(a_ref, b_ref, c_ref, o_ref):
        b = pl.program_id(1)
        acc = b_ref[...]
        acc = acc + jnp.where(b > 0, a_ref[...], 0.0)
        acc = acc + jnp.where(b < nq - 1, c_ref[...], 0.0)
        o_ref[...] = acc

    def part(f, slot):
        return pl.BlockSpec((None, None, None, bq, hd), lambda kh, b: (kh, f(b), slot, 0, 0))

    return pl.pallas_call(
        body, grid=(Hkv, nq),
        in_specs=[part(lambda b: jnp.maximum(b - 1, 0), 2), part(lambda b: b, 1), part(lambda b: jnp.minimum(b + 1, nq - 1), 0)],
        out_specs=pl.BlockSpec((None, bq, hd), lambda kh, b: (kh, b, 0)),
        out_shape=jax.ShapeDtypeStruct((Hkv, nq * bq, hd), F32),
        compiler_params=_cparams(("parallel", "parallel")), name=name)(sp, sp, sp)


def _mla_tiles(S):
    t = _tile(S, (512, 256, 128))
    return t, t


def _mla_fwd_attn(qn, qp, kn, kp, v, name):
    H, S, dn = qn.shape
    R = qp.shape[2]
    dv = v.shape[2]
    tq, tk = _mla_tiles(S)
    nk = S // tk

    def body(qn_ref, qp_ref, kn_ref, kp_ref, v_ref, o_ref, l_ref, m_s, l_s, acc):
        k = pl.program_id(2)

        @pl.when(k == 0)
        def _():
            m_s[...] = jnp.full_like(m_s, NEG)
            l_s[...] = jnp.zeros_like(l_s)
            acc[...] = jnp.zeros_like(acc)

        s = _dot_nt(qn_ref[...], kn_ref[...]) + _dot_nt(qp_ref[...], kp_ref[...])
        m_new = jnp.maximum(m_s[...], jnp.max(s, axis=-1, keepdims=True))
        a = jnp.exp(m_s[...] - m_new)
        p = jnp.exp(s - m_new)
        l_s[...] = a * l_s[...] + jnp.sum(p, axis=-1, keepdims=True)
        acc[...] = a * acc[...] + jnp.dot(p.astype(BF16), v_ref[...], preferred_element_type=F32)
        m_s[...] = m_new

        @pl.when(k == nk - 1)
        def _():
            o_ref[...] = acc[...] / l_s[...]
            l_ref[...] = m_s[...] + jnp.log(l_s[...])

    qs = lambda d: pl.BlockSpec((None, tq, d), lambda h, i, k: (h, i, 0))
    ks = lambda d: pl.BlockSpec((None, tk, d), lambda h, i, k: (h, k, 0))
    return pl.pallas_call(
        body, grid=(H, S // tq, nk),
        in_specs=[qs(dn), qs(R), ks(dn), pl.BlockSpec((tk, R), lambda h, i, k: (k, 0)), ks(dv)],
        out_specs=[qs(dv), qs(1)],
        out_shape=[jax.ShapeDtypeStruct((H, S, dv), F32), jax.ShapeDtypeStruct((H, S, 1), F32)],
        scratch_shapes=[pltpu.VMEM((tq, 1), F32), pltpu.VMEM((tq, 1), F32), pltpu.VMEM((tq, dv), F32)],
        compiler_params=_cparams(("parallel", "parallel", "arbitrary")), name=name)(qn, qp, kn, kp, v)


def _mla_bwd_q(qn, qp, kn, kp, v, o, lse, do, name):
    H, S, dn = qn.shape
    R = qp.shape[2]
    dv = v.shape[2]
    tq, tk = _mla_tiles(S)
    nk = S // tk

    def body(qn_ref, qp_ref, kn_ref, kp_ref, v_ref, o_ref, l_ref, do_ref, dqn_ref, dqp_ref, dl_ref, an, ap, dl_s):
        k = pl.program_id(2)

        @pl.when(k == 0)
        def _():
            an[...] = jnp.zeros_like(an)
            ap[...] = jnp.zeros_like(ap)
            dl_s[...] = jnp.sum(do_ref[...] * o_ref[...], axis=-1, keepdims=True)

        s = _dot_nt(qn_ref[...], kn_ref[...]) + _dot_nt(qp_ref[...], kp_ref[...])
        p = jnp.exp(s - l_ref[...])
        dsc = (p * (_dot_nt(do_ref[...].astype(BF16), v_ref[...]) - dl_s[...])).astype(BF16)
        an[...] += jnp.dot(dsc, kn_ref[...], preferred_element_type=F32)
        ap[...] += jnp.dot(dsc, kp_ref[...], preferred_element_type=F32)

        @pl.when(k == nk - 1)
        def _():
            dqn_ref[...] = an[...]
            dqp_ref[...] = ap[...]
            dl_ref[...] = dl_s[...]

    qs = lambda d: pl.BlockSpec((None, tq, d), lambda h, i, k: (h, i, 0))
    ks = lambda d: pl.BlockSpec((None, tk, d), lambda h, i, k: (h, k, 0))
    return pl.pallas_call(
        body, grid=(H, S // tq, nk),
        in_specs=[qs(dn), qs(R), ks(dn), pl.BlockSpec((tk, R), lambda h, i, k: (k, 0)), ks(dv), qs(dv), qs(1), qs(dv)],
        out_specs=[qs(dn), qs(R), qs(1)],
        out_shape=[jax.ShapeDtypeStruct((H, S, dn), F32), jax.ShapeDtypeStruct((H, S, R), F32),
                   jax.ShapeDtypeStruct((H, S, 1), F32)],
        scratch_shapes=[pltpu.VMEM((tq, dn), F32), pltpu.VMEM((tq, R), F32), pltpu.VMEM((tq, 1), F32)],
        compiler_params=_cparams(("parallel", "parallel", "arbitrary")), name=name)(qn, qp, kn, kp, v, o, lse, do)


def _mla_bwd_kv(qn, qp, kn, kp, v, lse, delta, do, name):
    H, S, dn = qn.shape
    R = qp.shape[2]
    dv = v.shape[2]
    tq, tk = _mla_tiles(S)
    nq = S // tq

    def body(qn_ref, qp_ref, kn_ref, kp_ref, v_ref, l_ref, dl_ref, do_ref, dkn_ref, dv_ref, dkp_ref, an, av, ap):
        h, i = pl.program_id(1), pl.program_id(2)

        @pl.when(i == 0)
        def _():
            an[...] = jnp.zeros_like(an)
            av[...] = jnp.zeros_like(av)

        @pl.when((i == 0) & (h == 0))
        def _():
            ap[...] = jnp.zeros_like(ap)

        s = _dot_nt(qn_ref[...], kn_ref[...]) + _dot_nt(qp_ref[...], kp_ref[...])
        p = jnp.exp(s - l_ref[...])
        dob = do_ref[...].astype(BF16)
        dsc = (p * (_dot_nt(dob, v_ref[...]) - dl_ref[...])).astype(BF16)
        an[...] += _dot_tn(dsc, qn_ref[...])
        ap[...] += _dot_tn(dsc, qp_ref[...])
        av[...] += _dot_tn(p.astype(BF16), dob)

        @pl.when(i == nq - 1)
        def _():
            dkn_ref[...] = an[...]
            dv_ref[...] = av[...]

        @pl.when((i == nq - 1) & (h == H - 1))
        def _():
            dkp_ref[...] = ap[...]

    qs = lambda d: pl.BlockSpec((None, tq, d), lambda k, h, i: (h, i, 0))
    ks = lambda d: pl.BlockSpec((None, tk, d), lambda k, h, i: (h, k, 0))
    kps = pl.BlockSpec((tk, R), lambda k, h, i: (k, 0))
    return pl.pallas_call(
        body, grid=(S // tk, H, nq),
        in_specs=[qs(dn), qs(R), ks(dn), kps, ks(dv), qs(1), qs(1), qs(dv)],
        out_specs=[ks(dn), ks(dv), kps],
        out_shape=[jax.ShapeDtypeStruct((H, S, dn), F32), jax.ShapeDtypeStruct((H, S, dv), F32),
                   jax.ShapeDtypeStruct((S, R), F32)],
        scratch_shapes=[pltpu.VMEM((tk, dn), F32), pltpu.VMEM((tk, dv), F32), pltpu.VMEM((tk, R), F32)],
        compiler_params=_cparams(("parallel", "arbitrary", "arbitrary")), name=name)(qn, qp, kn, kp, v, lse, delta, do)


def _loss_head(y, target, name):
    S, D = y.shape
    tr = _row_tile(S, D)
    row = pl.BlockSpec((tr, D), lambda i: (i, 0))
    vec = pl.BlockSpec((1, D), lambda i: (0, 0))

    def body(y_ref, t_ref, sq_ref, dy_ref):
        @pl.when(pl.program_id(0) == 0)
        def _():
            sq_ref[...] = jnp.zeros_like(sq_ref)

        e = y_ref[...] - t_ref[...]
        sq_ref[...] += jnp.sum(e * e, axis=0, keepdims=True)
        dy_ref[...] = e / D

    return pl.pallas_call(body, grid=(S // tr,), in_specs=[row, row], out_specs=[vec, row],
                          out_shape=[jax.ShapeDtypeStruct((1, D), F32), jax.ShapeDtypeStruct((S, D), F32)],
                          compiler_params=_cparams(("arbitrary",)), name=name)(y, target)


def _sum_chips(recv, name):
    _, R, C = recv.shape
    tr = _row_tile(R, C)

    def body(r_ref, o_ref):
        acc = r_ref[0].astype(F32)
        for i in range(1, N_CHIPS):
            acc = acc + r_ref[i].astype(F32)
        o_ref[...] = acc

    return pl.pallas_call(body, grid=(R // tr,), in_specs=[pl.BlockSpec((N_CHIPS, tr, C), lambda i: (0, i, 0))],
                          out_specs=pl.BlockSpec((tr, C), lambda i: (i, 0)), out_shape=jax.ShapeDtypeStruct((R, C), F32),
                          compiler_params=_cparams(("parallel",)), name=name)(recv)


def _adamw(w, m, v, gs, name):
    R, C = w.shape
    tr = _row_tile(R, 2 * C)
    row = pl.BlockSpec((tr, C), lambda i: (i, 0))
    n = len(gs)

    def body(*refs):
        w_ref, m_ref, v_ref = refs[:3]
        g_ref, d_ref, nm_ref, nv_ref = refs[3 + n:]
        g = refs[3][...]
        for r in refs[4:3 + n]:
            g = g + r[...]
        mm = ADAM_B1 * m_ref[...] + (1.0 - ADAM_B1) * g
        vv = ADAM_B2 * v_ref[...] + (1.0 - ADAM_B2) * (g * g)
        m_hat = mm / (1.0 - ADAM_B1 ** ADAM_STEP)
        v_hat = vv / (1.0 - ADAM_B2 ** ADAM_STEP)
        g_ref[...] = g
        d_ref[...] = -ADAM_LR * (m_hat / (jnp.sqrt(v_hat) + ADAM_EPS) + ADAM_WD * w_ref[...])
        nm_ref[...] = mm
        nv_ref[...] = vv

    sds = jax.ShapeDtypeStruct((R, C), F32)
    return pl.pallas_call(body, grid=(R // tr,), in_specs=[row] * (3 + n), out_specs=[row] * 4, out_shape=[sds] * 4,
                          compiler_params=_cparams(("parallel",)), name=name)(w, m, v, *gs)


def _chip_peers():
    x, y, c = lax.axis_index("x"), lax.axis_index("y"), lax.axis_index("c")
    others = [(1 - x, y), (x, 1 - y), (1 - x, 1 - y)]
    return x, y, c, 2 * x + y, [(px, py, 2 * px + py) for px, py in others]


def _gather_chips(items, name):
    srcs, outs, plan = [], [], []
    for it in items:
        arr, layered = it if isinstance(it, tuple) else (it, False)
        srcs.append(arr)
        for l in (range(arr.shape[0]) if layered else [None]):
            shape = arr.shape[1:] if layered else arr.shape
            outs.append(jax.ShapeDtypeStruct((N_CHIPS,) + tuple(shape), arr.dtype))
            plan.append((len(srcs) - 1, l))
    ns, no = len(srcs), len(outs)

    def body(*refs):
        src_refs, out_refs = refs[:ns], refs[ns:ns + no]
        send_sems, recv_sems, local_sems = refs[ns + no:]
        x, y, c, me, peers = _chip_peers()
        local, remote = [], []
        for t, (si, l) in enumerate(plan):
            src = src_refs[si] if l is None else src_refs[si].at[l]
            mine = pltpu.make_async_copy(src, out_refs[t].at[me], local_sems.at[t])
            mine.start()
            local.append(mine)
            for j, (px, py, _) in enumerate(peers):
                cp = pltpu.make_async_remote_copy(src_ref=src, dst_ref=out_refs[t].at[me], send_sem=send_sems.at[3 * t + j],
                                                  recv_sem=recv_sems.at[3 * t + j], device_id=(px, py, c), device_id_type=MESH)
                cp.start()
                remote.append(cp)
        for t, (si, l) in enumerate(plan):
            src = src_refs[si] if l is None else src_refs[si].at[l]
            for j, (px, py, pi) in enumerate(peers):
                pltpu.make_async_remote_copy(src_ref=src, dst_ref=out_refs[t].at[pi], send_sem=send_sems.at[3 * t + j],
                                             recv_sem=recv_sems.at[3 * t + j], device_id=(px, py, c),
                                             device_id_type=MESH).wait_recv()
        for cp in remote:
            cp.wait_send()
        for cp in local:
            cp.wait()

    hbm = pl.BlockSpec(memory_space=pl.ANY)
    return pl.pallas_call(
        body, in_specs=[hbm] * ns, out_specs=[hbm] * no, out_shape=outs,
        scratch_shapes=[pltpu.SemaphoreType.DMA((3 * no,)), pltpu.SemaphoreType.DMA((3 * no,)), pltpu.SemaphoreType.DMA((no,))],
        name=name)(*srcs)


def _scatter_chips(items, name):
    srcs, outs, plan = [], [], []
    for layers in items:
        shard = layers[0].shape[1:]
        outs.append(jax.ShapeDtypeStruct((N_CHIPS, len(layers)) + tuple(shard), layers[0].dtype))
        for l, arr in enumerate(layers):
            srcs.append(arr)
            plan.append((len(srcs) - 1, len(outs) - 1, l))
    ns, no = len(srcs), len(outs)

    def body(*refs):
        src_refs, out_refs = refs[:ns], refs[ns:ns + no]
        send_sems, recv_sems, local_sems = refs[ns + no:]
        x, y, c, me, peers = _chip_peers()
        local, remote = [], []
        for t, (si, oi, l) in enumerate(plan):
            mine = pltpu.make_async_copy(src_refs[si].at[me], out_refs[oi].at[me, l], local_sems.at[t])
            mine.start()
            local.append(mine)
            for j, (px, py, pi) in enumerate(peers):
                cp = pltpu.make_async_remote_copy(src_ref=src_refs[si].at[pi], dst_ref=out_refs[oi].at[me, l],
                                                  send_sem=send_sems.at[3 * t + j], recv_sem=recv_sems.at[3 * t + j],
                                                  device_id=(px, py, c), device_id_type=MESH)
                cp.start()
                remote.append(cp)
        for t, (si, oi, l) in enumerate(plan):
            for j, (px, py, pi) in enumerate(peers):
                pltpu.make_async_remote_copy(src_ref=src_refs[si].at[pi], dst_ref=out_refs[oi].at[pi, l],
                                             send_sem=send_sems.at[3 * t + j], recv_sem=recv_sems.at[3 * t + j],
                                             device_id=(px, py, c), device_id_type=MESH).wait_recv()
        for cp in remote:
            cp.wait_send()
        for cp in local:
            cp.wait()

    hbm = pl.BlockSpec(memory_space=pl.ANY)
    return pl.pallas_call(
        body, in_specs=[hbm] * ns, out_specs=[hbm] * no, out_shape=outs,
        scratch_shapes=[pltpu.SemaphoreType.DMA((3 * ns,)), pltpu.SemaphoreType.DMA((3 * ns,)), pltpu.SemaphoreType.DMA((ns,))],
        name=name)(*srcs)


def _swap_cores(arrs, name):
    n = len(arrs)

    def body(*refs):
        src_refs, out_refs = refs[:n], refs[n:2 * n]
        send_sems, recv_sems = refs[2 * n:]
        x, y, c = lax.axis_index("x"), lax.axis_index("y"), lax.axis_index("c")
        cps = []
        for t in range(n):
            cp = pltpu.make_async_remote_copy(src_ref=src_refs[t], dst_ref=out_refs[t], send_sem=send_sems.at[t],
                                              recv_sem=recv_sems.at[t], device_id=(x, y, 1 - c), device_id_type=MESH)
            cp.start()
            cps.append(cp)
        for cp in cps:
            cp.wait()

    hbm = pl.BlockSpec(memory_space=pl.ANY)
    return pl.pallas_call(
        body, in_specs=[hbm] * n, out_specs=[hbm] * n, out_shape=[jax.ShapeDtypeStruct(a.shape, a.dtype) for a in arrs],
        scratch_shapes=[pltpu.SemaphoreType.DMA((n,)), pltpu.SemaphoreType.DMA((n,))],
        name=name)(*arrs)


def _allreduce_small(buf, name):
    R, C = buf.shape

    def body(b_ref, o_ref, recv, send_sems, recv_sems):
        x, y, c = lax.axis_index("x"), lax.axis_index("y"), lax.axis_index("c")
        me = 4 * x + 2 * y + c
        recv[me] = b_ref[...]
        cps = []
        for mask in range(1, N_DEV):
            px, py, pc = x ^ (mask >> 2 & 1), y ^ (mask >> 1 & 1), c ^ (mask & 1)
            cp = pltpu.make_async_remote_copy(src_ref=b_ref, dst_ref=recv.at[me], send_sem=send_sems.at[mask - 1],
                                              recv_sem=recv_sems.at[mask - 1], device_id=(px, py, pc), device_id_type=MESH)
            cp.start()
            cps.append((cp, 4 * px + 2 * py + pc, (px, py, pc), mask))
        for cp, pi, dev, mask in cps:
            pltpu.make_async_remote_copy(src_ref=b_ref, dst_ref=recv.at[pi], send_sem=send_sems.at[mask - 1],
                                         recv_sem=recv_sems.at[mask - 1], device_id=dev, device_id_type=MESH).wait_recv()
        for cp, _, _, _ in cps:
            cp.wait_send()
        acc = recv[0]
        for i in range(1, N_DEV):
            acc = acc + recv[i]
        o_ref[...] = acc

    vm = pl.BlockSpec(memory_space=pltpu.VMEM)
    return pl.pallas_call(
        body, in_specs=[vm], out_specs=vm, out_shape=jax.ShapeDtypeStruct((R, C), F32),
        scratch_shapes=[pltpu.VMEM((N_DEV, R, C), F32), pltpu.SemaphoreType.DMA((N_DEV - 1,)),
                        pltpu.SemaphoreType.DMA((N_DEV - 1,))],
        compiler_params=pltpu.CompilerParams(vmem_limit_bytes=VMEM_LIMIT), name=name)(buf)


def _pack(arrs):
    parts, offs, n = [], [], 0
    for a in arrs:
        f = a.reshape(-1).astype(F32)
        k = -(-f.shape[0] // LANES) * LANES
        parts.append(jnp.pad(f, (0, k - f.shape[0])))
        offs.append(n)
        n += k
    total = -(-n // (8 * LANES)) * (8 * LANES)
    if total > n:
        parts.append(jnp.zeros((total - n,), F32))
    return jnp.concatenate(parts).reshape(-1, LANES), offs


def _unpack(buf, offs, shapes):
    flat = buf.reshape(-1)
    return [flat[o:o + int(np.prod(s))].reshape(s) for o, s in zip(offs, shapes)]


def _ffn_fwd(xm, g, w_in, cw, cb, w_out, t):
    h2 = _rms_fwd(xm, g, BF16, f"ffn_norm{t}")
    u = _mm_nn(h2, w_in, F32, f"ffn_in{t}", out_panels=2)
    act = _glu_fwd(u, cw, cb, f"glu{t}")
    xo = _mm_nn(act, w_out, F32, f"ffn_out{t}", res=xm)
    return xo, (h2, u, act)


def _ffn_bwd(dxo, xm, g, w_in, cw, cb, w_out, saved, t):
    h2, u, act = saved
    dact = _mm_nt(dxo, w_out, F32, f"ffn_dact{t}")
    dw_out = _mm_tn(act, dxo, BF16, f"ffn_dwout{t}")
    du, dcw, dcb = _glu_bwd(u, cw, cb, dact, f"glu_bwd{t}")
    dh2 = _mm_nt(du, w_in, F32, f"ffn_dh{t}")
    dw_in = _mm_tn(h2, du, BF16, f"ffn_dwin{t}", out_panels=N_CHIPS)
    dxm, dg = _rms_bwd(xm, g, dh2, f"ffn_norm_bwd{t}", res=dxo)
    return dxm, dict(ffn_w_in=dw_in, ffn_conv_w=dcw, ffn_conv_b=dcb, ffn_w_out=dw_out, norm_ffn_g=dg)


def _pool_fwd(x, g, wg, scale, t):
    G = len(POOL_WINDOWS)
    h = _rms_fwd(x, g, F32, f"pool_norm{t}")
    pooled = _pool_win(h, True, BF16, f"pool_win{t}")
    yraw = jnp.concatenate([_mm_nn(pooled[i], wg[i], F32, f"pool_mm{t}_{i}") for i in range(G)], axis=1)
    xm = _scale_res(x, yraw, scale, f"pool_out{t}")
    return xm, (pooled, yraw)


def _pool_bwd(dxm, x, g, wg, scale, saved, t):
    G = len(POOL_WINDOWS)
    pooled, yraw = saved
    dg_ = x.shape[1] // G
    dyraw, dscale = _scale_bwd(dxm, yraw, scale, f"pool_out_bwd{t}")
    parts = [dyraw[:, i * dg_:(i + 1) * dg_] for i in range(G)]
    dpool = jnp.concatenate([_mm_nt(parts[i], wg[i], F32, f"pool_dp{t}_{i}") for i in range(G)], axis=1)
    dw = jnp.stack([_mm_tn(pooled[i], parts[i], BF16, f"pool_dw{t}_{i}") for i in range(G)])
    dh = jnp.concatenate(_pool_win(dpool, False, F32, f"pool_win_bwd{t}"), axis=1)
    dx, dgn = _rms_bwd(x, g, dh, f"pool_norm_bwd{t}", res=dxm)
    return dx, dict(pool_w=dw, pool_scale=dscale, norm_mix_g=dgn)


def _swa_fwd(x, g, w_qkv, q_gain, k_gain, sinks, w_o, pos, t):
    S = x.shape[0]
    Hq, Hkv, hd = SWA_HEADS, SWA_KV_HEADS, SWA_HEAD_DIM
    nq, nkv = Hq * hd, Hkv * hd
    posq, posk, slopes = pos
    h = _rms_fwd(x, g, BF16, f"swa_norm{t}")
    qkv = _mm_nn(h, w_qkv, F32, f"swa_qkv{t}")
    q = qkv[:, :nq].reshape(S * Hq, hd)
    k = qkv[:, nq:nq + nkv].reshape(S * Hkv, hd)
    qn = _rms_fwd(q, q_gain, BF16, f"swa_qnorm{t}", scale=hd ** -0.5).reshape(S, nq)
    kT = _rms_fwd(k, k_gain, BF16, f"swa_knorm{t}").reshape(S, Hkv, hd).transpose(1, 0, 2)
    vT = qkv[:, nq + nkv:].astype(BF16).reshape(S, Hkv, hd).transpose(1, 0, 2)
    o, lse = _swa_fwd_attn(qn, kT, vT, posq, posk, slopes, sinks.reshape(-1), f"swa_attn{t}")
    xm = _mm_nn(o, w_o, F32, f"swa_o{t}", res=x)
    return xm, (h, q, k, qn, kT, vT, o, lse)


def _swa_bwd(dxm, x, g, w_qkv, q_gain, k_gain, sinks, w_o, pos, saved, t):
    S = x.shape[0]
    Hq, Hkv, hd = SWA_HEADS, SWA_KV_HEADS, SWA_HEAD_DIM
    nq, nkv = Hq * hd, Hkv * hd
    posq, posk, slopes = pos
    h, q, k, qn, kT, vT, o, lse = saved
    do = _mm_nt(dxm, w_o, F32, f"swa_do{t}")
    dw_o = _mm_tn(o, dxm, BF16, f"swa_dwo{t}")
    dqn, dkp, dvp, dsink = _swa_bwd_attn(qn, kT, vT, posq, posk, slopes, sinks.reshape(-1), o, lse, do, f"swa_attn_bwd{t}")
    dkn = _overlap3(dkp, f"swa_dk{t}").transpose(1, 0, 2).reshape(S * Hkv, hd)
    dv = _overlap3(dvp, f"swa_dv{t}").transpose(1, 0, 2).reshape(S, nkv)
    dq, dqg = _rms_bwd(q, q_gain, dqn.reshape(S * Hq, hd), f"swa_qnorm_bwd{t}", scale=hd ** -0.5)
    dk, dkg = _rms_bwd(k, k_gain, dkn, f"swa_knorm_bwd{t}")
    dqkv = jnp.concatenate([dq.reshape(S, nq), dk.reshape(S, nkv), dv], axis=1)
    dh = _mm_nt(dqkv, w_qkv, F32, f"swa_dh{t}")
    dw_qkv = _mm_tn(h, dqkv, BF16, f"swa_dwqkv{t}", out_panels=N_CHIPS)
    dx, dgn = _rms_bwd(x, g, dh, f"swa_norm_bwd{t}", res=dxm)
    return dx, dict(swa_w_qkv=dw_qkv, swa_q_gain=dqg, swa_k_gain=dkg, swa_sinks=dsink.reshape(1, Hq), swa_w_o=dw_o,
                    norm_mix_g=dgn)


def _mla_fwd(x, g, w_down, q_a_gain, kv_a_gain, w_uq, w_ukv, qn_gain, qr_gain, kn_gain, kr_gain, w_o, rope, t):
    S = x.shape[0]
    H, dn, R, dv, qr_, kvr = MLA_HEADS, MLA_NOPE, MLA_ROPE, MLA_V, MLA_Q_RANK, MLA_KV_RANK
    sc = (dn + R) ** -0.5
    h = _rms_fwd(x, g, BF16, f"mla_norm{t}")
    d = _mm_nn(h, w_down, F32, f"mla_down{t}")
    cq_pre, ckv_pre, kpe_pre = d[:, :qr_], d[:, qr_:qr_ + kvr], d[:, qr_ + kvr:]
    cq = _rms_fwd(cq_pre, q_a_gain, BF16, f"mla_cq{t}")
    ckv = _rms_fwd(ckv_pre, kv_a_gain, BF16, f"mla_ckv{t}")
    q3 = _mm_nn(cq, w_uq, F32, f"mla_uq{t}").reshape(S, H, dn + R).transpose(1, 0, 2)
    kv3 = _mm_nn(ckv, w_ukv, F32, f"mla_ukv{t}").reshape(S, H, dn + dv).transpose(1, 0, 2)
    qn_in, qp_in = q3[..., :dn].reshape(H * S, dn), q3[..., dn:].reshape(H * S, R)
    kn_in, v = kv3[..., :dn].reshape(H * S, dn), kv3[..., dn:].astype(BF16)
    qn = _rms_fwd(qn_in, qn_gain, BF16, f"mla_qn{t}", scale=sc).reshape(H, S, dn)
    qp = _rms_fwd(qp_in, qr_gain, BF16, f"mla_qp{t}", scale=sc, rope=rope).reshape(H, S, R)
    kn = _rms_fwd(kn_in, kn_gain, BF16, f"mla_kn{t}").reshape(H, S, dn)
    kp = _rms_fwd(kpe_pre, kr_gain, BF16, f"mla_kp{t}", rope=rope)
    o, lse = _mla_fwd_attn(qn, qp, kn, kp, v, f"mla_attn{t}")
    o2 = o.transpose(1, 0, 2).reshape(S, H * dv)
    xm = _mm_nn(o2, w_o, F32, f"mla_o{t}", res=x)
    return xm, (h, cq_pre, ckv_pre, kpe_pre, cq, ckv, qn_in, qp_in, kn_in, v, qn, qp, kn, kp, o, lse, o2)


def _mla_bwd(dxm, x, g, w_down, q_a_gain, kv_a_gain, w_uq, w_ukv, qn_gain, qr_gain, kn_gain, kr_gain, w_o, rope,
             saved, t):
    S = x.shape[0]
    H, dn, R, dv = MLA_HEADS, MLA_NOPE, MLA_ROPE, MLA_V
    sc = (dn + R) ** -0.5
    h, cq_pre, ckv_pre, kpe_pre, cq, ckv, qn_in, qp_in, kn_in, v, qn, qp, kn, kp, o, lse, o2 = saved
    do2 = _mm_nt(dxm, w_o, F32, f"mla_do{t}")
    dw_o = _mm_tn(o2, dxm, BF16, f"mla_dwo{t}")
    do = do2.reshape(S, H, dv).transpose(1, 0, 2)
    dqn, dqp, delta = _mla_bwd_q(qn, qp, kn, kp, v, o, lse, do, f"mla_attn_dq{t}")
    dkn, dvv, dkp = _mla_bwd_kv(qn, qp, kn, kp, v, lse, delta, do, f"mla_attn_dkv{t}")
    dqn_in, dqng = _rms_bwd(qn_in, qn_gain, dqn.reshape(H * S, dn), f"mla_qn_bwd{t}", scale=sc)
    dqp_in, dqrg = _rms_bwd(qp_in, qr_gain, dqp.reshape(H * S, R), f"mla_qp_bwd{t}", scale=sc, rope=rope)
    dkn_in, dkng = _rms_bwd(kn_in, kn_gain, dkn.reshape(H * S, dn), f"mla_kn_bwd{t}")
    dkpe_pre, dkrg = _rms_bwd(kpe_pre, kr_gain, dkp, f"mla_kp_bwd{t}", rope=rope)
    dq = jnp.concatenate([dqn_in.reshape(H, S, dn), dqp_in.reshape(H, S, R)], axis=-1).transpose(1, 0, 2).reshape(S, -1)
    dkv = jnp.concatenate([dkn_in.reshape(H, S, dn), dvv], axis=-1).transpose(1, 0, 2).reshape(S, -1)
    dcq = _mm_nt(dq, w_uq, F32, f"mla_dcq{t}")
    dw_uq = _mm_tn(cq, dq, BF16, f"mla_dwuq{t}", out_panels=N_CHIPS)
    dckv = _mm_nt(dkv, w_ukv, F32, f"mla_dckv{t}")
    dw_ukv = _mm_tn(ckv, dkv, BF16, f"mla_dwukv{t}", out_panels=N_CHIPS)
    dcq_pre, dqag = _rms_bwd(cq_pre, q_a_gain, dcq, f"mla_cq_bwd{t}")
    dckv_pre, dkvag = _rms_bwd(ckv_pre, kv_a_gain, dckv, f"mla_ckv_bwd{t}")
    dd = jnp.concatenate([dcq_pre, dckv_pre, dkpe_pre], axis=1)
    dh = _mm_nt(dd, w_down, F32, f"mla_dh{t}")
    dw_down = _mm_tn(h, dd, BF16, f"mla_dwdown{t}")
    dx, dgn = _rms_bwd(x, g, dh, f"mla_norm_bwd{t}", res=dxm)
    return dx, dict(mla_w_down=dw_down, mla_q_a_gain=dqag, mla_kv_a_gain=dkvag, mla_w_uq=dw_uq, mla_w_ukv=dw_ukv,
                    mla_qn_gain=dqng, mla_qr_gain=dqrg, mla_kn_gain=dkng, mla_kr_gain=dkrg, mla_w_o=dw_o, norm_mix_g=dgn)


def _chips_to_axis(gathered, axis):
    moved = jnp.moveaxis(gathered, 0, axis)
    shape = list(moved.shape)
    shape[axis:axis + 2] = [shape[axis] * shape[axis + 1]]
    return moved.reshape(shape)


def kernel(x, positions, norm_mix_g, norm_ffn_g, pool_w, pool_scale, swa_w_qkv, swa_q_gain, swa_k_gain, swa_sinks, swa_w_o, mla_w_down, mla_q_a_gain, mla_kv_a_gain, mla_w_uq, mla_w_ukv, mla_qn_gain, mla_qr_gain, mla_kn_gain, mla_kr_gain, mla_w_o, ffn_w_in, ffn_conv_w, ffn_conv_b, ffn_w_out, loss_target, m_norm_mix_g, m_norm_ffn_g, m_pool_w, m_pool_scale, m_swa_w_qkv, m_swa_q_gain, m_swa_k_gain, m_swa_sinks, m_swa_w_o, m_mla_w_down, m_mla_q_a_gain, m_mla_kv_a_gain, m_mla_w_uq, m_mla_w_ukv, m_mla_qn_gain, m_mla_qr_gain, m_mla_kn_gain, m_mla_kr_gain, m_mla_w_o, m_ffn_w_in, m_ffn_conv_w, m_ffn_conv_b, m_ffn_w_out, v_norm_mix_g, v_norm_ffn_g, v_pool_w, v_pool_scale, v_swa_w_qkv, v_swa_q_gain, v_swa_k_gain, v_swa_sinks, v_swa_w_o, v_mla_w_down, v_mla_q_a_gain, v_mla_kv_a_gain, v_mla_w_uq, v_mla_w_ukv, v_mla_qn_gain, v_mla_qr_gain, v_mla_kn_gain, v_mla_kr_gain, v_mla_w_o, v_ffn_w_in, v_ffn_conv_w, v_ffn_conv_b, v_ffn_w_out):
    args = dict(locals())
    W = {n: args[n] for n in WEIGHTS}
    M = {n: args["m_" + n] for n in WEIGHTS}
    V = {n: args["v_" + n] for n in WEIGHTS}
    xs = x[0]
    S, D = xs.shape
    chip = 2 * lax.axis_index("x") + lax.axis_index("y")

    big = ['pool_w', 'swa_w_qkv', 'swa_w_o', 'mla_w_down', 'mla_w_uq', 'mla_w_ukv', 'mla_w_o', 'ffn_w_in', 'ffn_w_out']
    small_sharded = [n for n in WEIGHTS if SMALL.get(n) is not None]
    gathered = _gather_chips([(W[n].astype(BF16), True) for n in big] + [W[n] for n in small_sharded], "gather_weights")
    Wg = {}
    pos_ = 0
    for n in big:
        L = W[n].shape[0]
        Wg[n] = gathered[pos_:pos_ + L]
        pos_ += L
    full = dict(W)
    for n in small_sharded:
        full[n] = _chips_to_axis(gathered[pos_], SMALL[n])
        pos_ += 1
    rows = lambda a: a.reshape((a.shape[0] * a.shape[1],) + a.shape[2:])
    pool_wg = [[rows(jnp.swapaxes(a, 0, 1)[i]) for i in range(len(POOL_WINDOWS))] for a in Wg['pool_w']]
    swa_wo = [rows(a) for a in Wg['swa_w_o']]
    mla_wdown = [rows(a) for a in Wg['mla_w_down']]
    mla_wo = [rows(a) for a in Wg['mla_w_o']]
    ffn_wout = [rows(a) for a in Wg['ffn_w_out']]

    posf = positions.astype(F32)
    slopes = jnp.asarray(2.0 ** (-8.0 * np.arange(1, SWA_HEADS + 1) / SWA_HEADS), dtype=F32)
    pos = (posf.reshape(S, 1), posf.reshape(1, S), slopes)
    inv = ROPE_THETA ** (-jnp.arange(0, MLA_ROPE, 2, dtype=F32) / MLA_ROPE)
    ang = posf[:, None] * inv[None, :]
    cos, sin = jnp.cos(ang), jnp.sin(ang)
    rope = (jnp.concatenate([cos, cos], axis=1), jnp.concatenate([-sin, sin], axis=1))

    def mixer_args(i):
        kind, j = i % 3, i // 3
        g = full['norm_mix_g'][i:i + 1]
        if kind == 0:
            return kind, (g, pool_wg[j], full['pool_scale'][j:j + 1])
        if kind == 1:
            return kind, (g, Wg['swa_w_qkv'][j], full['swa_q_gain'][j:j + 1], full['swa_k_gain'][j:j + 1],
                          full['swa_sinks'][j:j + 1], swa_wo[j], pos)
        return kind, (g, mla_wdown[j], full['mla_q_a_gain'][j:j + 1], full['mla_kv_a_gain'][j:j + 1], Wg['mla_w_uq'][j],
                      Wg['mla_w_ukv'][j], full['mla_qn_gain'][j:j + 1], full['mla_qr_gain'][j:j + 1],
                      full['mla_kn_gain'][j:j + 1], full['mla_kr_gain'][j:j + 1], mla_wo[j], rope)

    def ffn_args(i):
        return (full['norm_ffn_g'][i:i + 1], Wg['ffn_w_in'][i], full['ffn_conv_w'][i], full['ffn_conv_b'][i:i + 1],
                ffn_wout[i])

    fwd = (_pool_fwd, _swa_fwd, _mla_fwd)
    bwd = (_pool_bwd, _swa_bwd, _mla_bwd)
    tape = []
    cur = xs
    for i in range(DEPTH):
        kind, margs = mixer_args(i)
        xm, msaved = fwd[kind](cur, *margs, i)
        xo, fsaved = _ffn_fwd(xm, *ffn_args(i), i)
        tape.append((cur, xm, msaved, fsaved))
        cur = xo
    sq, dcur = _loss_head(cur, loss_target[0], "loss_head")
    loss = lax.psum(0.5 / D * jnp.sum(sq), ("x", "y", "c"))

    grads = {n: [None] * W[n].shape[0] for n in WEIGHTS}
    for i in reversed(range(DEPTH)):
        kind, margs = mixer_args(i)
        x_in, xm, msaved, fsaved = tape[i]
        dxm, fg = _ffn_bwd(dcur, xm, *ffn_args(i), fsaved, i)
        dcur, mg = bwd[kind](dxm, x_in, *margs, msaved, i)
        for n, gval in fg.items():
            grads[n][i] = gval
        for n, gval in mg.items():
            grads[n][i if n == 'norm_mix_g' else i // 3] = gval
    grad_x = dcur[None]

    def by_chip(n, gl):
        if n == 'pool_w':
            G, dg_ = gl.shape[0], gl.shape[1]
            return gl.reshape(G, N_CHIPS, dg_ // N_CHIPS, dg_).swapaxes(0, 1)
        if len(gl.shape) == 3:
            return gl
        return gl.reshape((N_CHIPS, gl.shape[0] // N_CHIPS) + gl.shape[1:])

    recvs = _scatter_chips([[by_chip(n, gl) for gl in grads[n]] for n in big], "scatter_grads")
    partial = []
    for n, r in zip(big, recvs):
        C = r.shape[-1]
        partial.append(_sum_chips(r.reshape(N_CHIPS, -1, C), f"sum_{n}"))
    other = _swap_cores(partial, "swap_cores")
    out = {}
    for n, p, q in zip(big, partial, other):
        C = p.shape[1]
        res = _adamw(W[n].reshape(-1, C), M[n].reshape(-1, C), V[n].reshape(-1, C), [p, q], f"adamw_{n}")
        out[n] = [r.reshape(W[n].shape) for r in res]

    small = [n for n in WEIGHTS if n in SMALL]
    full_shapes = [tuple(full[n].shape) for n in small]
    sg = []
    for n in small:
        parts = grads[n]
        if n == 'ffn_conv_w':
            sg.append(jnp.stack(parts))
        else:
            sg.append(jnp.concatenate(parts, axis=0))
    buf, offs = _pack(sg)
    summed = _unpack(_allreduce_small(buf, "allreduce_small"), offs, full_shapes)
    gsm = []
    for n, gfull in zip(small, summed):
        ax = SMALL[n]
        if ax is not None:
            size = W[n].shape[ax]
            gfull = lax.dynamic_slice_in_dim(gfull, chip * size, size, axis=ax)
        gsm.append(gfull)
    gb, goffs = _pack(gsm)
    wb, _ = _pack([W[n] for n in small])
    mb, _ = _pack([M[n] for n in small])
    vb, _ = _pack([V[n] for n in small])
    res = _adamw(wb, mb, vb, [gb], "adamw_small")
    shapes = [tuple(W[n].shape) for n in small]
    unp = [_unpack(r, goffs, shapes) for r in res]
    for k, n in enumerate(small):
        out[n] = [u[k] for u in unp]

    return (loss, grad_x, *[out[n][0] for n in WEIGHTS], *[out[n][1] for n in WEIGHTS],
            *[out[n][2] for n in WEIGHTS], *[out[n][3] for n in WEIGHTS])
```

```python
import functools
import math

import numpy as np
import jax
import jax.numpy as jnp
from jax import lax
from jax.experimental import pallas as pl
from jax.experimental.pallas import tpu as pltpu

F32 = jnp.float32
BF16 = jnp.bfloat16

D_MODEL = 2048
SEQ = 4096
DEPTH = 4
EPS = 1e-6
POOL_WINDOWS = (2, 4, 8, 16)
SWA_HEADS = 32
SWA_KV_HEADS = 4
SWA_HEAD_DIM = 64
SWA_WINDOW = 128
MLA_HEADS = 16
MLA_NOPE = 128
MLA_ROPE = 64
MLA_V = 128
MLA_Q_RANK = 512
MLA_KV_RANK = 512
ROPE_THETA = 10000.0
D_FF = 5632
ADAM_LR = 0.001
ADAM_B1 = 0.9
ADAM_B2 = 0.999
ADAM_EPS = 1e-08
ADAM_WD = 0.01
ADAM_STEP = 10

N_CHIPS = 4
N_DEV = 8
MESH = pl.DeviceIdType.MESH
VMEM_LIMIT = 48 << 20
LANES = 128
NEG = -1e30

WEIGHTS = ['norm_mix_g', 'norm_ffn_g', 'pool_w', 'pool_scale', 'swa_w_qkv', 'swa_q_gain', 'swa_k_gain', 'swa_sinks',
           'swa_w_o', 'mla_w_down', 'mla_q_a_gain', 'mla_kv_a_gain', 'mla_w_uq', 'mla_w_ukv', 'mla_qn_gain',
           'mla_qr_gain', 'mla_kn_gain', 'mla_kr_gain', 'mla_w_o', 'ffn_w_in', 'ffn_conv_w', 'ffn_conv_b', 'ffn_w_out']
SMALL = {'norm_mix_g': None, 'norm_ffn_g': None, 'pool_scale': 1, 'swa_q_gain': None, 'swa_k_gain': None,
         'swa_sinks': None, 'mla_q_a_gain': 1, 'mla_kv_a_gain': 1, 'mla_qn_gain': None, 'mla_qr_gain': None,
         'mla_kn_gain': None, 'mla_kr_gain': None, 'ffn_conv_w': 2, 'ffn_conv_b': None}


def _cparams(sem):
    return pltpu.CompilerParams(dimension_semantics=sem, vmem_limit_bytes=VMEM_LIMIT)


def _tile(n, cands):
    for c in cands:
        if c <= n and n % c == 0:
            return c
    return n


WIDE = (1408, 1024, 768, 640, 512, 384, 256, 128)


def _dims(arr):
    if len(arr.shape) == 2:
        return arr.shape[0], arr.shape[1], arr.shape[1]
    return arr.shape[1], arr.shape[0] * arr.shape[2], arr.shape[2]


def _pspec(arr, tr, tc, fn):
    if len(arr.shape) == 2:
        return pl.BlockSpec((tr, tc), fn)
    per = arr.shape[2] // tc

    def im(*g):
        r, c = fn(*g)
        return (c // per, r, c % per)

    return pl.BlockSpec((None, tr, tc), im)


class _Comm:
    def __init__(self, ins, outs, aliases, sems, start, finish):
        self.ins, self.outs, self.aliases, self.sems, self.start, self.finish = ins, outs, aliases, sems, start, finish


def _call(body, grid, sem, in_specs, out_specs, out_shape, scratch, args, name, comm=None):
    if comm is None:
        return pl.pallas_call(body, grid=grid, in_specs=in_specs, out_specs=out_specs, out_shape=out_shape,
                              scratch_shapes=scratch, compiler_params=_cparams(sem), name=name)(*args)
    ni, ns, nci, nco = len(args), len(scratch), len(comm.ins), len(comm.outs)

    def wrapped(*refs):
        core_in, c_in = refs[:ni], refs[ni:ni + nci]
        o_ref, c_out = refs[ni + nci], refs[ni + nci + 1:ni + nci + 1 + nco]
        rest = refs[ni + nci + 1 + nco:]
        ids = [pl.program_id(d) for d in range(len(grid))]
        first = functools.reduce(lambda p, q: p & q, [i == 0 for i in ids])
        last = functools.reduce(lambda p, q: p & q, [i == n - 1 for i, n in zip(ids, grid)])

        @pl.when(first)
        def _():
            comm.start(c_in, c_out, rest[ns:])

        body(*core_in, o_ref, *rest[:ns])

        @pl.when(last)
        def _():
            comm.finish(c_in, c_out, rest[ns:])

    hbm = pl.BlockSpec(memory_space=pl.ANY)
    res = pl.pallas_call(
        wrapped, grid=grid, in_specs=list(in_specs) + [hbm] * nci, out_specs=[out_specs] + [hbm] * nco,
        out_shape=[out_shape] + list(comm.outs), scratch_shapes=list(scratch) + list(comm.sems),
        input_output_aliases={ni + i: 1 + o for i, o in comm.aliases.items()},
        compiler_params=_cparams(("arbitrary",) * len(grid)), name=name)(*args, *comm.ins)
    return res[0], list(res[1:])


def _mm_nn(a, b, out_dtype, name, res=None, out_panels=1, comm=None):
    M, K, ka = _dims(a)
    K2, N, nb = _dims(b)
    assert K == K2
    no = N // out_panels
    tm = _tile(M, (1024, 512, 256, 128))
    tk = _tile(ka, (512, 256, 128))
    tn = _tile(math.gcd(nb, no), WIDE)
    nk = K // tk
    o_sds = jax.ShapeDtypeStruct((M, N) if out_panels == 1 else (out_panels, M, no), out_dtype)

    def body(*refs):
        a_ref, b_ref = refs[:2]
        o_ref, acc = refs[-2:]
        k = pl.program_id(2)

        @pl.when(k == 0)
        def _():
            acc[...] = jnp.zeros_like(acc)

        acc[...] += jnp.dot(a_ref[...].astype(BF16), b_ref[...].astype(BF16), preferred_element_type=F32)

        @pl.when(k == nk - 1)
        def _():
            r = acc[...]
            if res is not None:
                r = r + refs[2][...]
            o_ref[...] = r.astype(o_ref.dtype)

    in_specs = [_pspec(a, tm, tk, lambda i, j, k: (i, k)), _pspec(b, tk, tn, lambda i, j, k: (k, j))]
    args = [a, b]
    if res is not None:
        in_specs.append(_pspec(res, tm, tn, lambda i, j, k: (i, j)))
        args.append(res)
    return _call(body, (M // tm, N // tn, nk), ("parallel", "parallel", "arbitrary"), in_specs,
                 _pspec(o_sds, tm, tn, lambda i, j, k: (i, j)), o_sds, [pltpu.VMEM((tm, tn), F32)], args, name, comm)


def _mm_nt(a, b, out_dtype, name, comm=None):
    M, N, na = _dims(a)
    K, N2, nb = _dims(b)
    assert N == N2
    tm = _tile(M, (1024, 512, 256, 128))
    to = _tile(K, (1024, 512, 256, 128))
    tc = _tile(math.gcd(na, nb), WIDE)
    nc = N // tc

    def body(a_ref, b_ref, o_ref, acc):
        c = pl.program_id(2)

        @pl.when(c == 0)
        def _():
            acc[...] = jnp.zeros_like(acc)

        acc[...] += lax.dot_general(a_ref[...].astype(BF16), b_ref[...].astype(BF16), (((1,), (1,)), ((), ())),
                                    preferred_element_type=F32)

        @pl.when(c == nc - 1)
        def _():
            o_ref[...] = acc[...].astype(o_ref.dtype)

    return _call(body, (M // tm, K // to, nc), ("parallel", "parallel", "arbitrary"),
                 [_pspec(a, tm, tc, lambda i, j, c: (i, c)), _pspec(b, to, tc, lambda i, j, c: (j, c))],
                 pl.BlockSpec((tm, to), lambda i, j, c: (i, j)), jax.ShapeDtypeStruct((M, K), out_dtype),
                 [pltpu.VMEM((tm, to), F32)], [a, b], name, comm)


def _mm_tn(a, b, out_dtype, name, out_panels=1, res=None, comm=None):
    M, K, ka = _dims(a)
    M2, N, nb = _dims(b)
    assert M == M2
    no = N // out_panels
    tk = _tile(ka, (1024, 512, 256, 128))
    tn = _tile(math.gcd(nb, no), WIDE)
    tm = _tile(M, (512, 256, 128))
    nm = M // tm
    o_sds = jax.ShapeDtypeStruct((K, N) if out_panels == 1 else (out_panels, K, no), out_dtype)

    def body(*refs):
        a_ref, b_ref = refs[:2]
        o_ref, acc = refs[-2:]
        m = pl.program_id(2)

        @pl.when(m == 0)
        def _():
            acc[...] = jnp.zeros_like(acc)

        acc[...] += lax.dot_general(a_ref[...].astype(BF16), b_ref[...].astype(BF16), (((0,), (0,)), ((), ())),
                                    preferred_element_type=F32)

        @pl.when(m == nm - 1)
        def _():
            r = acc[...]
            if res is not None:
                r = r + refs[2][...]
            o_ref[...] = r.astype(o_ref.dtype)

    in_specs = [_pspec(a, tm, tk, lambda i, j, m: (m, i)), _pspec(b, tm, tn, lambda i, j, m: (m, j))]
    args = [a, b]
    if res is not None:
        in_specs.append(_pspec(res, tk, tn, lambda i, j, m: (i, j)))
        args.append(res)
    return _call(body, (K // tk, N // tn, nm), ("parallel", "parallel", "arbitrary"), in_specs,
                 _pspec(o_sds, tk, tn, lambda i, j, m: (i, j)), o_sds, [pltpu.VMEM((tk, tn), F32)], args, name, comm)


def _swap_halves(y):
    h = y.shape[-1] // 2
    return jnp.concatenate([y[:, h:], y[:, :h]], axis=1)


def _row_tile(R, d, limit=None):
    cap = max(8, (1 << 19) // d)
    cands = [c for c in (4096, 2048, 1024, 512, 256, 128, 64, 32, 16, 8) if c <= cap]
    if limit is not None:
        cands = [c for c in cands if limit % c == 0]
    return _tile(R, cands)


def _rms_fwd(x, g, out_dtype, name, scale=1.0, rope=None):
    R, d = x.shape
    tr = _row_tile(R, d, None if rope is None else rope[0].shape[0])

    def body(*refs):
        x_ref, g_ref = refs[:2]
        o_ref = refs[-1]
        xv = x_ref[...].astype(F32)
        y = xv * lax.rsqrt(jnp.mean(xv * xv, axis=-1, keepdims=True) + EPS)
        y = y * g_ref[...]
        if rope is not None:
            y = y * refs[2][...] + _swap_halves(y) * refs[3][...]
        if scale != 1.0:
            y = y * scale
        o_ref[...] = y.astype(o_ref.dtype)

    in_specs = [pl.BlockSpec((tr, d), lambda i: (i, 0)), pl.BlockSpec((1, d), lambda i: (0, 0))]
    args = [x, g]
    if rope is not None:
        nrt = rope[0].shape[0] // tr
        in_specs += [pl.BlockSpec((tr, d), lambda i: (i % nrt, 0))] * 2
        args += list(rope)
    return pl.pallas_call(
        body, grid=(R // tr,), in_specs=in_specs, out_specs=pl.BlockSpec((tr, d), lambda i: (i, 0)),
        out_shape=jax.ShapeDtypeStruct((R, d), out_dtype), compiler_params=_cparams(("parallel",)), name=name)(*args)


def _rms_bwd(x, g, dy, name, scale=1.0, rope=None, res=None):
    R, d = x.shape
    tr = _row_tile(R, d, None if rope is None else rope[0].shape[0])

    def body(*refs):
        x_ref, g_ref, dy_ref = refs[:3]
        dx_ref, dg_ref = refs[-2:]
        i = pl.program_id(0)
        xv = x_ref[...].astype(F32)
        r = lax.rsqrt(jnp.mean(xv * xv, axis=-1, keepdims=True) + EPS)
        xhat = xv * r
        dyv = dy_ref[...].astype(F32)
        if scale != 1.0:
            dyv = dyv * scale
        if rope is not None:
            dyv = dyv * refs[3][...] + _swap_halves(dyv * refs[4][...])

        @pl.when(i == 0)
        def _():
            dg_ref[...] = jnp.zeros_like(dg_ref)

        dg_ref[...] += jnp.sum(dyv * xhat, axis=0, keepdims=True)
        dxh = dyv * g_ref[...]
        dx = r * (dxh - xhat * jnp.mean(dxh * xhat, axis=-1, keepdims=True))
        if res is not None:
            dx = dx + refs[-3][...]
        dx_ref[...] = dx

    row = pl.BlockSpec((tr, d), lambda i: (i, 0))
    vec = pl.BlockSpec((1, d), lambda i: (0, 0))
    in_specs = [row, vec, row]
    args = [x, g, dy]
    if rope is not None:
        nrt = rope[0].shape[0] // tr
        in_specs += [pl.BlockSpec((tr, d), lambda i: (i % nrt, 0))] * 2
        args += list(rope)
    if res is not None:
        in_specs.append(row)
        args.append(res)
    return pl.pallas_call(
        body, grid=(R // tr,), in_specs=in_specs, out_specs=[row, vec],
        out_shape=[jax.ShapeDtypeStruct((R, d), F32), jax.ShapeDtypeStruct((1, d), F32)],
        compiler_params=_cparams(("arbitrary",)), name=name)(*args)


PAD = 16


def _pool_win(x, fwd, out_dtype, name):
    S, D = x.shape
    G = len(POOL_WINDOWS)
    dg = D // G
    tc = _tile(dg, (128,))
    nt = dg // tc
    rc = _tile(S, (512,))

    def body(*refs):
        x_refs, o_refs, scr = refs[:G], refs[G:2 * G], refs[2 * G]
        zeros = jnp.zeros((PAD, tc), F32)
        scr[pl.ds(0, PAD), :] = zeros
        scr[pl.ds(PAD + S, PAD), :] = zeros
        for gi, w in enumerate(POOL_WINDOWS):
            left, right = w // 2, w - 1 - w // 2

            def count(r0):
                t = r0 + lax.broadcasted_iota(jnp.int32, (rc, 1), 0)
                return (jnp.minimum(t + right + 1, S) - jnp.maximum(t - left, 0)).astype(F32)

            for r0 in range(0, S, rc):
                xv = x_refs[gi][pl.ds(r0, rc), :]
                scr[pl.ds(PAD + r0, rc), :] = xv if fwd else xv / count(r0)
            lo, hi = (left, right) if fwd else (right, left)
            for r0 in range(0, S, rc):
                acc = scr[pl.ds(PAD + r0 - lo, rc), :]
                for o in range(-lo + 1, hi + 1):
                    acc = acc + scr[pl.ds(PAD + r0 + o, rc), :]
                xv = x_refs[gi][pl.ds(r0, rc), :]
                out = acc / count(r0) - xv if fwd else acc - xv
                o_refs[gi][pl.ds(r0, rc), :] = out.astype(out_dtype)

    in_specs = [pl.BlockSpec((S, tc), functools.partial(lambda j, gi: (0, gi * nt + j), gi=gi)) for gi in range(G)]
    out_specs = [pl.BlockSpec((S, tc), lambda j: (0, j)) for _ in range(G)]
    return pl.pallas_call(
        body, grid=(nt,), in_specs=in_specs, out_specs=out_specs,
        out_shape=[jax.ShapeDtypeStruct((S, dg), out_dtype) for _ in range(G)],
        scratch_shapes=[pltpu.VMEM((S + 2 * PAD, tc), F32)],
        compiler_params=_cparams(("parallel",)), name=name)(*([x] * G))


def _scale_res(x, y, scale, name):
    S, D = x.shape
    tr = _row_tile(S, D)
    row = pl.BlockSpec((tr, D), lambda i: (i, 0))

    def body(x_ref, y_ref, s_ref, o_ref):
        o_ref[...] = x_ref[...] + y_ref[...] * s_ref[...]

    return pl.pallas_call(body, grid=(S // tr,), in_specs=[row, row, pl.BlockSpec((1, D), lambda i: (0, 0))],
                          out_specs=row, out_shape=jax.ShapeDtypeStruct((S, D), F32),
                          compiler_params=_cparams(("parallel",)), name=name)(x, y, scale)


def _scale_bwd(dy, y, scale, name):
    S, D = dy.shape
    tr = _row_tile(S, D)
    row = pl.BlockSpec((tr, D), lambda i: (i, 0))
    vec = pl.BlockSpec((1, D), lambda i: (0, 0))

    def body(dy_ref, y_ref, s_ref, o_ref, ds_ref):
        @pl.when(pl.program_id(0) == 0)
        def _():
            ds_ref[...] = jnp.zeros_like(ds_ref)

        d = dy_ref[...]
        ds_ref[...] += jnp.sum(d * y_ref[...], axis=0, keepdims=True)
        o_ref[...] = (d * s_ref[...]).astype(BF16)

    return pl.pallas_call(body, grid=(S // tr,), in_specs=[row, row, vec], out_specs=[row, vec],
                          out_shape=[jax.ShapeDtypeStruct((S, D), BF16), jax.ShapeDtypeStruct((1, D), F32)],
                          compiler_params=_cparams(("arbitrary",)), name=name)(dy, y, scale)


GPAD = 8


def _sigmoid(z):
    return 1.0 / (1.0 + jnp.exp(-z))


def _glu_fwd(u, cw, cb, name):
    _, S, F = u.shape
    tc = _tile(F, (128,))
    rc = _tile(S, (512,))

    def body(u_ref, w_ref, b_ref, o_ref, scr):
        zeros = jnp.zeros((GPAD, tc), F32)
        scr[pl.ds(0, GPAD), :] = zeros
        scr[pl.ds(GPAD + S, GPAD), :] = zeros
        for r0 in range(0, S, rc):
            scr[pl.ds(GPAD + r0, rc), :] = u_ref[0, pl.ds(r0, rc), :]
        w0, w1, w2, b = w_ref[0:1, :], w_ref[1:2, :], w_ref[2:3, :], b_ref[...]
        for r0 in range(0, S, rc):
            gc = (scr[pl.ds(GPAD + r0 - 1, rc), :] * w0 + scr[pl.ds(GPAD + r0, rc), :] * w1
                  + scr[pl.ds(GPAD + r0 + 1, rc), :] * w2 + b)
            o_ref[pl.ds(r0, rc), :] = (gc * _sigmoid(gc) * u_ref[1, pl.ds(r0, rc), :]).astype(BF16)

    return pl.pallas_call(
        body, grid=(F // tc,),
        in_specs=[pl.BlockSpec((2, S, tc), lambda j: (0, 0, j)), pl.BlockSpec((3, tc), lambda j: (0, j)),
                  pl.BlockSpec((1, tc), lambda j: (0, j))],
        out_specs=pl.BlockSpec((S, tc), lambda j: (0, j)), out_shape=jax.ShapeDtypeStruct((S, F), BF16),
        scratch_shapes=[pltpu.VMEM((S + 2 * GPAD, tc), F32)],
        compiler_params=_cparams(("parallel",)), name=name)(u, cw, cb)


def _glu_bwd(u, cw, cb, dact, name):
    _, S, F = u.shape
    tc = _tile(F, (128,))
    rc = _tile(S, (512,))

    def body(u_ref, w_ref, b_ref, da_ref, du_ref, dw_ref, db_ref, scr_g, scr_d):
        zeros = jnp.zeros((GPAD, tc), F32)
        for scr in (scr_g, scr_d):
            scr[pl.ds(0, GPAD), :] = zeros
            scr[pl.ds(GPAD + S, GPAD), :] = zeros
        for r0 in range(0, S, rc):
            scr_g[pl.ds(GPAD + r0, rc), :] = u_ref[0, pl.ds(r0, rc), :]
        w0, w1, w2, b = w_ref[0:1, :], w_ref[1:2, :], w_ref[2:3, :], b_ref[...]
        sums = [jnp.zeros((1, tc), F32) for _ in range(4)]
        for r0 in range(0, S, rc):
            gp = scr_g[pl.ds(GPAD + r0 - 1, rc), :]
            g0 = scr_g[pl.ds(GPAD + r0, rc), :]
            gn = scr_g[pl.ds(GPAD + r0 + 1, rc), :]
            gc = gp * w0 + g0 * w1 + gn * w2 + b
            sig = _sigmoid(gc)
            da = da_ref[pl.ds(r0, rc), :]
            du_ref[1, pl.ds(r0, rc), :] = (da * (gc * sig)).astype(BF16)
            dgc = da * u_ref[1, pl.ds(r0, rc), :] * (sig * (1.0 + gc * (1.0 - sig)))
            scr_d[pl.ds(GPAD + r0, rc), :] = dgc
            for n, t in enumerate((dgc * gp, dgc * g0, dgc * gn, dgc)):
                sums[n] = sums[n] + jnp.sum(t, axis=0, keepdims=True)
        dw_ref[...] = jnp.concatenate(sums[:3], axis=0)
        db_ref[...] = sums[3]
        for r0 in range(0, S, rc):
            dg = (scr_d[pl.ds(GPAD + r0 + 1, rc), :] * w0 + scr_d[pl.ds(GPAD + r0, rc), :] * w1
                  + scr_d[pl.ds(GPAD + r0 - 1, rc), :] * w2)
            du_ref[0, pl.ds(r0, rc), :] = dg.astype(BF16)

    col = pl.BlockSpec((S, tc), lambda j: (0, j))
    return pl.pallas_call(
        body, grid=(F // tc,),
        in_specs=[pl.BlockSpec((2, S, tc), lambda j: (0, 0, j)), pl.BlockSpec((3, tc), lambda j: (0, j)),
                  pl.BlockSpec((1, tc), lambda j: (0, j)), col],
        out_specs=[pl.BlockSpec((2, S, tc), lambda j: (0, 0, j)), pl.BlockSpec((3, tc), lambda j: (0, j)),
                   pl.BlockSpec((1, tc), lambda j: (0, j))],
        out_shape=[jax.ShapeDtypeStruct((2, S, F), BF16), jax.ShapeDtypeStruct((3, F), F32),
                   jax.ShapeDtypeStruct((1, F), F32)],
        scratch_shapes=[pltpu.VMEM((S + 2 * GPAD, tc), F32), pltpu.VMEM((S + 2 * GPAD, tc), F32)],
        compiler_params=_cparams(("parallel",)), name=name)(u, cw, cb, dact)


def _dot_nt(a, b):
    return lax.dot_general(a, b, (((1,), (1,)), ((), ())), preferred_element_type=F32)


def _dot_tn(a, b):
    return lax.dot_general(a, b, (((0,), (0,)), ((), ())), preferred_element_type=F32)


def _swa_specs(S, nq, G, hd, bq):
    prev = lambda j: jnp.maximum(j - 1, 0)
    nxt = lambda j: jnp.minimum(j + 1, nq - 1)
    kv = [pl.BlockSpec((None, bq, hd), functools.partial(lambda kh, j, f: (kh, f(j), 0), f=f))
          for f in (prev, lambda j: j, nxt)]
    pk = [pl.BlockSpec((1, bq), functools.partial(lambda kh, j, f: (0, f(j)), f=f)) for f in (prev, lambda j: j, nxt)]
    smem = pl.BlockSpec(memory_space=pltpu.SMEM)
    return ([pl.BlockSpec((bq, G * hd), lambda kh, j: (j, kh))] + kv + kv
            + [pl.BlockSpec((bq, 1), lambda kh, j: (j, 0))] + pk + [smem, smem])


def _swa_scores(j, kh, g, S, bq, G, hd, q_ref, kspan, dist, slopes, sinks):
    qi = j * bq + lax.broadcasted_iota(jnp.int32, (bq, 1), 0)
    ki = (j - 1) * bq + lax.broadcasted_iota(jnp.int32, (1, 3 * bq), 1)
    ok = (jnp.abs(qi - ki) <= SWA_WINDOW) & (ki >= 0) & (ki < S)
    qg = q_ref[:, g * hd:(g + 1) * hd]
    s = _dot_nt(qg, kspan) - slopes[kh * G + g] * dist
    return qg, jnp.where(ok, s, NEG), sinks[kh * G + g]


def _swa_fwd_attn(qn, kT, vT, posq, posk, slopes, sinks, name):
    Hkv, S, hd = kT.shape
    G = qn.shape[1] // (Hkv * hd)
    bq = SWA_WINDOW
    nq = S // bq

    def body(q_ref, k0, k1, k2, v0, v1, v2, pq, p0, p1, p2, slopes_ref, sinks_ref, o_ref, l_ref):
        kh, j = pl.program_id(0), pl.program_id(1)
        kspan = jnp.concatenate([k0[...], k1[...], k2[...]], axis=0)
        vspan = jnp.concatenate([v0[...], v1[...], v2[...]], axis=0)
        dist = jnp.abs(pq[...] - jnp.concatenate([p0[...], p1[...], p2[...]], axis=1))
        outs, lses = [], []
        for g in range(G):
            _, s, sink = _swa_scores(j, kh, g, S, bq, G, hd, q_ref, kspan, dist, slopes_ref, sinks_ref)
            m = jnp.maximum(jnp.max(s, axis=-1, keepdims=True), sink)
            p = jnp.exp(s - m)
            den = jnp.sum(p, axis=-1, keepdims=True) + jnp.exp(sink - m)
            outs.append(jnp.dot((p / den).astype(BF16), vspan, preferred_element_type=F32))
            lses.append(m + jnp.log(den))
        o_ref[...] = jnp.concatenate(outs, axis=1)
        l_ref[...] = jnp.concatenate(lses, axis=1)

    return pl.pallas_call(
        body, grid=(Hkv, nq), in_specs=_swa_specs(S, nq, G, hd, bq),
        out_specs=[pl.BlockSpec((bq, G * hd), lambda kh, j: (j, kh)), pl.BlockSpec((None, bq, G), lambda kh, j: (kh, j, 0))],
        out_shape=[jax.ShapeDtypeStruct(qn.shape, F32), jax.ShapeDtypeStruct((Hkv, S, G), F32)],
        compiler_params=_cparams(("parallel", "parallel")), name=name)(
            qn, kT, kT, kT, vT, vT, vT, posq, posk, posk, posk, slopes, sinks)


def _swa_bwd_attn(qn, kT, vT, posq, posk, slopes, sinks, o, lse, do, name):
    Hkv, S, hd = kT.shape
    G = qn.shape[1] // (Hkv * hd)
    bq = SWA_WINDOW
    nq = S // bq

    def body(q_ref, k0, k1, k2, v0, v1, v2, pq, p0, p1, p2, slopes_ref, sinks_ref, o_ref, l_ref, do_ref,
             dq_ref, dk_ref, dv_ref, ds_ref):
        kh, j = pl.program_id(0), pl.program_id(1)
        kspan = jnp.concatenate([k0[...], k1[...], k2[...]], axis=0)
        vspan = jnp.concatenate([v0[...], v1[...], v2[...]], axis=0)
        dist = jnp.abs(pq[...] - jnp.concatenate([p0[...], p1[...], p2[...]], axis=1))
        dk = jnp.zeros((3 * bq, hd), F32)
        dv = jnp.zeros((3 * bq, hd), F32)
        dqs, dsinks = [], []
        for g in range(G):
            qg, s, sink = _swa_scores(j, kh, g, S, bq, G, hd, q_ref, kspan, dist, slopes_ref, sinks_ref)
            lg = l_ref[:, g:g + 1]
            p = jnp.exp(s - lg)
            dog = do_ref[:, g * hd:(g + 1) * hd]
            delta = jnp.sum(dog * o_ref[:, g * hd:(g + 1) * hd], axis=-1, keepdims=True)
            dog = dog.astype(BF16)
            dsc = (p * (_dot_nt(dog, vspan) - delta)).astype(BF16)
            dqs.append(jnp.dot(dsc, kspan, preferred_element_type=F32))
            dk = dk + _dot_tn(dsc, qg)
            dv = dv + _dot_tn(p.astype(BF16), dog)
            dsinks.append(jnp.sum(-jnp.exp(sink - lg) * delta, axis=0, keepdims=True))
        dq_ref[...] = jnp.concatenate(dqs, axis=1)
        dk_ref[...] = dk
        dv_ref[...] = dv

        @pl.when(j == 0)
        def _():
            ds_ref[...] = jnp.zeros_like(ds_ref)

        ds_ref[...] += jnp.concatenate(dsinks, axis=1)

    qblk = pl.BlockSpec((bq, G * hd), lambda kh, j: (j, kh))
    span = pl.BlockSpec((None, None, 3 * bq, hd), lambda kh, j: (kh, j, 0, 0))
    return pl.pallas_call(
        body, grid=(Hkv, nq),
        in_specs=_swa_specs(S, nq, G, hd, bq) + [qblk, pl.BlockSpec((None, bq, G), lambda kh, j: (kh, j, 0)), qblk],
        out_specs=[qblk, span, span, pl.BlockSpec((None, 1, G), lambda kh, j: (kh, 0, 0))],
        out_shape=[jax.ShapeDtypeStruct(qn.shape, F32), jax.ShapeDtypeStruct((Hkv, nq, 3 * bq, hd), F32),
                   jax.ShapeDtypeStruct((Hkv, nq, 3 * bq, hd), F32), jax.ShapeDtypeStruct((Hkv, 1, G), F32)],
        compiler_params=_cparams(("parallel", "arbitrary")), name=name)(
            qn, kT, kT, kT, vT, vT, vT, posq, posk, posk, posk, slopes, sinks, o, lse, do)


def _overlap3(spans, name):
    Hkv, nq, bq3, hd = spans.shape
    bq = bq3 // 3
    sp = spans.reshape(Hkv, nq, 3, bq, hd)

    def body(a_ref, b_ref, c_ref, o_ref):
        b = pl.program_id(1)
        acc = b_ref[...]
        acc = acc + jnp.where(b > 0, a_ref[...], 0.0)
        acc = acc + jnp.where(b < nq - 1, c_ref[...], 0.0)
        o_ref[...] = acc

    def part(f, slot):
        return pl.BlockSpec((None, None, None, bq, hd), lambda kh, b: (kh, f(b), slot, 0, 0))

    return pl.pallas_call(
        body, grid=(Hkv, nq),
        in_specs=[part(lambda b: jnp.maximum(b - 1, 0), 2), part(lambda b: b, 1), part(lambda b: jnp.minimum(b + 1, nq - 1), 0)],
        out_specs=pl.BlockSpec((None, bq, hd), lambda kh, b: (kh, b, 0)),
        out_shape=jax.ShapeDtypeStruct((Hkv, nq * bq, hd), F32),
        compiler_params=_cparams(("parallel", "parallel")), name=name)(sp, sp, sp)


def _mla_fwd_attn(q, k, vT, name):
    H, S, dk = q.shape
    dv = vT.shape[1]
    tq = _tile(S, (1024, 512, 256, 128))
    tk = _tile(S, (512, 256, 128))
    cq = _tile(tq, (256, 128))
    nk = S // tk

    def body(q_ref, k_ref, v_ref, o_ref, l_ref, m_s, l_s, acc):
        kk = pl.program_id(2)

        @pl.when(kk == 0)
        def _():
            m_s[...] = jnp.full_like(m_s, NEG)
            l_s[...] = jnp.zeros_like(l_s)
            acc[...] = jnp.zeros_like(acc)

        kb, vb = k_ref[...], v_ref[...]
        for c0 in range(0, tq, cq):
            cols = pl.ds(c0, cq)
            sT = _dot_nt(kb, q_ref[cols, :])
            m_old = m_s[:, cols]
            m_new = jnp.maximum(m_old, jnp.max(sT, axis=0, keepdims=True))
            a = jnp.exp(m_old - m_new)
            p = jnp.exp(sT - m_new)
            l_s[:, cols] = a * l_s[:, cols] + jnp.sum(p, axis=0, keepdims=True)
            acc[:, cols] = a * acc[:, cols] + jnp.dot(vb, p.astype(BF16), preferred_element_type=F32)
            m_s[:, cols] = m_new

        @pl.when(kk == nk - 1)
        def _():
            o_ref[...] = acc[...] / l_s[...]
            l_ref[...] = m_s[...] + jnp.log(l_s[...])

    return pl.pallas_call(
        body, grid=(H, S // tq, nk),
        in_specs=[pl.BlockSpec((None, tq, dk), lambda h, i, kk: (h, i, 0)),
                  pl.BlockSpec((None, tk, dk), lambda h, i, kk: (h, kk, 0)),
                  pl.BlockSpec((None, dv, tk), lambda h, i, kk: (h, 0, kk))],
        out_specs=[pl.BlockSpec((None, dv, tq), lambda h, i, kk: (h, 0, i)),
                   pl.BlockSpec((None, 1, tq), lambda h, i, kk: (h, 0, i))],
        out_shape=[jax.ShapeDtypeStruct((H, dv, S), F32), jax.ShapeDtypeStruct((H, 1, S), F32)],
        scratch_shapes=[pltpu.VMEM((1, tq), F32), pltpu.VMEM((1, tq), F32), pltpu.VMEM((dv, tq), F32)],
        compiler_params=_cparams(("parallel", "parallel", "arbitrary")), name=name)(q, k, vT)


def _mla_bwd_attn(q, k, kT, v, oT, doT, lse, name):
    H, S, dk = q.shape
    dv = v.shape[2]
    tq = _tile(S, (512, 256, 128))
    tk = _tile(S, (512, 256, 128))
    nq = S // tq

    def body(q_ref, k_ref, kT_ref, v_ref, o_ref, do_ref, l_ref, dq_ref, dk_ref, dv_ref, dk_acc, dv_acc):
        j, i = pl.program_id(1), pl.program_id(2)

        @pl.when(i == 0)
        def _():
            dk_acc[...] = jnp.zeros_like(dk_acc)
            dv_acc[...] = jnp.zeros_like(dv_acc)

        @pl.when((i == 0) & (j == 0))
        def _():
            dq_ref[...] = jnp.zeros_like(dq_ref)

        qb, dob = q_ref[...], do_ref[...]
        delta = jnp.sum(dob * o_ref[...], axis=0, keepdims=True)
        dob = dob.astype(BF16)
        pT = jnp.exp(_dot_nt(k_ref[...], qb) - l_ref[...])
        dpT = jnp.dot(v_ref[...], dob, preferred_element_type=F32)
        dsT = (pT * (dpT - delta)).astype(BF16)
        dk_acc[...] += jnp.dot(dsT, qb, preferred_element_type=F32)
        dv_acc[...] += _dot_nt(pT.astype(BF16), dob)
        cols = pl.ds(pl.multiple_of(i * tq, tq), tq)
        dq_ref[:, cols] += jnp.dot(kT_ref[...], dsT, preferred_element_type=F32)

        @pl.when(i == nq - 1)
        def _():
            dk_ref[...] = dk_acc[...]
            dv_ref[...] = dv_acc[...]

    qrow = lambda d: pl.BlockSpec((None, tq, d), lambda h, j, i: (h, i, 0))
    krow = lambda d: pl.BlockSpec((None, tk, d), lambda h, j, i: (h, j, 0))
    qcol = lambda d: pl.BlockSpec((None, d, tq), lambda h, j, i: (h, 0, i))
    return pl.pallas_call(
        body, grid=(H, S // tk, nq),
        in_specs=[qrow(dk), krow(dk), pl.BlockSpec((None, dk, tk), lambda h, j, i: (h, 0, j)), krow(dv), qcol(dv), qcol(dv),
                  qcol(1)],
        out_specs=[pl.BlockSpec((None, dk, S), lambda h, j, i: (h, 0, 0)), krow(dk), krow(dv)],
        out_shape=[jax.ShapeDtypeStruct((H, dk, S), F32), jax.ShapeDtypeStruct((H, S, dk), F32),
                   jax.ShapeDtypeStruct((H, S, dv), F32)],
        scratch_shapes=[pltpu.VMEM((tk, dk), F32), pltpu.VMEM((tk, dv), F32)],
        compiler_params=_cparams(("parallel", "arbitrary", "arbitrary")), name=name)(q, k, kT, v, oT, doT, lse)


def _sum_heads(x, name):
    H, S, d = x.shape
    ts = _tile(S, (1024, 512, 256, 128))

    def body(x_ref, o_ref):
        acc = x_ref[0]
        for h in range(1, H):
            acc = acc + x_ref[h]
        o_ref[...] = acc

    return pl.pallas_call(body, grid=(S // ts,), in_specs=[pl.BlockSpec((H, ts, d), lambda i: (0, i, 0))],
                          out_specs=pl.BlockSpec((ts, d), lambda i: (i, 0)), out_shape=jax.ShapeDtypeStruct((S, d), F32),
                          compiler_params=_cparams(("parallel",)), name=name)(x)


def _loss_head(y, target, name):
    S, D = y.shape
    tr = _row_tile(S, D)
    row = pl.BlockSpec((tr, D), lambda i: (i, 0))
    vec = pl.BlockSpec((1, D), lambda i: (0, 0))

    def body(y_ref, t_ref, sq_ref, dy_ref):
        @pl.when(pl.program_id(0) == 0)
        def _():
            sq_ref[...] = jnp.zeros_like(sq_ref)

        e = y_ref[...] - t_ref[...]
        sq_ref[...] += jnp.sum(e * e, axis=0, keepdims=True)
        dy_ref[...] = e / D

    return pl.pallas_call(body, grid=(S // tr,), in_specs=[row, row], out_specs=[vec, row],
                          out_shape=[jax.ShapeDtypeStruct((1, D), F32), jax.ShapeDtypeStruct((S, D), F32)],
                          compiler_params=_cparams(("arbitrary",)), name=name)(y, target)


def _sum_chips(recv, name):
    _, R, C = recv.shape
    tr = _row_tile(R, C)

    def body(r_ref, o_ref):
        acc = r_ref[0].astype(F32)
        for i in range(1, N_CHIPS):
            acc = acc + r_ref[i].astype(F32)
        o_ref[...] = acc

    return pl.pallas_call(body, grid=(R // tr,), in_specs=[pl.BlockSpec((N_CHIPS, tr, C), lambda i: (0, i, 0))],
                          out_specs=pl.BlockSpec((tr, C), lambda i: (i, 0)), out_shape=jax.ShapeDtypeStruct((R, C), F32),
                          compiler_params=_cparams(("parallel",)), name=name)(recv)


def _adamw(w, m, v, gs, name):
    R, C = w.shape
    tr = _row_tile(R, 2 * C)
    row = pl.BlockSpec((tr, C), lambda i: (i, 0))
    n = len(gs)

    def body(*refs):
        w_ref, m_ref, v_ref = refs[:3]
        g_ref, d_ref, nm_ref, nv_ref = refs[3 + n:]
        g = refs[3][...]
        for r in refs[4:3 + n]:
            g = g + r[...]
        mm = ADAM_B1 * m_ref[...] + (1.0 - ADAM_B1) * g
        vv = ADAM_B2 * v_ref[...] + (1.0 - ADAM_B2) * (g * g)
        m_hat = mm / (1.0 - ADAM_B1 ** ADAM_STEP)
        v_hat = vv / (1.0 - ADAM_B2 ** ADAM_STEP)
        g_ref[...] = g
        d_ref[...] = -ADAM_LR * (m_hat / (jnp.sqrt(v_hat) + ADAM_EPS) + ADAM_WD * w_ref[...])
        nm_ref[...] = mm
        nv_ref[...] = vv

    sds = jax.ShapeDtypeStruct((R, C), F32)
    return pl.pallas_call(body, grid=(R // tr,), in_specs=[row] * (3 + n), out_specs=[row] * 4, out_shape=[sds] * 4,
                          compiler_params=_cparams(("parallel",)), name=name)(w, m, v, *gs)


def _chip_peers():
    x, y, c = lax.axis_index("x"), lax.axis_index("y"), lax.axis_index("c")
    others = [(1 - x, y), (x, 1 - y), (1 - x, 1 - y)]
    return x, y, c, 2 * x + y, [(px, py, 2 * px + py) for px, py in others]


HALVE_MIN_BYTES = 1 << 20


def _gather_comm(srcs):
    n = len(srcs)
    shapes = [tuple(a.shape[1:] if l is not None else a.shape) for a, l in srcs]
    halved = [s[0] % 2 == 0 and int(np.prod(s)) * a.dtype.itemsize >= HALVE_MIN_BYTES for s, (a, _) in zip(shapes, srcs)]
    outs = [jax.ShapeDtypeStruct((N_CHIPS,) + s, a.dtype) for s, (a, _) in zip(shapes, srcs)]
    dma = pltpu.SemaphoreType.DMA
    sems = [dma((3 * n,)), dma((3 * n,)), dma((3 * n,)), dma((3 * n,)), dma((n,))]

    def rows(ref, t, half):
        if not halved[t]:
            return ref
        h = shapes[t][0] // 2
        return ref.at[pl.ds(half * h, h)]

    def copies(in_refs, out_refs, sem_refs):
        send, recv, fsend, frecv, lsem = sem_refs
        x, y, c, me, peers = _chip_peers()
        src = [in_refs[t] if l is None else in_refs[t].at[l] for t, (_, l) in enumerate(srcs)]
        local = [pltpu.make_async_copy(src[t], out_refs[t].at[me], lsem.at[t]) for t in range(n)]

        def ici(t, j, origin):
            px, py, _ = peers[j]
            return pltpu.make_async_remote_copy(src_ref=rows(src[t], t, c), dst_ref=rows(out_refs[t].at[origin], t, c),
                                                send_sem=send.at[3 * t + j], recv_sem=recv.at[3 * t + j],
                                                device_id=(px, py, c), device_id_type=MESH)

        def hand(t, j, half):
            blk = rows(out_refs[t].at[peers[j][2]], t, half)
            return pltpu.make_async_remote_copy(src_ref=blk, dst_ref=blk, send_sem=fsend.at[3 * t + j],
                                                recv_sem=frecv.at[3 * t + j], device_id=(x, y, 1 - c), device_id_type=MESH)

        return c, me, peers, local, ici, hand

    def start(in_refs, out_refs, sem_refs):
        c, me, peers, local, ici, hand = copies(in_refs, out_refs, sem_refs)
        for t in range(n):
            local[t].start()
            for j in range(3):
                ici(t, j, me).start()

    def finish(in_refs, out_refs, sem_refs):
        c, me, peers, local, ici, hand = copies(in_refs, out_refs, sem_refs)
        for t in range(n):
            for j in range(3):
                ici(t, j, peers[j][2]).wait_recv()
                if halved[t]:
                    hand(t, j, c).start()
        for t in range(n):
            for j in range(3):
                if halved[t]:
                    hand(t, j, 1 - c).wait_recv()
        for t in range(n):
            for j in range(3):
                ici(t, j, me).wait_send()
                if halved[t]:
                    hand(t, j, c).wait_send()
            local[t].wait()

    return _Comm([a for a, _ in srcs], outs, {}, sems, start, finish)


def _scatter_comm(items, bufs):
    n = len(items)
    dma = pltpu.SemaphoreType.DMA
    sems = [dma((3 * n,)), dma((3 * n,)), dma((n,))]

    def piece(ref, rows):
        return ref if rows is None else ref.at[pl.ds(rows[0], rows[1])]

    def copies(in_refs, out_refs, sem_refs):
        send, recv, lsem = sem_refs
        x, y, c, me, peers = _chip_peers()

        def local(t):
            _, bi, l, rows = items[t]
            return pltpu.make_async_copy(piece(in_refs[t].at[me], rows), piece(out_refs[bi].at[me, l], rows), lsem.at[t])

        def ici(t, j, origin):
            _, bi, l, rows = items[t]
            px, py, pi = peers[j]
            return pltpu.make_async_remote_copy(src_ref=piece(in_refs[t].at[pi], rows),
                                                dst_ref=piece(out_refs[bi].at[origin, l], rows),
                                                send_sem=send.at[3 * t + j], recv_sem=recv.at[3 * t + j],
                                                device_id=(px, py, c), device_id_type=MESH)

        return me, peers, local, ici

    def start(in_refs, out_refs, sem_refs):
        me, peers, local, ici = copies(in_refs, out_refs, sem_refs)
        for t in range(n):
            local(t).start()
            for j in range(3):
                ici(t, j, me).start()

    def finish(in_refs, out_refs, sem_refs):
        me, peers, local, ici = copies(in_refs, out_refs, sem_refs)
        for t in range(n):
            for j in range(3):
                ici(t, j, peers[j][2]).wait_recv()
        for t in range(n):
            for j in range(3):
                ici(t, j, me).wait_send()
            local(t).wait()

    return _Comm([it[0] for it in items] + list(bufs), [jax.ShapeDtypeStruct(b.shape, b.dtype) for b in bufs],
                 {n + k: k for k in range(len(bufs))}, sems, start, finish)


def _run_comm(comm, name):
    ni, no = len(comm.ins), len(comm.outs)

    def body(*refs):
        comm.start(refs[:ni], refs[ni:ni + no], refs[ni + no:])
        comm.finish(refs[:ni], refs[ni:ni + no], refs[ni + no:])

    hbm = pl.BlockSpec(memory_space=pl.ANY)
    return pl.pallas_call(body, in_specs=[hbm] * ni, out_specs=[hbm] * no, out_shape=list(comm.outs),
                          scratch_shapes=list(comm.sems), input_output_aliases=dict(comm.aliases), name=name)(*comm.ins)


def _swap_cores(arrs, name):
    n = len(arrs)

    def body(*refs):
        src_refs, out_refs = refs[:n], refs[n:2 * n]
        send_sems, recv_sems = refs[2 * n:]
        x, y, c = lax.axis_index("x"), lax.axis_index("y"), lax.axis_index("c")
        cps = []
        for t in range(n):
            cp = pltpu.make_async_remote_copy(src_ref=src_refs[t], dst_ref=out_refs[t], send_sem=send_sems.at[t],
                                              recv_sem=recv_sems.at[t], device_id=(x, y, 1 - c), device_id_type=MESH)
            cp.start()
            cps.append(cp)
        for cp in cps:
            cp.wait()

    hbm = pl.BlockSpec(memory_space=pl.ANY)
    return pl.pallas_call(
        body, in_specs=[hbm] * n, out_specs=[hbm] * n, out_shape=[jax.ShapeDtypeStruct(a.shape, a.dtype) for a in arrs],
        scratch_shapes=[pltpu.SemaphoreType.DMA((n,)), pltpu.SemaphoreType.DMA((n,))],
        name=name)(*arrs)


def _allreduce_small(buf, name):
    R, C = buf.shape

    def body(b_ref, o_ref, recv, send_sems, recv_sems):
        x, y, c = lax.axis_index("x"), lax.axis_index("y"), lax.axis_index("c")
        me = 4 * x + 2 * y + c
        recv[me] = b_ref[...]
        cps = []
        for mask in range(1, N_DEV):
            px, py, pc = x ^ (mask >> 2 & 1), y ^ (mask >> 1 & 1), c ^ (mask & 1)
            cp = pltpu.make_async_remote_copy(src_ref=b_ref, dst_ref=recv.at[me], send_sem=send_sems.at[mask - 1],
                                              recv_sem=recv_sems.at[mask - 1], device_id=(px, py, pc), device_id_type=MESH)
            cp.start()
            cps.append((cp, 4 * px + 2 * py + pc, (px, py, pc), mask))
        for cp, pi, dev, mask in cps:
            pltpu.make_async_remote_copy(src_ref=b_ref, dst_ref=recv.at[pi], send_sem=send_sems.at[mask - 1],
                                         recv_sem=recv_sems.at[mask - 1], device_id=dev, device_id_type=MESH).wait_recv()
        for cp, _, _, _ in cps:
            cp.wait_send()
        acc = recv[0]
        for i in range(1, N_DEV):
            acc = acc + recv[i]
        o_ref[...] = acc

    vm = pl.BlockSpec(memory_space=pltpu.VMEM)
    return pl.pallas_call(
        body, in_specs=[vm], out_specs=vm, out_shape=jax.ShapeDtypeStruct((R, C), F32),
        scratch_shapes=[pltpu.VMEM((N_DEV, R, C), F32), pltpu.SemaphoreType.DMA((N_DEV - 1,)),
                        pltpu.SemaphoreType.DMA((N_DEV - 1,))],
        compiler_params=pltpu.CompilerParams(vmem_limit_bytes=VMEM_LIMIT), name=name)(buf)


def _pack(arrs):
    parts, offs, n = [], [], 0
    for a in arrs:
        f = a.reshape(-1).astype(F32)
        k = -(-f.shape[0] // LANES) * LANES
        parts.append(jnp.pad(f, (0, k - f.shape[0])))
        offs.append(n)
        n += k
    total = -(-n // (8 * LANES)) * (8 * LANES)
    if total > n:
        parts.append(jnp.zeros((total - n,), F32))
    return jnp.concatenate(parts).reshape(-1, LANES), offs


def _unpack(buf, offs, shapes):
    flat = buf.reshape(-1)
    return [flat[o:o + int(np.prod(s))].reshape(s) for o, s in zip(offs, shapes)]


def _ride(mm, rider):
    if rider is None:
        return mm(None)
    build, done = rider
    out, res = mm(build())
    done(res)
    return out


def _ffn_fwd(xm, g, w_in, cw, cb, w_out, t, riders=(None, None)):
    h2 = _rms_fwd(xm, g, BF16, f"ffn_norm{t}")
    u = _ride(lambda c: _mm_nn(h2, w_in, F32, f"ffn_in{t}", out_panels=2, comm=c), riders[0])
    act = _glu_fwd(u, cw, cb, f"glu{t}")
    xo = _ride(lambda c: _mm_nn(act, w_out, F32, f"ffn_out{t}", res=xm, comm=c), riders[1])
    return xo, (h2, u, act)


def _ffn_bwd(dxo, xm, g, w_in, cw, cb, w_out, saved, t, riders=(None, None, None, None)):
    h2, u, act = saved
    dact = _ride(lambda c: _mm_nt(dxo, w_out, F32, f"ffn_dact{t}", comm=c), riders[0])
    dw_out = _ride(lambda c: _mm_tn(act, dxo, BF16, f"ffn_dwout{t}", comm=c), riders[1])
    du, dcw, dcb = _glu_bwd(u, cw, cb, dact, f"glu_bwd{t}")
    dh2 = _ride(lambda c: _mm_nt(du, w_in, F32, f"ffn_dh{t}", comm=c), riders[2])
    dw_in = _ride(lambda c: _mm_tn(h2, du, BF16, f"ffn_dwin{t}", out_panels=N_CHIPS, comm=c), riders[3])
    dxm, dg = _rms_bwd(xm, g, dh2, f"ffn_norm_bwd{t}", res=dxo)
    return dxm, dict(ffn_w_in=dw_in, ffn_conv_w=dcw, ffn_conv_b=dcb, ffn_w_out=dw_out, norm_ffn_g=dg)


def _pool_fwd(x, g, wg, scale, t):
    G = len(POOL_WINDOWS)
    h = _rms_fwd(x, g, F32, f"pool_norm{t}")
    pooled = _pool_win(h, True, BF16, f"pool_win{t}")
    yraw = jnp.concatenate([_mm_nn(pooled[i], wg[i], F32, f"pool_mm{t}_{i}") for i in range(G)], axis=1)
    xm = _scale_res(x, yraw, scale, f"pool_out{t}")
    return xm, (pooled, yraw)


def _pool_bwd(dxm, x, g, wg, scale, saved, t):
    G = len(POOL_WINDOWS)
    pooled, yraw = saved
    dg_ = x.shape[1] // G
    dyraw, dscale = _scale_bwd(dxm, yraw, scale, f"pool_out_bwd{t}")
    parts = [dyraw[:, i * dg_:(i + 1) * dg_] for i in range(G)]
    dpool = jnp.concatenate([_mm_nt(parts[i], wg[i], F32, f"pool_dp{t}_{i}") for i in range(G)], axis=1)
    dw = jnp.stack([_mm_tn(pooled[i], parts[i], BF16, f"pool_dw{t}_{i}") for i in range(G)])
    dh = jnp.concatenate(_pool_win(dpool, False, F32, f"pool_win_bwd{t}"), axis=1)
    dx, dgn = _rms_bwd(x, g, dh, f"pool_norm_bwd{t}", res=dxm)
    return dx, dict(pool_w=dw, pool_scale=dscale, norm_mix_g=dgn)


def _swa_fwd(x, g, w_qkv, q_gain, k_gain, sinks, w_o, pos, t):
    S = x.shape[0]
    Hq, Hkv, hd = SWA_HEADS, SWA_KV_HEADS, SWA_HEAD_DIM
    nq, nkv = Hq * hd, Hkv * hd
    posq, posk, slopes = pos
    h = _rms_fwd(x, g, BF16, f"swa_norm{t}")
    qkv = _mm_nn(h, w_qkv, F32, f"swa_qkv{t}")
    q = qkv[:, :nq].reshape(S * Hq, hd)
    k = qkv[:, nq:nq + nkv].reshape(S * Hkv, hd)
    qn = _rms_fwd(q, q_gain, BF16, f"swa_qnorm{t}", scale=hd ** -0.5).reshape(S, nq)
    kT = _rms_fwd(k, k_gain, BF16, f"swa_knorm{t}").reshape(S, Hkv, hd).transpose(1, 0, 2)
    vT = qkv[:, nq + nkv:].astype(BF16).reshape(S, Hkv, hd).transpose(1, 0, 2)
    o, lse = _swa_fwd_attn(qn, kT, vT, posq, posk, slopes, sinks.reshape(-1), f"swa_attn{t}")
    xm = _mm_nn(o, w_o, F32, f"swa_o{t}", res=x)
    return xm, (h, q, k, qn, kT, vT, o, lse)


def _swa_bwd(dxm, x, g, w_qkv, q_gain, k_gain, sinks, w_o, pos, saved, t):
    S = x.shape[0]
    Hq, Hkv, hd = SWA_HEADS, SWA_KV_HEADS, SWA_HEAD_DIM
    nq, nkv = Hq * hd, Hkv * hd
    posq, posk, slopes = pos
    h, q, k, qn, kT, vT, o, lse = saved
    do = _mm_nt(dxm, w_o, F32, f"swa_do{t}")
    dw_o = _mm_tn(o, dxm, BF16, f"swa_dwo{t}")
    dqn, dkp, dvp, dsink = _swa_bwd_attn(qn, kT, vT, posq, posk, slopes, sinks.reshape(-1), o, lse, do, f"swa_attn_bwd{t}")
    dkn = _overlap3(dkp, f"swa_dk{t}").transpose(1, 0, 2).reshape(S * Hkv, hd)
    dv = _overlap3(dvp, f"swa_dv{t}").transpose(1, 0, 2).reshape(S, nkv)
    dq, dqg = _rms_bwd(q, q_gain, dqn.reshape(S * Hq, hd), f"swa_qnorm_bwd{t}", scale=hd ** -0.5)
    dk, dkg = _rms_bwd(k, k_gain, dkn, f"swa_knorm_bwd{t}")
    dqkv = jnp.concatenate([dq.reshape(S, nq), dk.reshape(S, nkv), dv], axis=1)
    dh = _mm_nt(dqkv, w_qkv, F32, f"swa_dh{t}")
    dw_qkv = _mm_tn(h, dqkv, BF16, f"swa_dwqkv{t}", out_panels=N_CHIPS)
    dx, dgn = _rms_bwd(x, g, dh, f"swa_norm_bwd{t}", res=dxm)
    return dx, dict(swa_w_qkv=dw_qkv, swa_q_gain=dqg, swa_k_gain=dkg, swa_sinks=dsink.reshape(1, Hq), swa_w_o=dw_o,
                    norm_mix_g=dgn)


def _mla_fwd(x, g, w_down, q_a_gain, kv_a_gain, w_uq, w_ukv, qn_gain, qr_gain, kn_gain, kr_gain, w_o, rope, t):
    S = x.shape[0]
    H, dn, R, dv, qr_, kvr = MLA_HEADS, MLA_NOPE, MLA_ROPE, MLA_V, MLA_Q_RANK, MLA_KV_RANK
    sc = (dn + R) ** -0.5
    h = _rms_fwd(x, g, BF16, f"mla_norm{t}")
    d = _mm_nn(h, w_down, F32, f"mla_down{t}")
    cq_pre, ckv_pre, kpe_pre = d[:, :qr_], d[:, qr_:qr_ + kvr], d[:, qr_ + kvr:]
    cq = _rms_fwd(cq_pre, q_a_gain, BF16, f"mla_cq{t}")
    ckv = _rms_fwd(ckv_pre, kv_a_gain, BF16, f"mla_ckv{t}")
    q3 = _mm_nn(cq, w_uq, F32, f"mla_uq{t}").reshape(S, H, dn + R).transpose(1, 0, 2)
    kv3 = _mm_nn(ckv, w_ukv, F32, f"mla_ukv{t}").reshape(S, H, dn + dv).transpose(1, 0, 2)
    qn_in, qp_in = q3[..., :dn].reshape(H * S, dn), q3[..., dn:].reshape(H * S, R)
    kn_in, v = kv3[..., :dn].reshape(H * S, dn), kv3[..., dn:].astype(BF16)
    qn = _rms_fwd(qn_in, qn_gain, BF16, f"mla_qn{t}", scale=sc).reshape(H, S, dn)
    qp = _rms_fwd(qp_in, qr_gain, BF16, f"mla_qp{t}", scale=sc, rope=rope).reshape(H, S, R)
    kn = _rms_fwd(kn_in, kn_gain, BF16, f"mla_kn{t}").reshape(H, S, dn)
    kp = _rms_fwd(kpe_pre, kr_gain, BF16, f"mla_kp{t}", rope=rope)
    qf = jnp.concatenate([qn, qp], axis=-1)
    kf = jnp.concatenate([kn, jnp.broadcast_to(kp[None], (H, S, R))], axis=-1)
    oT, lse = _mla_fwd_attn(qf, kf, v.transpose(0, 2, 1), f"mla_attn{t}")
    oT2 = oT.reshape(H * dv, S)
    xm = _mm_tn(oT2, w_o, F32, f"mla_o{t}", res=x)
    return xm, (h, cq_pre, ckv_pre, kpe_pre, cq, ckv, qn_in, qp_in, kn_in, v, qf, kf, oT, lse)


def _mla_bwd(dxm, x, g, w_down, q_a_gain, kv_a_gain, w_uq, w_ukv, qn_gain, qr_gain, kn_gain, kr_gain, w_o, rope,
             saved, t):
    S = x.shape[0]
    H, dn, R, dv = MLA_HEADS, MLA_NOPE, MLA_ROPE, MLA_V
    sc = (dn + R) ** -0.5
    h, cq_pre, ckv_pre, kpe_pre, cq, ckv, qn_in, qp_in, kn_in, v, qf, kf, oT, lse = saved
    oT2 = oT.reshape(H * dv, S)
    doT = _mm_nt(w_o, dxm, F32, f"mla_do{t}").reshape(H, dv, S)
    dw_o = _mm_nn(oT2, dxm, BF16, f"mla_dwo{t}")
    dqT, dkf, dvv = _mla_bwd_attn(qf, kf, kf.transpose(0, 2, 1), v, oT, doT, lse, f"mla_attn_bwd{t}")
    dqf = dqT.transpose(0, 2, 1)
    dqn, dqp, dkn = dqf[..., :dn], dqf[..., dn:], dkf[..., :dn]
    dkp = _sum_heads(dkf[..., dn:], f"mla_dkp{t}")
    dqn_in, dqng = _rms_bwd(qn_in, qn_gain, dqn.reshape(H * S, dn), f"mla_qn_bwd{t}", scale=sc)
    dqp_in, dqrg = _rms_bwd(qp_in, qr_gain, dqp.reshape(H * S, R), f"mla_qp_bwd{t}", scale=sc, rope=rope)
    dkn_in, dkng = _rms_bwd(kn_in, kn_gain, dkn.reshape(H * S, dn), f"mla_kn_bwd{t}")
    dkpe_pre, dkrg = _rms_bwd(kpe_pre, kr_gain, dkp, f"mla_kp_bwd{t}", rope=rope)
    dq = jnp.concatenate([dqn_in.reshape(H, S, dn), dqp_in.reshape(H, S, R)], axis=-1).transpose(1, 0, 2).reshape(S, -1)
    dkv = jnp.concatenate([dkn_in.reshape(H, S, dn), dvv], axis=-1).transpose(1, 0, 2).reshape(S, -1)
    dcq = _mm_nt(dq, w_uq, F32, f"mla_dcq{t}")
    dw_uq = _mm_tn(cq, dq, BF16, f"mla_dwuq{t}", out_panels=N_CHIPS)
    dckv = _mm_nt(dkv, w_ukv, F32, f"mla_dckv{t}")
    dw_ukv = _mm_tn(ckv, dkv, BF16, f"mla_dwukv{t}", out_panels=N_CHIPS)
    dcq_pre, dqag = _rms_bwd(cq_pre, q_a_gain, dcq, f"mla_cq_bwd{t}")
    dckv_pre, dkvag = _rms_bwd(ckv_pre, kv_a_gain, dckv, f"mla_ckv_bwd{t}")
    dd = jnp.concatenate([dcq_pre, dckv_pre, dkpe_pre], axis=1)
    dh = _mm_nt(dd, w_down, F32, f"mla_dh{t}")
    dw_down = _mm_tn(h, dd, BF16, f"mla_dwdown{t}")
    dx, dgn = _rms_bwd(x, g, dh, f"mla_norm_bwd{t}", res=dxm)
    return dx, dict(mla_w_down=dw_down, mla_q_a_gain=dqag, mla_kv_a_gain=dkvag, mla_w_uq=dw_uq, mla_w_ukv=dw_ukv,
                    mla_qn_gain=dqng, mla_qr_gain=dqrg, mla_kn_gain=dkng, mla_kr_gain=dkrg, mla_w_o=dw_o, norm_mix_g=dgn)


def _chips_to_axis(gathered, axis):
    moved = jnp.moveaxis(gathered, 0, axis)
    shape = list(moved.shape)
    shape[axis:axis + 2] = [shape[axis] * shape[axis + 1]]
    return moved.reshape(shape)


def kernel(x, positions, norm_mix_g, norm_ffn_g, pool_w, pool_scale, swa_w_qkv, swa_q_gain, swa_k_gain, swa_sinks, swa_w_o, mla_w_down, mla_q_a_gain, mla_kv_a_gain, mla_w_uq, mla_w_ukv, mla_qn_gain, mla_qr_gain, mla_kn_gain, mla_kr_gain, mla_w_o, ffn_w_in, ffn_conv_w, ffn_conv_b, ffn_w_out, loss_target, m_norm_mix_g, m_norm_ffn_g, m_pool_w, m_pool_scale, m_swa_w_qkv, m_swa_q_gain, m_swa_k_gain, m_swa_sinks, m_swa_w_o, m_mla_w_down, m_mla_q_a_gain, m_mla_kv_a_gain, m_mla_w_uq, m_mla_w_ukv, m_mla_qn_gain, m_mla_qr_gain, m_mla_kn_gain, m_mla_kr_gain, m_mla_w_o, m_ffn_w_in, m_ffn_conv_w, m_ffn_conv_b, m_ffn_w_out, v_norm_mix_g, v_norm_ffn_g, v_pool_w, v_pool_scale, v_swa_w_qkv, v_swa_q_gain, v_swa_k_gain, v_swa_sinks, v_swa_w_o, v_mla_w_down, v_mla_q_a_gain, v_mla_kv_a_gain, v_mla_w_uq, v_mla_w_ukv, v_mla_qn_gain, v_mla_qr_gain, v_mla_kn_gain, v_mla_kr_gain, v_mla_w_o, v_ffn_w_in, v_ffn_conv_w, v_ffn_conv_b, v_ffn_w_out):
    args = dict(locals())
    W = {n: args[n] for n in WEIGHTS}
    M = {n: args["m_" + n] for n in WEIGHTS}
    V = {n: args["v_" + n] for n in WEIGHTS}
    xs = x[0]
    S, D = xs.shape
    chip = 2 * lax.axis_index("x") + lax.axis_index("y")

    big = ['pool_w', 'swa_w_qkv', 'swa_w_o', 'mla_w_down', 'mla_w_uq', 'mla_w_ukv', 'mla_w_o', 'ffn_w_in', 'ffn_w_out']
    small_sharded = [n for n in WEIGHTS if SMALL.get(n) is not None]
    mixer_w = {0: ['pool_w'], 1: ['swa_w_qkv', 'swa_w_o'], 2: ['mla_w_down', 'mla_w_uq', 'mla_w_ukv', 'mla_w_o']}
    Wb = {n: W[n].astype(BF16) for n in big}
    Wg = {}

    def mixer_keys(i):
        return [(n, i // 3) for n in mixer_w[i % 3]]

    def gather_rider(keys):
        if keys is None:
            return None
        return (lambda: _gather_comm([(Wb[n], l) for n, l in keys])), (lambda res: Wg.update(zip(keys, res)))

    keys0 = mixer_keys(0) + [('ffn_w_in', 0), ('ffn_w_out', 0)]
    first = _run_comm(_gather_comm([(Wb[n], l) for n, l in keys0] + [(W[n], None) for n in small_sharded]), "gather_first")
    Wg.update(zip(keys0, first))
    full = dict(W)
    for n, r in zip(small_sharded, first[len(keys0):]):
        full[n] = _chips_to_axis(r, SMALL[n])
    rows = lambda a: a.reshape((a.shape[0] * a.shape[1],) + a.shape[2:])

    posf = positions.astype(F32)
    slopes = jnp.asarray(2.0 ** (-8.0 * np.arange(1, SWA_HEADS + 1) / SWA_HEADS), dtype=F32)
    pos = (posf.reshape(S, 1), posf.reshape(1, S), slopes)
    inv = ROPE_THETA ** (-jnp.arange(0, MLA_ROPE, 2, dtype=F32) / MLA_ROPE)
    ang = posf[:, None] * inv[None, :]
    cos, sin = jnp.cos(ang), jnp.sin(ang)
    rope = (jnp.concatenate([cos, cos], axis=1), jnp.concatenate([-sin, sin], axis=1))

    margs_cache = {}

    def mixer_args(i):
        if i in margs_cache:
            return margs_cache[i]
        kind, j = i % 3, i // 3
        g = full['norm_mix_g'][i:i + 1]
        if kind == 0:
            byg = jnp.swapaxes(Wg[('pool_w', j)], 0, 1)
            a = (g, [rows(byg[k]) for k in range(len(POOL_WINDOWS))], full['pool_scale'][j:j + 1])
        elif kind == 1:
            a = (g, Wg[('swa_w_qkv', j)], full['swa_q_gain'][j:j + 1], full['swa_k_gain'][j:j + 1],
                 full['swa_sinks'][j:j + 1], rows(Wg[('swa_w_o', j)]), pos)
        else:
            a = (g, rows(Wg[('mla_w_down', j)]), full['mla_q_a_gain'][j:j + 1], full['mla_kv_a_gain'][j:j + 1],
                 Wg[('mla_w_uq', j)], Wg[('mla_w_ukv', j)], full['mla_qn_gain'][j:j + 1], full['mla_qr_gain'][j:j + 1],
                 full['mla_kn_gain'][j:j + 1], full['mla_kr_gain'][j:j + 1], rows(Wg[('mla_w_o', j)]), rope)
        margs_cache[i] = (kind, a)
        return kind, a

    def ffn_args(i):
        return (full['norm_ffn_g'][i:i + 1], Wg[('ffn_w_in', i)], full['ffn_conv_w'][i], full['ffn_conv_b'][i:i + 1],
                rows(Wg[('ffn_w_out', i)]))

    fwd = (_pool_fwd, _swa_fwd, _mla_fwd)
    bwd = (_pool_bwd, _swa_bwd, _mla_bwd)
    tape = []
    cur = xs
    for i in range(DEPTH):
        kind, margs = mixer_args(i)
        xm, msaved = fwd[kind](cur, *margs, i)
        riding = [None, None] if i + 1 == DEPTH else [[('ffn_w_in', i + 1)], [('ffn_w_out', i + 1)] + mixer_keys(i + 1)]
        xo, fsaved = _ffn_fwd(xm, *ffn_args(i), i, [gather_rider(k) for k in riding])
        tape.append((cur, xm, msaved, fsaved))
        cur = xo
    sq, dcur = _loss_head(cur, loss_target[0], "loss_head")
    loss = lax.psum(0.5 / D * jnp.sum(sq), ("x", "y", "c"))

    def by_chip(n, gl):
        if n == 'pool_w':
            G, dg_ = gl.shape[0], gl.shape[1]
            return gl.reshape(G, N_CHIPS, dg_ // N_CHIPS, dg_).swapaxes(0, 1)
        if len(gl.shape) == 3:
            return gl
        return gl.reshape((N_CHIPS, gl.shape[0] // N_CHIPS) + gl.shape[1:])

    bufs = {n: lax.empty((N_CHIPS,) + tuple(W[n].shape), BF16) for n in big}

    def scatter_rider(group):
        if group is None:
            return None
        names = list(dict.fromkeys(n for n, _, _, _ in group))
        build = lambda: _scatter_comm([(gl, names.index(n), l, r) for n, l, gl, r in group], [bufs[n] for n in names])
        return build, (lambda res: bufs.update(zip(names, res)))

    grads = {n: [None] * W[n].shape[0] for n in WEIGHTS}
    pending = [None] * 4
    for i in reversed(range(DEPTH)):
        kind, margs = mixer_args(i)
        x_in, xm, msaved, fsaved = tape[i]
        dxm, fg = _ffn_bwd(dcur, xm, *ffn_args(i), fsaved, i, [scatter_rider(grp) for grp in pending])
        dcur, mg = bwd[kind](dxm, x_in, *margs, msaved, i)
        for n, gval in fg.items():
            grads[n][i] = gval
        for n, gval in mg.items():
            grads[n][i if n == 'norm_mix_g' else i // 3] = gval
        g_in = by_chip('ffn_w_in', fg['ffn_w_in'])
        half = g_in.shape[1] // 2
        pending = [[('ffn_w_in', i, g_in, (0, half))], [('ffn_w_in', i, g_in, (half, half))],
                   [('ffn_w_out', i, by_chip('ffn_w_out', fg['ffn_w_out']), None)],
                   [(n, i // 3, by_chip(n, mg[n]), None) for n in mixer_w[kind]]]
    grad_x = dcur[None]
    build, done = scatter_rider([it for grp in pending for it in grp])
    done(_run_comm(build(), "scatter_last"))

    recvs = [bufs[n] for n in big]
    partial = []
    for n, r in zip(big, recvs):
        C = r.shape[-1]
        partial.append(_sum_chips(r.reshape(N_CHIPS, -1, C), f"sum_{n}"))
    other = _swap_cores(partial, "swap_cores")
    out = {}
    for n, p, q in zip(big, partial, other):
        C = p.shape[1]
        res = _adamw(W[n].reshape(-1, C), M[n].reshape(-1, C), V[n].reshape(-1, C), [p, q], f"adamw_{n}")
        out[n] = [r.reshape(W[n].shape) for r in res]

    small = [n for n in WEIGHTS if n in SMALL]
    full_shapes = [tuple(full[n].shape) for n in small]
    sg = []
    for n in small:
        parts = grads[n]
        if n == 'ffn_conv_w':
            sg.append(jnp.stack(parts))
        else:
            sg.append(jnp.concatenate(parts, axis=0))
    buf, offs = _pack(sg)
    summed = _unpack(_allreduce_small(buf, "allreduce_small"), offs, full_shapes)
    gsm = []
    for n, gfull in zip(small, summed):
        ax = SMALL[n]
        if ax is not None:
            size = W[n].shape[ax]
            gfull = lax.dynamic_slice_in_dim(gfull, chip * size, size, axis=ax)
        gsm.append(gfull)
    gb, goffs = _pack(gsm)
    wb, _ = _pack([W[n] for n in small])
    mb, _ = _pack([M[n] for n in small])
    vb, _ = _pack([V[n] for n in small])
    res = _adamw(wb, mb, vb, [gb], "adamw_small")
    shapes = [tuple(W[n].shape) for n in small]
    unp = [_unpack(r, goffs, shapes) for r in res]
    for k, n in enumerate(small):
        out[n] = [u[k] for u in unp]

    return (loss, grad_x, *[out[n][0] for n in WEIGHTS], *[out[n][1] for n in WEIGHTS],
            *[out[n][2] for n in WEIGHTS], *[out[n][3] for n in WEIGHTS])
```

```python
import functools
import math

import numpy as np
import jax
import jax.numpy as jnp
from jax import lax
from jax.experimental import pallas as pl
from jax.experimental.pallas import tpu as pltpu

F32 = jnp.float32
BF16 = jnp.bfloat16

D_MODEL = 2048
SEQ = 4096
DEPTH = 4
EPS = 1e-6
POOL_WINDOWS = (2, 4, 8, 16)
SWA_HEADS = 32
SWA_KV_HEADS = 4
SWA_HEAD_DIM = 64
SWA_WINDOW = 128
MLA_HEADS = 16
MLA_NOPE = 128
MLA_ROPE = 64
MLA_V = 128
MLA_Q_RANK = 512
MLA_KV_RANK = 512
ROPE_THETA = 10000.0
D_FF = 5632
ADAM_LR = 0.001
ADAM_B1 = 0.9
ADAM_B2 = 0.999
ADAM_EPS = 1e-08
ADAM_WD = 0.01
ADAM_STEP = 10

N_CHIPS = 4
N_DEV = 8
MESH = pl.DeviceIdType.MESH
VMEM_LIMIT = 48 << 20
LANES = 128
NEG = -1e30

WEIGHTS = ['norm_mix_g', 'norm_ffn_g', 'pool_w', 'pool_scale', 'swa_w_qkv', 'swa_q_gain', 'swa_k_gain', 'swa_sinks',
           'swa_w_o', 'mla_w_down', 'mla_q_a_gain', 'mla_kv_a_gain', 'mla_w_uq', 'mla_w_ukv', 'mla_qn_gain',
           'mla_qr_gain', 'mla_kn_gain', 'mla_kr_gain', 'mla_w_o', 'ffn_w_in', 'ffn_conv_w', 'ffn_conv_b', 'ffn_w_out']
SMALL = {'norm_mix_g': None, 'norm_ffn_g': None, 'pool_scale': 1, 'swa_q_gain': None, 'swa_k_gain': None,
         'swa_sinks': None, 'mla_q_a_gain': 1, 'mla_kv_a_gain': 1, 'mla_qn_gain': None, 'mla_qr_gain': None,
         'mla_kn_gain': None, 'mla_kr_gain': None, 'ffn_conv_w': 2, 'ffn_conv_b': None}


def _cparams(sem):
    return pltpu.CompilerParams(dimension_semantics=sem, vmem_limit_bytes=VMEM_LIMIT)


def _tile(n, cands):
    for c in cands:
        if c <= n and n % c == 0:
            return c
    return n


WIDE = (1408, 1024, 768, 640, 512, 384, 256, 128)


def _dims(arr):
    if len(arr.shape) == 2:
        return arr.shape[0], arr.shape[1], arr.shape[1]
    return arr.shape[1], arr.shape[0] * arr.shape[2], arr.shape[2]


def _pspec(arr, tr, tc, fn):
    if len(arr.shape) == 2:
        return pl.BlockSpec((tr, tc), fn)
    per = arr.shape[2] // tc

    def im(*g):
        r, c = fn(*g)
        return (c // per, r, c % per)

    return pl.BlockSpec((None, tr, tc), im)


class _Comm:
    def __init__(self, ins, outs, aliases, sems, start, finish):
        self.ins, self.outs, self.aliases, self.sems, self.start, self.finish = ins, outs, aliases, sems, start, finish


def _call(body, grid, sem, in_specs, out_specs, out_shape, scratch, args, name, comm=None):
    if comm is None:
        return list(pl.pallas_call(body, grid=grid, in_specs=in_specs, out_specs=out_specs, out_shape=out_shape,
                                   scratch_shapes=scratch, compiler_params=_cparams(sem), name=name)(*args))
    ni, no, ns, nci, nco = len(args), len(out_shape), len(scratch), len(comm.ins), len(comm.outs)

    def wrapped(*refs):
        core_in, c_in = refs[:ni], refs[ni:ni + nci]
        core_out, c_out = refs[ni + nci:ni + nci + no], refs[ni + nci + no:ni + nci + no + nco]
        rest = refs[ni + nci + no + nco:]
        ids = [pl.program_id(d) for d in range(len(grid))]
        first = functools.reduce(lambda p, q: p & q, [i == 0 for i in ids])
        last = functools.reduce(lambda p, q: p & q, [i == n - 1 for i, n in zip(ids, grid)])

        @pl.when(first)
        def _():
            comm.start(c_in, c_out, rest[ns:])

        body(*core_in, *core_out, *rest[:ns])

        @pl.when(last)
        def _():
            comm.finish(c_in, c_out, rest[ns:])

    hbm = pl.BlockSpec(memory_space=pl.ANY)
    res = pl.pallas_call(
        wrapped, grid=grid, in_specs=list(in_specs) + [hbm] * nci, out_specs=list(out_specs) + [hbm] * nco,
        out_shape=list(out_shape) + list(comm.outs), scratch_shapes=list(scratch) + list(comm.sems),
        input_output_aliases={ni + i: no + o for i, o in comm.aliases.items()},
        compiler_params=_cparams(("arbitrary",) * len(grid)), name=name)(*args, *comm.ins)
    return list(res[:no]), list(res[no:])


MM_VMEM_BUDGET = 40 << 20
_CONTRACT = {"nn": (((1,), (0,)), ((), ())), "nt": (((1,), (1,)), ((), ())), "tn": (((0,), (0,)), ((), ()))}


def _mm(kind, a, b, out_dtype, name, res=None, out_panels=1, comm=None):
    ar, ac, aw = _dims(a)
    br, bc, bw = _dims(b)
    if kind == "nn":
        M, K, N, mw, kw, nw = ar, ac, bc, ar, aw, bw
    elif kind == "nt":
        M, K, N, mw, kw, nw = ar, ac, br, ar, math.gcd(aw, bw), br
    else:
        M, K, N, mw, kw, nw = ac, ar, bc, aw, ar, bw
    assert K == (br if kind != "nt" else bc)
    no = N // out_panels
    tm = _tile(mw, (1024, 512, 256, 128))
    tn = _tile(math.gcd(nw, no), WIDE if kind != "nt" else (1024, 512, 256, 128))
    fixed = tm * tn * (2 * jnp.dtype(out_dtype).itemsize + 4 + (8 if res is not None else 0))
    per_k = 2 * (tm * a.dtype.itemsize + tn * b.dtype.itemsize)
    tk = next((c for c in (kw, 4096, 2816, 2048, 1408, 1024, 768, 640, 512, 384, 256, 128)
               if c <= kw and kw % c == 0 and fixed + c * per_k <= MM_VMEM_BUDGET), _tile(kw, (128,)))
    nk = K // tk
    o_sds = jax.ShapeDtypeStruct((M, N) if out_panels == 1 else (out_panels, M, no), out_dtype)

    def body(*refs):
        a_ref, b_ref = refs[:2]

        def out(r):
            if res is not None:
                r = r + refs[2][...]
            return r.astype(out_dtype)

        prod = lax.dot_general(a_ref[...].astype(BF16), b_ref[...].astype(BF16), _CONTRACT[kind], preferred_element_type=F32)
        if nk == 1:
            refs[-1][...] = out(prod)
            return
        o_ref, acc = refs[-2:]
        k = pl.program_id(2)

        @pl.when(k == 0)
        def _():
            acc[...] = prod

        @pl.when(k > 0)
        def _():
            acc[...] += prod

        @pl.when(k == nk - 1)
        def _():
            o_ref[...] = out(acc[...])

    if kind == "nn":
        specs = [_pspec(a, tm, tk, lambda i, j, k: (i, k)), _pspec(b, tk, tn, lambda i, j, k: (k, j))]
    elif kind == "nt":
        specs = [_pspec(a, tm, tk, lambda i, j, k: (i, k)), _pspec(b, tn, tk, lambda i, j, k: (j, k))]
    else:
        specs = [_pspec(a, tk, tm, lambda i, j, k: (k, i)), _pspec(b, tk, tn, lambda i, j, k: (k, j))]
    args = [a, b]
    if res is not None:
        specs.append(_pspec(res, tm, tn, lambda i, j, k: (i, j)))
        args.append(res)
    got = _call(body, (M // tm, N // tn, nk), ("parallel", "parallel", "arbitrary"), specs,
                [_pspec(o_sds, tm, tn, lambda i, j, k: (i, j))], [o_sds], [pltpu.VMEM((tm, tn), F32)] if nk > 1 else [],
                args, name, comm)
    return got[0] if comm is None else (got[0][0], got[1])


def _mm_nn(a, b, out_dtype, name, **kw):
    return _mm("nn", a, b, out_dtype, name, **kw)


def _mm_nt(a, b, out_dtype, name, **kw):
    return _mm("nt", a, b, out_dtype, name, **kw)


def _mm_tn(a, b, out_dtype, name, **kw):
    return _mm("tn", a, b, out_dtype, name, **kw)


def _swap_halves(y):
    h = y.shape[-1] // 2
    return jnp.concatenate([y[:, h:], y[:, :h]], axis=1)


def _row_tile(R, d, limit=None):
    cap = max(8, (1 << 19) // d)
    cands = [c for c in (4096, 2048, 1024, 512, 256, 128, 64, 32, 16, 8) if c <= cap]
    if limit is not None:
        cands = [c for c in cands if limit % c == 0]
    return _tile(R, cands)


def _rms_fwd(x, g, out_dtype, name, scale=1.0, rope=None):
    R, d = x.shape
    tr = _row_tile(R, d, None if rope is None else rope[0].shape[0])

    def body(*refs):
        x_ref, g_ref = refs[:2]
        o_ref = refs[-1]
        xv = x_ref[...].astype(F32)
        y = xv * lax.rsqrt(jnp.mean(xv * xv, axis=-1, keepdims=True) + EPS)
        y = y * g_ref[...]
        if rope is not None:
            y = y * refs[2][...] + _swap_halves(y) * refs[3][...]
        if scale != 1.0:
            y = y * scale
        o_ref[...] = y.astype(o_ref.dtype)

    in_specs = [pl.BlockSpec((tr, d), lambda i: (i, 0)), pl.BlockSpec((1, d), lambda i: (0, 0))]
    args = [x, g]
    if rope is not None:
        nrt = rope[0].shape[0] // tr
        in_specs += [pl.BlockSpec((tr, d), lambda i: (i % nrt, 0))] * 2
        args += list(rope)
    return pl.pallas_call(
        body, grid=(R // tr,), in_specs=in_specs, out_specs=pl.BlockSpec((tr, d), lambda i: (i, 0)),
        out_shape=jax.ShapeDtypeStruct((R, d), out_dtype), compiler_params=_cparams(("parallel",)), name=name)(*args)


def _rms_bwd(x, g, dy, name, scale=1.0, rope=None, res=None):
    R, d = x.shape
    tr = _row_tile(R, d, None if rope is None else rope[0].shape[0])

    def body(*refs):
        x_ref, g_ref, dy_ref = refs[:3]
        dx_ref, dg_ref = refs[-2:]
        i = pl.program_id(0)
        xv = x_ref[...].astype(F32)
        r = lax.rsqrt(jnp.mean(xv * xv, axis=-1, keepdims=True) + EPS)
        xhat = xv * r
        dyv = dy_ref[...].astype(F32)
        if scale != 1.0:
            dyv = dyv * scale
        if rope is not None:
            dyv = dyv * refs[3][...] + _swap_halves(dyv * refs[4][...])

        @pl.when(i == 0)
        def _():
            dg_ref[...] = jnp.zeros_like(dg_ref)

        dg_ref[...] += jnp.sum(dyv * xhat, axis=0, keepdims=True)
        dxh = dyv * g_ref[...]
        dx = r * (dxh - xhat * jnp.mean(dxh * xhat, axis=-1, keepdims=True))
        if res is not None:
            dx = dx + refs[-3][...]
        dx_ref[...] = dx

    row = pl.BlockSpec((tr, d), lambda i: (i, 0))
    vec = pl.BlockSpec((1, d), lambda i: (0, 0))
    in_specs = [row, vec, row]
    args = [x, g, dy]
    if rope is not None:
        nrt = rope[0].shape[0] // tr
        in_specs += [pl.BlockSpec((tr, d), lambda i: (i % nrt, 0))] * 2
        args += list(rope)
    if res is not None:
        in_specs.append(row)
        args.append(res)
    return pl.pallas_call(
        body, grid=(R // tr,), in_specs=in_specs, out_specs=[row, vec],
        out_shape=[jax.ShapeDtypeStruct((R, d), F32), jax.ShapeDtypeStruct((1, d), F32)],
        compiler_params=_cparams(("arbitrary",)), name=name)(*args)


PAD = 16


def _pool_win(x, fwd, out_dtype, name):
    S, D = x.shape
    G = len(POOL_WINDOWS)
    dg = D // G
    tc = _tile(dg, (128,))
    nt = dg // tc
    rc = _tile(S, (512,))

    def body(*refs):
        x_refs, o_refs, scr = refs[:G], refs[G:2 * G], refs[2 * G]
        zeros = jnp.zeros((PAD, tc), F32)
        scr[pl.ds(0, PAD), :] = zeros
        scr[pl.ds(PAD + S, PAD), :] = zeros
        for gi, w in enumerate(POOL_WINDOWS):
            left, right = w // 2, w - 1 - w // 2

            def count(r0):
                t = r0 + lax.broadcasted_iota(jnp.int32, (rc, 1), 0)
                return (jnp.minimum(t + right + 1, S) - jnp.maximum(t - left, 0)).astype(F32)

            for r0 in range(0, S, rc):
                xv = x_refs[gi][pl.ds(r0, rc), :]
                scr[pl.ds(PAD + r0, rc), :] = xv if fwd else xv / count(r0)
            lo, hi = (left, right) if fwd else (right, left)
            for r0 in range(0, S, rc):
                acc = scr[pl.ds(PAD + r0 - lo, rc), :]
                for o in range(-lo + 1, hi + 1):
                    acc = acc + scr[pl.ds(PAD + r0 + o, rc), :]
                xv = x_refs[gi][pl.ds(r0, rc), :]
                out = acc / count(r0) - xv if fwd else acc - xv
                o_refs[gi][pl.ds(r0, rc), :] = out.astype(out_dtype)

    in_specs = [pl.BlockSpec((S, tc), functools.partial(lambda j, gi: (0, gi * nt + j), gi=gi)) for gi in range(G)]
    out_specs = [pl.BlockSpec((S, tc), lambda j: (0, j)) for _ in range(G)]
    return pl.pallas_call(
        body, grid=(nt,), in_specs=in_specs, out_specs=out_specs,
        out_shape=[jax.ShapeDtypeStruct((S, dg), out_dtype) for _ in range(G)],
        scratch_shapes=[pltpu.VMEM((S + 2 * PAD, tc), F32)],
        compiler_params=_cparams(("parallel",)), name=name)(*([x] * G))


def _scale_res(x, y, scale, name):
    S, D = x.shape
    tr = _row_tile(S, D)
    row = pl.BlockSpec((tr, D), lambda i: (i, 0))

    def body(x_ref, y_ref, s_ref, o_ref):
        o_ref[...] = x_ref[...] + y_ref[...] * s_ref[...]

    return pl.pallas_call(body, grid=(S // tr,), in_specs=[row, row, pl.BlockSpec((1, D), lambda i: (0, 0))],
                          out_specs=row, out_shape=jax.ShapeDtypeStruct((S, D), F32),
                          compiler_params=_cparams(("parallel",)), name=name)(x, y, scale)


def _scale_bwd(dy, y, scale, name):
    S, D = dy.shape
    tr = _row_tile(S, D)
    row = pl.BlockSpec((tr, D), lambda i: (i, 0))
    vec = pl.BlockSpec((1, D), lambda i: (0, 0))

    def body(dy_ref, y_ref, s_ref, o_ref, ds_ref):
        @pl.when(pl.program_id(0) == 0)
        def _():
            ds_ref[...] = jnp.zeros_like(ds_ref)

        d = dy_ref[...]
        ds_ref[...] += jnp.sum(d * y_ref[...], axis=0, keepdims=True)
        o_ref[...] = (d * s_ref[...]).astype(BF16)

    return pl.pallas_call(body, grid=(S // tr,), in_specs=[row, row, vec], out_specs=[row, vec],
                          out_shape=[jax.ShapeDtypeStruct((S, D), BF16), jax.ShapeDtypeStruct((1, D), F32)],
                          compiler_params=_cparams(("arbitrary",)), name=name)(dy, y, scale)


GPAD = 8


def _sigmoid(z):
    return 1.0 / (1.0 + jnp.exp(-z))


def _glu_fwd(u, cw, cb, name):
    _, S, F = u.shape
    tc = _tile(F, (128,))
    rc = _tile(S, (512,))

    def body(u_ref, w_ref, b_ref, o_ref, scr):
        zeros = jnp.zeros((GPAD, tc), F32)
        scr[pl.ds(0, GPAD), :] = zeros
        scr[pl.ds(GPAD + S, GPAD), :] = zeros
        for r0 in range(0, S, rc):
            scr[pl.ds(GPAD + r0, rc), :] = u_ref[0, pl.ds(r0, rc), :]
        w0, w1, w2, b = w_ref[0:1, :], w_ref[1:2, :], w_ref[2:3, :], b_ref[...]
        for r0 in range(0, S, rc):
            gc = (scr[pl.ds(GPAD + r0 - 1, rc), :] * w0 + scr[pl.ds(GPAD + r0, rc), :] * w1
                  + scr[pl.ds(GPAD + r0 + 1, rc), :] * w2 + b)
            o_ref[pl.ds(r0, rc), :] = (gc * _sigmoid(gc) * u_ref[1, pl.ds(r0, rc), :]).astype(BF16)

    return pl.pallas_call(
        body, grid=(F // tc,),
        in_specs=[pl.BlockSpec((2, S, tc), lambda j: (0, 0, j)), pl.BlockSpec((3, tc), lambda j: (0, j)),
                  pl.BlockSpec((1, tc), lambda j: (0, j))],
        out_specs=pl.BlockSpec((S, tc), lambda j: (0, j)), out_shape=jax.ShapeDtypeStruct((S, F), BF16),
        scratch_shapes=[pltpu.VMEM((S + 2 * GPAD, tc), F32)],
        compiler_params=_cparams(("parallel",)), name=name)(u, cw, cb)


def _glu_bwd(u, cw, cb, dact, name):
    _, S, F = u.shape
    tc = _tile(F, (128,))
    rc = _tile(S, (512,))

    def body(u_ref, w_ref, b_ref, da_ref, du_ref, dw_ref, db_ref, scr_g, scr_d):
        zeros = jnp.zeros((GPAD, tc), F32)
        for scr in (scr_g, scr_d):
            scr[pl.ds(0, GPAD), :] = zeros
            scr[pl.ds(GPAD + S, GPAD), :] = zeros
        for r0 in range(0, S, rc):
            scr_g[pl.ds(GPAD + r0, rc), :] = u_ref[0, pl.ds(r0, rc), :]
        w0, w1, w2, b = w_ref[0:1, :], w_ref[1:2, :], w_ref[2:3, :], b_ref[...]
        sums = [jnp.zeros((1, tc), F32) for _ in range(4)]
        for r0 in range(0, S, rc):
            gp = scr_g[pl.ds(GPAD + r0 - 1, rc), :]
            g0 = scr_g[pl.ds(GPAD + r0, rc), :]
            gn = scr_g[pl.ds(GPAD + r0 + 1, rc), :]
            gc = gp * w0 + g0 * w1 + gn * w2 + b
            sig = _sigmoid(gc)
            da = da_ref[pl.ds(r0, rc), :]
            du_ref[1, pl.ds(r0, rc), :] = (da * (gc * sig)).astype(BF16)
            dgc = da * u_ref[1, pl.ds(r0, rc), :] * (sig * (1.0 + gc * (1.0 - sig)))
            scr_d[pl.ds(GPAD + r0, rc), :] = dgc
            for n, t in enumerate((dgc * gp, dgc * g0, dgc * gn, dgc)):
                sums[n] = sums[n] + jnp.sum(t, axis=0, keepdims=True)
        dw_ref[...] = jnp.concatenate(sums[:3], axis=0)
        db_ref[...] = sums[3]
        for r0 in range(0, S, rc):
            dg = (scr_d[pl.ds(GPAD + r0 + 1, rc), :] * w0 + scr_d[pl.ds(GPAD + r0, rc), :] * w1
                  + scr_d[pl.ds(GPAD + r0 - 1, rc), :] * w2)
            du_ref[0, pl.ds(r0, rc), :] = dg.astype(BF16)

    col = pl.BlockSpec((S, tc), lambda j: (0, j))
    return pl.pallas_call(
        body, grid=(F // tc,),
        in_specs=[pl.BlockSpec((2, S, tc), lambda j: (0, 0, j)), pl.BlockSpec((3, tc), lambda j: (0, j)),
                  pl.BlockSpec((1, tc), lambda j: (0, j)), col],
        out_specs=[pl.BlockSpec((2, S, tc), lambda j: (0, 0, j)), pl.BlockSpec((3, tc), lambda j: (0, j)),
                   pl.BlockSpec((1, tc), lambda j: (0, j))],
        out_shape=[jax.ShapeDtypeStruct((2, S, F), BF16), jax.ShapeDtypeStruct((3, F), F32),
                   jax.ShapeDtypeStruct((1, F), F32)],
        scratch_shapes=[pltpu.VMEM((S + 2 * GPAD, tc), F32), pltpu.VMEM((S + 2 * GPAD, tc), F32)],
        compiler_params=_cparams(("parallel",)), name=name)(u, cw, cb, dact)


def _dot_nt(a, b):
    return lax.dot_general(a, b, (((1,), (1,)), ((), ())), preferred_element_type=F32)


def _dot_tn(a, b):
    return lax.dot_general(a, b, (((0,), (0,)), ((), ())), preferred_element_type=F32)


def _swa_specs(S, nq, G, hd, bq):
    prev = lambda j: jnp.maximum(j - 1, 0)
    nxt = lambda j: jnp.minimum(j + 1, nq - 1)
    kv = [pl.BlockSpec((None, bq, hd), functools.partial(lambda kh, j, f: (kh, f(j), 0), f=f))
          for f in (prev, lambda j: j, nxt)]
    pk = [pl.BlockSpec((1, bq), functools.partial(lambda kh, j, f: (0, f(j)), f=f)) for f in (prev, lambda j: j, nxt)]
    smem = pl.BlockSpec(memory_space=pltpu.SMEM)
    return ([pl.BlockSpec((bq, G * hd), lambda kh, j: (j, kh))] + kv + kv
            + [pl.BlockSpec((bq, 1), lambda kh, j: (j, 0))] + pk + [smem, smem])


def _swa_scores(j, kh, g, S, bq, G, hd, q_ref, kspan, dist, slopes, sinks):
    qi = j * bq + lax.broadcasted_iota(jnp.int32, (bq, 1), 0)
    ki = (j - 1) * bq + lax.broadcasted_iota(jnp.int32, (1, 3 * bq), 1)
    ok = (jnp.abs(qi - ki) <= SWA_WINDOW) & (ki >= 0) & (ki < S)
    qg = q_ref[:, g * hd:(g + 1) * hd]
    s = _dot_nt(qg, kspan) - slopes[kh * G + g] * dist
    return qg, jnp.where(ok, s, NEG), sinks[kh * G + g]


def _swa_fwd_attn(qn, kT, vT, posq, posk, slopes, sinks, name, comm=None):
    Hkv, S, hd = kT.shape
    G = qn.shape[1] // (Hkv * hd)
    bq = SWA_WINDOW
    nq = S // bq

    def body(q_ref, k0, k1, k2, v0, v1, v2, pq, p0, p1, p2, slopes_ref, sinks_ref, o_ref, l_ref):
        kh, j = pl.program_id(0), pl.program_id(1)
        kspan = jnp.concatenate([k0[...], k1[...], k2[...]], axis=0)
        vspan = jnp.concatenate([v0[...], v1[...], v2[...]], axis=0)
        dist = jnp.abs(pq[...] - jnp.concatenate([p0[...], p1[...], p2[...]], axis=1))
        outs, lses = [], []
        for g in range(G):
            _, s, sink = _swa_scores(j, kh, g, S, bq, G, hd, q_ref, kspan, dist, slopes_ref, sinks_ref)
            m = jnp.maximum(jnp.max(s, axis=-1, keepdims=True), sink)
            p = jnp.exp(s - m)
            den = jnp.sum(p, axis=-1, keepdims=True) + jnp.exp(sink - m)
            outs.append(jnp.dot((p / den).astype(BF16), vspan, preferred_element_type=F32))
            lses.append(m + jnp.log(den))
        o_ref[...] = jnp.concatenate(outs, axis=1)
        l_ref[...] = jnp.concatenate(lses, axis=1)

    return _call(
        body, (Hkv, nq), ("parallel", "parallel"), _swa_specs(S, nq, G, hd, bq),
        [pl.BlockSpec((bq, G * hd), lambda kh, j: (j, kh)), pl.BlockSpec((None, bq, G), lambda kh, j: (kh, j, 0))],
        [jax.ShapeDtypeStruct(qn.shape, F32), jax.ShapeDtypeStruct((Hkv, S, G), F32)], [],
        [qn, kT, kT, kT, vT, vT, vT, posq, posk, posk, posk, slopes, sinks], name, comm)


def _swa_bwd_attn(qn, kT, vT, posq, posk, slopes, sinks, o, lse, do, name):
    Hkv, S, hd = kT.shape
    G = qn.shape[1] // (Hkv * hd)
    bq = SWA_WINDOW
    nq = S // bq

    def body(q_ref, k0, k1, k2, v0, v1, v2, pq, p0, p1, p2, slopes_ref, sinks_ref, o_ref, l_ref, do_ref,
             dq_ref, dk_ref, dv_ref, ds_ref):
        kh, j = pl.program_id(0), pl.program_id(1)
        kspan = jnp.concatenate([k0[...], k1[...], k2[...]], axis=0)
        vspan = jnp.concatenate([v0[...], v1[...], v2[...]], axis=0)
        dist = jnp.abs(pq[...] - jnp.concatenate([p0[...], p1[...], p2[...]], axis=1))
        dk = jnp.zeros((3 * bq, hd), F32)
        dv = jnp.zeros((3 * bq, hd), F32)
        dqs, dsinks = [], []
        for g in range(G):
            qg, s, sink = _swa_scores(j, kh, g, S, bq, G, hd, q_ref, kspan, dist, slopes_ref, sinks_ref)
            lg = l_ref[:, g:g + 1]
            p = jnp.exp(s - lg)
            dog = do_ref[:, g * hd:(g + 1) * hd]
            delta = jnp.sum(dog * o_ref[:, g * hd:(g + 1) * hd], axis=-1, keepdims=True)
            dog = dog.astype(BF16)
            dsc = (p * (_dot_nt(dog, vspan) - delta)).astype(BF16)
            dqs.append(jnp.dot(dsc, kspan, preferred_element_type=F32))
            dk = dk + _dot_tn(dsc, qg)
            dv = dv + _dot_tn(p.astype(BF16), dog)
            dsinks.append(jnp.sum(-jnp.exp(sink - lg) * delta, axis=0, keepdims=True))
        dq_ref[...] = jnp.concatenate(dqs, axis=1)
        dk_ref[...] = dk
        dv_ref[...] = dv

        @pl.when(j == 0)
        def _():
            ds_ref[...] = jnp.zeros_like(ds_ref)

        ds_ref[...] += jnp.concatenate(dsinks, axis=1)

    qblk = pl.BlockSpec((bq, G * hd), lambda kh, j: (j, kh))
    span = pl.BlockSpec((None, None, 3 * bq, hd), lambda kh, j: (kh, j, 0, 0))
    return pl.pallas_call(
        body, grid=(Hkv, nq),
        in_specs=_swa_specs(S, nq, G, hd, bq) + [qblk, pl.BlockSpec((None, bq, G), lambda kh, j: (kh, j, 0)), qblk],
        out_specs=[qblk, span, span, pl.BlockSpec((None, 1, G), lambda kh, j: (kh, 0, 0))],
        out_shape=[jax.ShapeDtypeStruct(qn.shape, F32), jax.ShapeDtypeStruct((Hkv, nq, 3 * bq, hd), F32),
                   jax.ShapeDtypeStruct((Hkv, nq, 3 * bq, hd), F32), jax.ShapeDtypeStruct((Hkv, 1, G), F32)],
        compiler_params=_cparams(("parallel", "arbitrary")), name=name)(
            qn, kT, kT, kT, vT, vT, vT, posq, posk, posk, posk, slopes, sinks, o, lse, do)


def _overlap3(spans, name):
    Hkv, nq, bq3, hd = spans.shape
    bq = bq3 // 3
    sp = spans.reshape(Hkv, nq, 3, bq, hd)

    def body(a_ref, b_ref, c_ref, o_ref):
        b = pl.program_id(1)
        acc = b_ref[...]
        acc = acc + jnp.where(b > 0, a_ref[...], 0.0)
        acc = acc + jnp.where(b < nq - 1, c_ref[...], 0.0)
        o_ref[...] = acc

    def part(f, slot):
        return pl.BlockSpec((None, None, None, bq, hd), lambda kh, b: (kh, f(b), slot, 0, 0))

    return pl.pallas_call(
        body, grid=(Hkv, nq),
        in_specs=[part(lambda b: jnp.maximum(b - 1, 0), 2), part(lambda b: b, 1), part(lambda b: jnp.minimum(b + 1, nq - 1), 0)],
        out_specs=pl.BlockSpec((None, bq, hd), lambda kh, b: (kh, b, 0)),
        out_shape=jax.ShapeDtypeStruct((Hkv, nq * bq, hd), F32),
        compiler_params=_cparams(("parallel", "parallel")), name=name)(sp, sp, sp)


def _mla_fwd_attn(q, k, vT, name, comm=None):
    H, S, dk = q.shape
    dv = vT.shape[1]
    tq = _tile(S, (1024, 512, 256, 128))
    tk = _tile(S, (512, 256, 128))
    cq = _tile(tq, (256, 128))
    nk = S // tk

    def body(q_ref, k_ref, v_ref, o_ref, l_ref, m_s, l_s, acc):
        kk = pl.program_id(2)

        @pl.when(kk == 0)
        def _():
            m_s[...] = jnp.full_like(m_s, NEG)
            l_s[...] = jnp.zeros_like(l_s)
            acc[...] = jnp.zeros_like(acc)

        kb, vb = k_ref[...], v_ref[...]
        for c0 in range(0, tq, cq):
            cols = pl.ds(c0, cq)
            sT = _dot_nt(kb, q_ref[cols, :])
            m_old = m_s[:, cols]
            m_new = jnp.maximum(m_old, jnp.max(sT, axis=0, keepdims=True))
            a = jnp.exp(m_old - m_new)
            p = jnp.exp(sT - m_new)
            l_s[:, cols] = a * l_s[:, cols] + jnp.sum(p, axis=0, keepdims=True)
            acc[:, cols] = a * acc[:, cols] + jnp.dot(vb, p.astype(BF16), preferred_element_type=F32)
            m_s[:, cols] = m_new

        @pl.when(kk == nk - 1)
        def _():
            o_ref[...] = acc[...] / l_s[...]
            l_ref[...] = m_s[...] + jnp.log(l_s[...])

    return _call(
        body, (H, S // tq, nk), ("parallel", "parallel", "arbitrary"),
        [pl.BlockSpec((None, tq, dk), lambda h, i, kk: (h, i, 0)),
         pl.BlockSpec((None, tk, dk), lambda h, i, kk: (h, kk, 0)),
         pl.BlockSpec((None, dv, tk), lambda h, i, kk: (h, 0, kk))],
        [pl.BlockSpec((None, dv, tq), lambda h, i, kk: (h, 0, i)), pl.BlockSpec((None, 1, tq), lambda h, i, kk: (h, 0, i))],
        [jax.ShapeDtypeStruct((H, dv, S), F32), jax.ShapeDtypeStruct((H, 1, S), F32)],
        [pltpu.VMEM((1, tq), F32), pltpu.VMEM((1, tq), F32), pltpu.VMEM((dv, tq), F32)], [q, k, vT], name, comm)


def _mla_bwd_attn(q, k, kT, v, oT, doT, lse, name):
    H, S, dk = q.shape
    dv = v.shape[2]
    tq = _tile(S, (512, 256, 128))
    tk = _tile(S, (512, 256, 128))
    nq = S // tq

    def body(q_ref, k_ref, kT_ref, v_ref, o_ref, do_ref, l_ref, dq_ref, dk_ref, dv_ref, dk_acc, dv_acc):
        j, i = pl.program_id(1), pl.program_id(2)

        @pl.when(i == 0)
        def _():
            dk_acc[...] = jnp.zeros_like(dk_acc)
            dv_acc[...] = jnp.zeros_like(dv_acc)

        @pl.when((i == 0) & (j == 0))
        def _():
            dq_ref[...] = jnp.zeros_like(dq_ref)

        qb, dob = q_ref[...], do_ref[...]
        delta = jnp.sum(dob * o_ref[...], axis=0, keepdims=True)
        dob = dob.astype(BF16)
        pT = jnp.exp(_dot_nt(k_ref[...], qb) - l_ref[...])
        dpT = jnp.dot(v_ref[...], dob, preferred_element_type=F32)
        dsT = (pT * (dpT - delta)).astype(BF16)
        dk_acc[...] += jnp.dot(dsT, qb, preferred_element_type=F32)
        dv_acc[...] += _dot_nt(pT.astype(BF16), dob)
        cols = pl.ds(pl.multiple_of(i * tq, tq), tq)
        dq_ref[:, cols] += jnp.dot(kT_ref[...], dsT, preferred_element_type=F32)

        @pl.when(i == nq - 1)
        def _():
            dk_ref[...] = dk_acc[...]
            dv_ref[...] = dv_acc[...]

    qrow = lambda d: pl.BlockSpec((None, tq, d), lambda h, j, i: (h, i, 0))
    krow = lambda d: pl.BlockSpec((None, tk, d), lambda h, j, i: (h, j, 0))
    qcol = lambda d: pl.BlockSpec((None, d, tq), lambda h, j, i: (h, 0, i))
    return pl.pallas_call(
        body, grid=(H, S // tk, nq),
        in_specs=[qrow(dk), krow(dk), pl.BlockSpec((None, dk, tk), lambda h, j, i: (h, 0, j)), krow(dv), qcol(dv), qcol(dv),
                  qcol(1)],
        out_specs=[pl.BlockSpec((None, dk, S), lambda h, j, i: (h, 0, 0)), krow(dk), krow(dv)],
        out_shape=[jax.ShapeDtypeStruct((H, dk, S), F32), jax.ShapeDtypeStruct((H, S, dk), F32),
                   jax.ShapeDtypeStruct((H, S, dv), F32)],
        scratch_shapes=[pltpu.VMEM((tk, dk), F32), pltpu.VMEM((tk, dv), F32)],
        compiler_params=_cparams(("parallel", "arbitrary", "arbitrary")), name=name)(q, k, kT, v, oT, doT, lse)


def _sum_heads(x, name):
    H, S, d = x.shape
    ts = _tile(S, (1024, 512, 256, 128))

    def body(x_ref, o_ref):
        acc = x_ref[0]
        for h in range(1, H):
            acc = acc + x_ref[h]
        o_ref[...] = acc

    return pl.pallas_call(body, grid=(S // ts,), in_specs=[pl.BlockSpec((H, ts, d), lambda i: (0, i, 0))],
                          out_specs=pl.BlockSpec((ts, d), lambda i: (i, 0)), out_shape=jax.ShapeDtypeStruct((S, d), F32),
                          compiler_params=_cparams(("parallel",)), name=name)(x)


def _loss_head(y, target, name):
    S, D = y.shape
    tr = _row_tile(S, D)
    row = pl.BlockSpec((tr, D), lambda i: (i, 0))
    vec = pl.BlockSpec((1, D), lambda i: (0, 0))

    def body(y_ref, t_ref, sq_ref, dy_ref):
        @pl.when(pl.program_id(0) == 0)
        def _():
            sq_ref[...] = jnp.zeros_like(sq_ref)

        e = y_ref[...] - t_ref[...]
        sq_ref[...] += jnp.sum(e * e, axis=0, keepdims=True)
        dy_ref[...] = e / D

    return pl.pallas_call(body, grid=(S // tr,), in_specs=[row, row], out_specs=[vec, row],
                          out_shape=[jax.ShapeDtypeStruct((1, D), F32), jax.ShapeDtypeStruct((S, D), F32)],
                          compiler_params=_cparams(("arbitrary",)), name=name)(y, target)


def _sum_chips(recv, name):
    _, R, C = recv.shape
    tr = _row_tile(R, C)

    def body(r_ref, o_ref):
        acc = r_ref[0].astype(F32)
        for i in range(1, N_CHIPS):
            acc = acc + r_ref[i].astype(F32)
        o_ref[...] = acc

    return pl.pallas_call(body, grid=(R // tr,), in_specs=[pl.BlockSpec((N_CHIPS, tr, C), lambda i: (0, i, 0))],
                          out_specs=pl.BlockSpec((tr, C), lambda i: (i, 0)), out_shape=jax.ShapeDtypeStruct((R, C), F32),
                          compiler_params=_cparams(("parallel",)), name=name)(recv)


def _adamw(w, m, v, gs, name):
    R, C = w.shape
    tr = _row_tile(R, 2 * C)
    row = pl.BlockSpec((tr, C), lambda i: (i, 0))
    n = len(gs)

    def body(*refs):
        w_ref, m_ref, v_ref = refs[:3]
        g_ref, d_ref, nm_ref, nv_ref = refs[3 + n:]
        g = refs[3][...]
        for r in refs[4:3 + n]:
            g = g + r[...]
        mm = ADAM_B1 * m_ref[...] + (1.0 - ADAM_B1) * g
        vv = ADAM_B2 * v_ref[...] + (1.0 - ADAM_B2) * (g * g)
        m_hat = mm / (1.0 - ADAM_B1 ** ADAM_STEP)
        v_hat = vv / (1.0 - ADAM_B2 ** ADAM_STEP)
        g_ref[...] = g
        d_ref[...] = -ADAM_LR * (m_hat / (jnp.sqrt(v_hat) + ADAM_EPS) + ADAM_WD * w_ref[...])
        nm_ref[...] = mm
        nv_ref[...] = vv

    sds = jax.ShapeDtypeStruct((R, C), F32)
    return pl.pallas_call(body, grid=(R // tr,), in_specs=[row] * (3 + n), out_specs=[row] * 4, out_shape=[sds] * 4,
                          compiler_params=_cparams(("parallel",)), name=name)(w, m, v, *gs)


def _chip_peers():
    x, y, c = lax.axis_index("x"), lax.axis_index("y"), lax.axis_index("c")
    others = [(1 - x, y), (x, 1 - y), (1 - x, 1 - y)]
    return x, y, c, 2 * x + y, [(px, py, 2 * px + py) for px, py in others]


HALVE_MIN_BYTES = 1 << 20


def _gather_comm(srcs):
    n = len(srcs)
    shapes = [tuple(a.shape[1:] if l is not None else a.shape) for a, l in srcs]
    halved = [s[0] % 2 == 0 and int(np.prod(s)) * a.dtype.itemsize >= HALVE_MIN_BYTES for s, (a, _) in zip(shapes, srcs)]
    outs = [jax.ShapeDtypeStruct((N_CHIPS,) + s, a.dtype) for s, (a, _) in zip(shapes, srcs)]
    dma = pltpu.SemaphoreType.DMA
    sems = [dma((3 * n,)), dma((3 * n,)), dma((3 * n,)), dma((3 * n,)), dma((n,))]

    def rows(ref, t, half):
        if not halved[t]:
            return ref
        h = shapes[t][0] // 2
        return ref.at[pl.ds(half * h, h)]

    def copies(in_refs, out_refs, sem_refs):
        send, recv, fsend, frecv, lsem = sem_refs
        x, y, c, me, peers = _chip_peers()
        src = [in_refs[t] if l is None else in_refs[t].at[l] for t, (_, l) in enumerate(srcs)]
        local = [pltpu.make_async_copy(src[t], out_refs[t].at[me], lsem.at[t]) for t in range(n)]

        def ici(t, j, origin):
            px, py, _ = peers[j]
            return pltpu.make_async_remote_copy(src_ref=rows(src[t], t, c), dst_ref=rows(out_refs[t].at[origin], t, c),
                                                send_sem=send.at[3 * t + j], recv_sem=recv.at[3 * t + j],
                                                device_id=(px, py, c), device_id_type=MESH)

        def hand(t, j, half):
            blk = rows(out_refs[t].at[peers[j][2]], t, half)
            return pltpu.make_async_remote_copy(src_ref=blk, dst_ref=blk, send_sem=fsend.at[3 * t + j],
                                                recv_sem=frecv.at[3 * t + j], device_id=(x, y, 1 - c), device_id_type=MESH)

        return c, me, peers, local, ici, hand

    def start(in_refs, out_refs, sem_refs):
        c, me, peers, local, ici, hand = copies(in_refs, out_refs, sem_refs)
        for t in range(n):
            local[t].start()
            for j in range(3):
                ici(t, j, me).start()

    def finish(in_refs, out_refs, sem_refs):
        c, me, peers, local, ici, hand = copies(in_refs, out_refs, sem_refs)
        for t in range(n):
            for j in range(3):
                ici(t, j, peers[j][2]).wait_recv()
                if halved[t]:
                    hand(t, j, c).start()
        for t in range(n):
            for j in range(3):
                if halved[t]:
                    hand(t, j, 1 - c).wait_recv()
        for t in range(n):
            for j in range(3):
                ici(t, j, me).wait_send()
                if halved[t]:
                    hand(t, j, c).wait_send()
            local[t].wait()

    return _Comm([a for a, _ in srcs], outs, {}, sems, start, finish)


def _scatter_comm(items, bufs):
    n = len(items)
    dma = pltpu.SemaphoreType.DMA
    sems = [dma((3 * n,)), dma((3 * n,)), dma((n,))]

    def piece(ref, rows):
        return ref if rows is None else ref.at[pl.ds(rows[0], rows[1])]

    def copies(in_refs, out_refs, sem_refs):
        send, recv, lsem = sem_refs
        x, y, c, me, peers = _chip_peers()

        def local(t):
            _, bi, l, rows, cut = items[t]
            return pltpu.make_async_copy(piece(in_refs[t].at[me], rows if cut else None),
                                         piece(out_refs[bi].at[me, l], rows), lsem.at[t])

        def ici(t, j, origin):
            _, bi, l, rows, cut = items[t]
            px, py, pi = peers[j]
            return pltpu.make_async_remote_copy(src_ref=piece(in_refs[t].at[pi], rows if cut else None),
                                                dst_ref=piece(out_refs[bi].at[origin, l], rows),
                                                send_sem=send.at[3 * t + j], recv_sem=recv.at[3 * t + j],
                                                device_id=(px, py, c), device_id_type=MESH)

        return me, peers, local, ici

    def start(in_refs, out_refs, sem_refs):
        me, peers, local, ici = copies(in_refs, out_refs, sem_refs)
        for t in range(n):
            local(t).start()
            for j in range(3):
                ici(t, j, me).start()

    def finish(in_refs, out_refs, sem_refs):
        me, peers, local, ici = copies(in_refs, out_refs, sem_refs)
        for t in range(n):
            for j in range(3):
                ici(t, j, peers[j][2]).wait_recv()
        for t in range(n):
            for j in range(3):
                ici(t, j, me).wait_send()
            local(t).wait()

    return _Comm([it[0] for it in items] + list(bufs), [jax.ShapeDtypeStruct(b.shape, b.dtype) for b in bufs],
                 {n + k: k for k in range(len(bufs))}, sems, start, finish)


def _run_comm(comm, name):
    ni, no = len(comm.ins), len(comm.outs)

    def body(*refs):
        comm.start(refs[:ni], refs[ni:ni + no], refs[ni + no:])
        comm.finish(refs[:ni], refs[ni:ni + no], refs[ni + no:])

    hbm = pl.BlockSpec(memory_space=pl.ANY)
    return pl.pallas_call(body, in_specs=[hbm] * ni, out_specs=[hbm] * no, out_shape=list(comm.outs),
                          scratch_shapes=list(comm.sems), input_output_aliases=dict(comm.aliases), name=name)(*comm.ins)


def _swap_cores(arrs, name):
    n = len(arrs)

    def body(*refs):
        src_refs, out_refs = refs[:n], refs[n:2 * n]
        send_sems, recv_sems = refs[2 * n:]
        x, y, c = lax.axis_index("x"), lax.axis_index("y"), lax.axis_index("c")
        cps = []
        for t in range(n):
            cp = pltpu.make_async_remote_copy(src_ref=src_refs[t], dst_ref=out_refs[t], send_sem=send_sems.at[t],
                                              recv_sem=recv_sems.at[t], device_id=(x, y, 1 - c), device_id_type=MESH)
            cp.start()
            cps.append(cp)
        for cp in cps:
            cp.wait()

    hbm = pl.BlockSpec(memory_space=pl.ANY)
    return pl.pallas_call(
        body, in_specs=[hbm] * n, out_specs=[hbm] * n, out_shape=[jax.ShapeDtypeStruct(a.shape, a.dtype) for a in arrs],
        scratch_shapes=[pltpu.SemaphoreType.DMA((n,)), pltpu.SemaphoreType.DMA((n,))],
        name=name)(*arrs)


def _allreduce_small(buf, name):
    R, C = buf.shape

    def body(b_ref, o_ref, recv, send_sems, recv_sems):
        x, y, c = lax.axis_index("x"), lax.axis_index("y"), lax.axis_index("c")
        me = 4 * x + 2 * y + c
        recv[me] = b_ref[...]
        cps = []
        for mask in range(1, N_DEV):
            px, py, pc = x ^ (mask >> 2 & 1), y ^ (mask >> 1 & 1), c ^ (mask & 1)
            cp = pltpu.make_async_remote_copy(src_ref=b_ref, dst_ref=recv.at[me], send_sem=send_sems.at[mask - 1],
                                              recv_sem=recv_sems.at[mask - 1], device_id=(px, py, pc), device_id_type=MESH)
            cp.start()
            cps.append((cp, 4 * px + 2 * py + pc, (px, py, pc), mask))
        for cp, pi, dev, mask in cps:
            pltpu.make_async_remote_copy(src_ref=b_ref, dst_ref=recv.at[pi], send_sem=send_sems.at[mask - 1],
                                         recv_sem=recv_sems.at[mask - 1], device_id=dev, device_id_type=MESH).wait_recv()
        for cp, _, _, _ in cps:
            cp.wait_send()
        acc = recv[0]
        for i in range(1, N_DEV):
            acc = acc + recv[i]
        o_ref[...] = acc

    vm = pl.BlockSpec(memory_space=pltpu.VMEM)
    return pl.pallas_call(
        body, in_specs=[vm], out_specs=vm, out_shape=jax.ShapeDtypeStruct((R, C), F32),
        scratch_shapes=[pltpu.VMEM((N_DEV, R, C), F32), pltpu.SemaphoreType.DMA((N_DEV - 1,)),
                        pltpu.SemaphoreType.DMA((N_DEV - 1,))],
        compiler_params=pltpu.CompilerParams(vmem_limit_bytes=VMEM_LIMIT), name=name)(buf)


def _pack(arrs):
    parts, offs, n = [], [], 0
    for a in arrs:
        f = a.reshape(-1).astype(F32)
        k = -(-f.shape[0] // LANES) * LANES
        parts.append(jnp.pad(f, (0, k - f.shape[0])))
        offs.append(n)
        n += k
    total = -(-n // (8 * LANES)) * (8 * LANES)
    if total > n:
        parts.append(jnp.zeros((total - n,), F32))
    return jnp.concatenate(parts).reshape(-1, LANES), offs


def _unpack(buf, offs, shapes):
    flat = buf.reshape(-1)
    return [flat[o:o + int(np.prod(s))].reshape(s) for o, s in zip(offs, shapes)]


def _ride(call, riders, name, made=None):
    rider = riders.get(name)
    if rider is None:
        return call(name, None)
    build, done = rider if isinstance(rider, tuple) else rider(made)
    out, res = call(name, build())
    done(res)
    return out


def _ffn_fwd(xm, g, w_in, cw, cb, w_out, t, riders):
    h2 = _rms_fwd(xm, g, BF16, f"ffn_norm{t}")
    u = _ride(lambda n, c: _mm_nn(h2, w_in, F32, n, out_panels=2, comm=c), riders, f"ffn_in{t}")
    act = _glu_fwd(u, cw, cb, f"glu{t}")
    w_out = w_out()
    xo = _ride(lambda n, c: _mm_nn(act, w_out, F32, n, res=xm, comm=c), riders, f"ffn_out{t}")
    return xo, (h2, u, act)


def _ffn_bwd(dxo, xm, g, w_in, cw, cb, w_out, saved, t, riders, split_dwin=False):
    h2, u, act = saved
    w_out = w_out()
    made = {}
    made['ffn_w_out'] =_ride(lambda n, c: _mm_tn(act, dxo, BF16, n, comm=c), riders, f"ffn_dwout{t}", made)
    dact = _ride(lambda n, c: _mm_nt(dxo, w_out, F32, n, comm=c), riders, f"ffn_dact{t}", made)
    du, dcw, dcb = _glu_bwd(u, cw, cb, dact, f"glu_bwd{t}")
    dh2 = _ride(lambda n, c: _mm_nt(du, w_in, F32, n, comm=c), riders, f"ffn_dh{t}", made)
    if split_dwin:
        half = h2.shape[1] // 2
        made['ffn_w_in_a'] = _ride(lambda n, c: _mm_tn(h2[:, :half], du, BF16, n, out_panels=N_CHIPS, comm=c),
                                   riders, f"ffn_dwin{t}a", made)
        dw_in = (made['ffn_w_in_a'], _ride(lambda n, c: _mm_tn(h2[:, half:], du, BF16, n, out_panels=N_CHIPS, comm=c),
                                           riders, f"ffn_dwin{t}b", made))
    else:
        dw_in = _ride(lambda n, c: _mm_tn(h2, du, BF16, n, out_panels=N_CHIPS, comm=c), riders, f"ffn_dwin{t}", made)
    dxm, dg = _rms_bwd(xm, g, dh2, f"ffn_norm_bwd{t}", res=dxo)
    return dxm, dict(ffn_w_in=dw_in, ffn_conv_w=dcw, ffn_conv_b=dcb, ffn_w_out=made['ffn_w_out'], norm_ffn_g=dg)


def _pool_fwd(x, g, wg, scale, t, riders):
    G = len(POOL_WINDOWS)
    h = _rms_fwd(x, g, F32, f"pool_norm{t}")
    pooled = _pool_win(h, True, BF16, f"pool_win{t}")
    yraw = jnp.concatenate([_mm_nn(pooled[i], wg[i], F32, f"pool_mm{t}_{i}") for i in range(G)], axis=1)
    xm = _scale_res(x, yraw, scale, f"pool_out{t}")
    return xm, (pooled, yraw)


def _pool_bwd(dxm, x, g, wg, scale, saved, t):
    G = len(POOL_WINDOWS)
    pooled, yraw = saved
    dg_ = x.shape[1] // G
    dyraw, dscale = _scale_bwd(dxm, yraw, scale, f"pool_out_bwd{t}")
    parts = [dyraw[:, i * dg_:(i + 1) * dg_] for i in range(G)]
    dpool = jnp.concatenate([_mm_nt(parts[i], wg[i], F32, f"pool_dp{t}_{i}") for i in range(G)], axis=1)
    dw = jnp.stack([_mm_tn(pooled[i], parts[i], BF16, f"pool_dw{t}_{i}") for i in range(G)])
    dh = jnp.concatenate(_pool_win(dpool, False, F32, f"pool_win_bwd{t}"), axis=1)
    dx, dgn = _rms_bwd(x, g, dh, f"pool_norm_bwd{t}", res=dxm)
    return dx, dict(pool_w=dw, pool_scale=dscale, norm_mix_g=dgn)


def _swa_fwd(x, g, w_qkv, q_gain, k_gain, sinks, w_o, pos, t, riders):
    S = x.shape[0]
    Hq, Hkv, hd = SWA_HEADS, SWA_KV_HEADS, SWA_HEAD_DIM
    nq, nkv = Hq * hd, Hkv * hd
    posq, posk, slopes = pos
    h = _rms_fwd(x, g, BF16, f"swa_norm{t}")
    qkv = _mm_nn(h, w_qkv, F32, f"swa_qkv{t}")
    q = qkv[:, :nq].reshape(S * Hq, hd)
    k = qkv[:, nq:nq + nkv].reshape(S * Hkv, hd)
    qn = _rms_fwd(q, q_gain, BF16, f"swa_qnorm{t}", scale=hd ** -0.5).reshape(S, nq)
    kT = _rms_fwd(k, k_gain, BF16, f"swa_knorm{t}").reshape(S, Hkv, hd).transpose(1, 0, 2)
    vT = qkv[:, nq + nkv:].astype(BF16).reshape(S, Hkv, hd).transpose(1, 0, 2)
    o, lse = _ride(lambda n, c: _swa_fwd_attn(qn, kT, vT, posq, posk, slopes, sinks.reshape(-1), n, comm=c), riders,
                   f"swa_attn{t}")
    xm = _mm_nn(o, w_o, F32, f"swa_o{t}", res=x)
    return xm, (h, q, k, qn, kT, vT, o, lse)


def _swa_bwd(dxm, x, g, w_qkv, q_gain, k_gain, sinks, w_o, pos, saved, t):
    S = x.shape[0]
    Hq, Hkv, hd = SWA_HEADS, SWA_KV_HEADS, SWA_HEAD_DIM
    nq, nkv = Hq * hd, Hkv * hd
    posq, posk, slopes = pos
    h, q, k, qn, kT, vT, o, lse = saved
    do = _mm_nt(dxm, w_o, F32, f"swa_do{t}")
    dw_o = _mm_tn(o, dxm, BF16, f"swa_dwo{t}")
    dqn, dkp, dvp, dsink = _swa_bwd_attn(qn, kT, vT, posq, posk, slopes, sinks.reshape(-1), o, lse, do, f"swa_attn_bwd{t}")
    dkn = _overlap3(dkp, f"swa_dk{t}").transpose(1, 0, 2).reshape(S * Hkv, hd)
    dv = _overlap3(dvp, f"swa_dv{t}").transpose(1, 0, 2).reshape(S, nkv)
    dq, dqg = _rms_bwd(q, q_gain, dqn.reshape(S * Hq, hd), f"swa_qnorm_bwd{t}", scale=hd ** -0.5)
    dk, dkg = _rms_bwd(k, k_gain, dkn, f"swa_knorm_bwd{t}")
    dqkv = jnp.concatenate([dq.reshape(S, nq), dk.reshape(S, nkv), dv], axis=1)
    dh = _mm_nt(dqkv, w_qkv, F32, f"swa_dh{t}")
    dw_qkv = _mm_tn(h, dqkv, BF16, f"swa_dwqkv{t}", out_panels=N_CHIPS)
    dx, dgn = _rms_bwd(x, g, dh, f"swa_norm_bwd{t}", res=dxm)
    return dx, dict(swa_w_qkv=dw_qkv, swa_q_gain=dqg, swa_k_gain=dkg, swa_sinks=dsink.reshape(1, Hq), swa_w_o=dw_o,
                    norm_mix_g=dgn)


def _mla_fwd(x, g, w_down, q_a_gain, kv_a_gain, w_uq, w_ukv, qn_gain, qr_gain, kn_gain, kr_gain, w_o, rope, t, riders):
    S = x.shape[0]
    H, dn, R, dv, qr_, kvr = MLA_HEADS, MLA_NOPE, MLA_ROPE, MLA_V, MLA_Q_RANK, MLA_KV_RANK
    sc = (dn + R) ** -0.5
    h = _rms_fwd(x, g, BF16, f"mla_norm{t}")
    d = _mm_nn(h, w_down, F32, f"mla_down{t}")
    cq_pre, ckv_pre, kpe_pre = d[:, :qr_], d[:, qr_:qr_ + kvr], d[:, qr_ + kvr:]
    cq = _rms_fwd(cq_pre, q_a_gain, BF16, f"mla_cq{t}")
    ckv = _rms_fwd(ckv_pre, kv_a_gain, BF16, f"mla_ckv{t}")
    q3 = _mm_nn(cq, w_uq, F32, f"mla_uq{t}").reshape(S, H, dn + R).transpose(1, 0, 2)
    kv3 = _mm_nn(ckv, w_ukv, F32, f"mla_ukv{t}").reshape(S, H, dn + dv).transpose(1, 0, 2)
    qn_in, qp_in = q3[..., :dn].reshape(H * S, dn), q3[..., dn:].reshape(H * S, R)
    kn_in, v = kv3[..., :dn].reshape(H * S, dn), kv3[..., dn:].astype(BF16)
    qn = _rms_fwd(qn_in, qn_gain, BF16, f"mla_qn{t}", scale=sc).reshape(H, S, dn)
    qp = _rms_fwd(qp_in, qr_gain, BF16, f"mla_qp{t}", scale=sc, rope=rope).reshape(H, S, R)
    kn = _rms_fwd(kn_in, kn_gain, BF16, f"mla_kn{t}").reshape(H, S, dn)
    kp = _rms_fwd(kpe_pre, kr_gain, BF16, f"mla_kp{t}", rope=rope)
    qf = jnp.concatenate([qn, qp], axis=-1)
    kf = jnp.concatenate([kn, jnp.broadcast_to(kp[None], (H, S, R))], axis=-1)
    vT = v.transpose(0, 2, 1)
    oT, lse = _ride(lambda n, c: _mla_fwd_attn(qf, kf, vT, n, comm=c), riders, f"mla_attn{t}")
    oT2 = oT.reshape(H * dv, S)
    xm = _mm_tn(oT2, w_o, F32, f"mla_o{t}", res=x)
    return xm, (h, cq_pre, ckv_pre, kpe_pre, cq, ckv, qn_in, qp_in, kn_in, v, qf, kf, oT, lse)


def _mla_bwd(dxm, x, g, w_down, q_a_gain, kv_a_gain, w_uq, w_ukv, qn_gain, qr_gain, kn_gain, kr_gain, w_o, rope,
             saved, t):
    S = x.shape[0]
    H, dn, R, dv = MLA_HEADS, MLA_NOPE, MLA_ROPE, MLA_V
    sc = (dn + R) ** -0.5
    h, cq_pre, ckv_pre, kpe_pre, cq, ckv, qn_in, qp_in, kn_in, v, qf, kf, oT, lse = saved
    oT2 = oT.reshape(H * dv, S)
    doT = _mm_nt(w_o, dxm, F32, f"mla_do{t}").reshape(H, dv, S)
    dw_o = _mm_nn(oT2, dxm, BF16, f"mla_dwo{t}")
    dqT, dkf, dvv = _mla_bwd_attn(qf, kf, kf.transpose(0, 2, 1), v, oT, doT, lse, f"mla_attn_bwd{t}")
    dqf = dqT.transpose(0, 2, 1)
    dqn, dqp, dkn = dqf[..., :dn], dqf[..., dn:], dkf[..., :dn]
    dkp = _sum_heads(dkf[..., dn:], f"mla_dkp{t}")
    dqn_in, dqng = _rms_bwd(qn_in, qn_gain, dqn.reshape(H * S, dn), f"mla_qn_bwd{t}", scale=sc)
    dqp_in, dqrg = _rms_bwd(qp_in, qr_gain, dqp.reshape(H * S, R), f"mla_qp_bwd{t}", scale=sc, rope=rope)
    dkn_in, dkng = _rms_bwd(kn_in, kn_gain, dkn.reshape(H * S, dn), f"mla_kn_bwd{t}")
    dkpe_pre, dkrg = _rms_bwd(kpe_pre, kr_gain, dkp, f"mla_kp_bwd{t}", rope=rope)
    dq = jnp.concatenate([dqn_in.reshape(H, S, dn), dqp_in.reshape(H, S, R)], axis=-1).transpose(1, 0, 2).reshape(S, -1)
    dkv = jnp.concatenate([dkn_in.reshape(H, S, dn), dvv], axis=-1).transpose(1, 0, 2).reshape(S, -1)
    dcq = _mm_nt(dq, w_uq, F32, f"mla_dcq{t}")
    dw_uq = _mm_tn(cq, dq, BF16, f"mla_dwuq{t}", out_panels=N_CHIPS)
    dckv = _mm_nt(dkv, w_ukv, F32, f"mla_dckv{t}")
    dw_ukv = _mm_tn(ckv, dkv, BF16, f"mla_dwukv{t}", out_panels=N_CHIPS)
    dcq_pre, dqag = _rms_bwd(cq_pre, q_a_gain, dcq, f"mla_cq_bwd{t}")
    dckv_pre, dkvag = _rms_bwd(ckv_pre, kv_a_gain, dckv, f"mla_ckv_bwd{t}")
    dd = jnp.concatenate([dcq_pre, dckv_pre, dkpe_pre], axis=1)
    dh = _mm_nt(dd, w_down, F32, f"mla_dh{t}")
    dw_down = _mm_tn(h, dd, BF16, f"mla_dwdown{t}")
    dx, dgn = _rms_bwd(x, g, dh, f"mla_norm_bwd{t}", res=dxm)
    return dx, dict(mla_w_down=dw_down, mla_q_a_gain=dqag, mla_kv_a_gain=dkvag, mla_w_uq=dw_uq, mla_w_ukv=dw_ukv,
                    mla_qn_gain=dqng, mla_qr_gain=dqrg, mla_kn_gain=dkng, mla_kr_gain=dkrg, mla_w_o=dw_o, norm_mix_g=dgn)


def _chips_to_axis(gathered, axis):
    moved = jnp.moveaxis(gathered, 0, axis)
    shape = list(moved.shape)
    shape[axis:axis + 2] = [shape[axis] * shape[axis + 1]]
    return moved.reshape(shape)


def kernel(x, positions, norm_mix_g, norm_ffn_g, pool_w, pool_scale, swa_w_qkv, swa_q_gain, swa_k_gain, swa_sinks, swa_w_o, mla_w_down, mla_q_a_gain, mla_kv_a_gain, mla_w_uq, mla_w_ukv, mla_qn_gain, mla_qr_gain, mla_kn_gain, mla_kr_gain, mla_w_o, ffn_w_in, ffn_conv_w, ffn_conv_b, ffn_w_out, loss_target, m_norm_mix_g, m_norm_ffn_g, m_pool_w, m_pool_scale, m_swa_w_qkv, m_swa_q_gain, m_swa_k_gain, m_swa_sinks, m_swa_w_o, m_mla_w_down, m_mla_q_a_gain, m_mla_kv_a_gain, m_mla_w_uq, m_mla_w_ukv, m_mla_qn_gain, m_mla_qr_gain, m_mla_kn_gain, m_mla_kr_gain, m_mla_w_o, m_ffn_w_in, m_ffn_conv_w, m_ffn_conv_b, m_ffn_w_out, v_norm_mix_g, v_norm_ffn_g, v_pool_w, v_pool_scale, v_swa_w_qkv, v_swa_q_gain, v_swa_k_gain, v_swa_sinks, v_swa_w_o, v_mla_w_down, v_mla_q_a_gain, v_mla_kv_a_gain, v_mla_w_uq, v_mla_w_ukv, v_mla_qn_gain, v_mla_qr_gain, v_mla_kn_gain, v_mla_kr_gain, v_mla_w_o, v_ffn_w_in, v_ffn_conv_w, v_ffn_conv_b, v_ffn_w_out):
    args = dict(locals())
    W = {n: args[n] for n in WEIGHTS}
    M = {n: args["m_" + n] for n in WEIGHTS}
    V = {n: args["v_" + n] for n in WEIGHTS}
    xs = x[0]
    S, D = xs.shape
    chip = 2 * lax.axis_index("x") + lax.axis_index("y")

    big = ['pool_w', 'swa_w_qkv', 'swa_w_o', 'mla_w_down', 'mla_w_uq', 'mla_w_ukv', 'mla_w_o', 'ffn_w_in', 'ffn_w_out']
    small_sharded = [n for n in WEIGHTS if SMALL.get(n) is not None]
    mixer_w = {0: ['pool_w'], 1: ['swa_w_qkv', 'swa_w_o'], 2: ['mla_w_down', 'mla_w_uq', 'mla_w_ukv', 'mla_w_o']}
    Wb = {n: W[n].astype(BF16) for n in big}
    Wg = {}

    def mixer_keys(i):
        return [(n, i // 3) for n in mixer_w[i % 3]]

    def gather_rider(keys):
        return (lambda: _gather_comm([(Wb[n], l) for n, l in keys])), (lambda res: Wg.update(zip(keys, res)))

    assert DEPTH == 4
    keys0 = mixer_keys(0) + [('ffn_w_in', 0)]
    riders = {
        'ffn_in0': gather_rider([('ffn_w_out', 0)] + mixer_keys(1)),
        'ffn_out0': gather_rider([('ffn_w_out', 1)]),
        'swa_attn1': gather_rider([('ffn_w_in', 1)]),
        'ffn_in1': gather_rider(mixer_keys(2)),
        'ffn_out1': gather_rider([('ffn_w_out', 2)]),
        'mla_attn2': gather_rider([('ffn_w_in', 2), ('ffn_w_in', 3), ('ffn_w_out', 3)] + mixer_keys(3)),
    }
    first = _run_comm(_gather_comm([(Wb[n], l) for n, l in keys0] + [(W[n], None) for n in small_sharded]), "gather_first")
    Wg.update(zip(keys0, first))
    full = dict(W)
    for n, r in zip(small_sharded, first[len(keys0):]):
        full[n] = _chips_to_axis(r, SMALL[n])
    rows = lambda a: a.reshape((a.shape[0] * a.shape[1],) + a.shape[2:])

    posf = positions.astype(F32)
    slopes = jnp.asarray(2.0 ** (-8.0 * np.arange(1, SWA_HEADS + 1) / SWA_HEADS), dtype=F32)
    pos = (posf.reshape(S, 1), posf.reshape(1, S), slopes)
    inv = ROPE_THETA ** (-jnp.arange(0, MLA_ROPE, 2, dtype=F32) / MLA_ROPE)
    ang = posf[:, None] * inv[None, :]
    cos, sin = jnp.cos(ang), jnp.sin(ang)
    rope = (jnp.concatenate([cos, cos], axis=1), jnp.concatenate([-sin, sin], axis=1))

    margs_cache = {}

    def mixer_args(i):
        if i in margs_cache:
            return margs_cache[i]
        kind, j = i % 3, i // 3
        g = full['norm_mix_g'][i:i + 1]
        if kind == 0:
            byg = jnp.swapaxes(Wg[('pool_w', j)], 0, 1)
            a = (g, [rows(byg[k]) for k in range(len(POOL_WINDOWS))], full['pool_scale'][j:j + 1])
        elif kind == 1:
            a = (g, Wg[('swa_w_qkv', j)], full['swa_q_gain'][j:j + 1], full['swa_k_gain'][j:j + 1],
                 full['swa_sinks'][j:j + 1], rows(Wg[('swa_w_o', j)]), pos)
        else:
            a = (g, rows(Wg[('mla_w_down', j)]), full['mla_q_a_gain'][j:j + 1], full['mla_kv_a_gain'][j:j + 1],
                 Wg[('mla_w_uq', j)], Wg[('mla_w_ukv', j)], full['mla_qn_gain'][j:j + 1], full['mla_qr_gain'][j:j + 1],
                 full['mla_kn_gain'][j:j + 1], full['mla_kr_gain'][j:j + 1], rows(Wg[('mla_w_o', j)]), rope)
        margs_cache[i] = (kind, a)
        return kind, a

    def ffn_args(i):
        return (full['norm_ffn_g'][i:i + 1], Wg[('ffn_w_in', i)], full['ffn_conv_w'][i], full['ffn_conv_b'][i:i + 1],
                lambda: rows(Wg[('ffn_w_out', i)]))

    fwd = (_pool_fwd, _swa_fwd, _mla_fwd)
    bwd = (_pool_bwd, _swa_bwd, _mla_bwd)
    tape = []
    cur = xs
    for i in range(DEPTH):
        kind, margs = mixer_args(i)
        xm, msaved = fwd[kind](cur, *margs, i, riders)
        xo, fsaved = _ffn_fwd(xm, *ffn_args(i), i, riders)
        tape.append((cur, xm, msaved, fsaved))
        cur = xo
    sq, dcur = _loss_head(cur, loss_target[0], "loss_head")
    loss = lax.psum(0.5 / D * jnp.sum(sq), ("x", "y", "c"))

    def by_chip(n, gl):
        if n == 'pool_w':
            G, dg_ = gl.shape[0], gl.shape[1]
            return gl.reshape(G, N_CHIPS, dg_ // N_CHIPS, dg_).swapaxes(0, 1)
        if len(gl.shape) == 3:
            return gl
        return gl.reshape((N_CHIPS, gl.shape[0] // N_CHIPS) + gl.shape[1:])

    bufs = {n: lax.empty((N_CHIPS,) + tuple(W[n].shape), BF16) for n in big}

    def scatter_rider(group):
        def make(made):
            grp = group(made) if callable(group) else group
            names = list(dict.fromkeys(it[0] for it in grp))
            build = lambda: _scatter_comm([(gl, names.index(n), l, r, cut) for n, l, gl, r, cut in grp],
                                          [bufs[n] for n in names])
            return build, (lambda res: bufs.update(zip(names, res)))
        return make

    def own_w_out(i):
        return scatter_rider(lambda made: [('ffn_w_out', i, by_chip('ffn_w_out', made['ffn_w_out']), None, True)])

    grads = {n: [None] * W[n].shape[0] for n in WEIGHTS}
    half = W['ffn_w_in'].shape[1] // 2
    riders = {f'ffn_dh{DEPTH - 1}': own_w_out(DEPTH - 1)}
    for i in reversed(range(DEPTH)):
        kind, margs = mixer_args(i)
        x_in, xm, msaved, fsaved = tape[i]
        dxm, fg = _ffn_bwd(dcur, xm, *ffn_args(i), fsaved, i, riders, split_dwin=(i == 0))
        dcur, mg = bwd[kind](dxm, x_in, *margs, msaved, i)
        for n, gval in fg.items():
            grads[n][i] = gval
        for n, gval in mg.items():
            grads[n][i if n == 'norm_mix_g' else i // 3] = gval
        if i > 0:
            g_in = by_chip('ffn_w_in', fg['ffn_w_in'])
            mixer_g = [(n, i // 3, by_chip(n, mg[n]), None, True) for n in mixer_w[kind]]
            riders = {f'ffn_dwout{i - 1}': scatter_rider([('ffn_w_in', i, g_in, (0, half), True)]),
                      f'ffn_dact{i - 1}': scatter_rider([('ffn_w_in', i, g_in, (half, half), True)]),
                      f'ffn_dh{i - 1}': own_w_out(i - 1),
                      f'ffn_dwin{i - 1}' + ('a' if i == 1 else ''): scatter_rider(mixer_g)}
            if i == 1:
                riders['ffn_dwin0b'] = scatter_rider(
                    lambda made: [('ffn_w_in', 0, by_chip('ffn_w_in', made['ffn_w_in_a']), (0, half), False)])
    grad_x = dcur[None]
    last = [('ffn_w_in', 0, by_chip('ffn_w_in', grads['ffn_w_in'][0][1]), (half, half), False)]
    last += [(n, 0, by_chip(n, grads[n][0]), None, True) for n in mixer_w[0]]
    build, done = scatter_rider(last)(None)
    done(_run_comm(build(), "scatter_last"))

    recvs = [bufs[n] for n in big]
    partial = []
    for n, r in zip(big, recvs):
        C = r.shape[-1]
        partial.append(_sum_chips(r.reshape(N_CHIPS, -1, C), f"sum_{n}"))
    other = _swap_cores(partial, "swap_cores")
    out = {}
    for n, p, q in zip(big, partial, other):
        C = p.shape[1]
        res = _adamw(W[n].reshape(-1, C), M[n].reshape(-1, C), V[n].reshape(-1, C), [p, q], f"adamw_{n}")
        out[n] = [r.reshape(W[n].shape) for r in res]

    small = [n for n in WEIGHTS if n in SMALL]
    full_shapes = [tuple(full[n].shape) for n in small]
    sg = []
    for n in small:
        parts = grads[n]
        if n == 'ffn_conv_w':
            sg.append(jnp.stack(parts))
        else:
            sg.append(jnp.concatenate(parts, axis=0))
    buf, offs = _pack(sg)
    summed = _unpack(_allreduce_small(buf, "allreduce_small"), offs, full_shapes)
    gsm = []
    for n, gfull in zip(small, summed):
        ax = SMALL[n]
        if ax is not None:
            size = W[n].shape[ax]
            gfull = lax.dynamic_slice_in_dim(gfull, chip * size, size, axis=ax)
        gsm.append(gfull)
    gb, goffs = _pack(gsm)
    wb, _ = _pack([W[n] for n in small])
    mb, _ = _pack([M[n] for n in small])
    vb, _ = _pack([V[n] for n in small])
    res = _adamw(wb, mb, vb, [gb], "adamw_small")
    shapes = [tuple(W[n].shape) for n in small]
    unp = [_unpack(r, goffs, shapes) for r in res]
    for k, n in enumerate(small):
        out[n] = [u[k] for u in unp]

    return (loss, grad_x, *[out[n][0] for n in WEIGHTS], *[out[n][1] for n in WEIGHTS],
            *[out[n][2] for n in WEIGHTS], *[out[n][3] for n in WEIGHTS])
```

```python
import functools
import math

import numpy as np
import jax
import jax.numpy as jnp
from jax import lax
from jax.experimental import pallas as pl
from jax.experimental.pallas import tpu as pltpu

F32 = jnp.float32
BF16 = jnp.bfloat16

D_MODEL = 2048
SEQ = 4096
DEPTH = 4
EPS = 1e-6
POOL_WINDOWS = (2, 4, 8, 16)
SWA_HEADS = 32
SWA_KV_HEADS = 4
SWA_HEAD_DIM = 64
SWA_WINDOW = 128
MLA_HEADS = 16
MLA_NOPE = 128
MLA_ROPE = 64
MLA_V = 128
MLA_Q_RANK = 512
MLA_KV_RANK = 512
ROPE_THETA = 10000.0
D_FF = 5632
ADAM_LR = 0.001
ADAM_B1 = 0.9
ADAM_B2 = 0.999
ADAM_EPS = 1e-08
ADAM_WD = 0.01
ADAM_STEP = 10

N_CHIPS = 4
N_DEV = 8
MESH = pl.DeviceIdType.MESH
VMEM_LIMIT = 48 << 20
LANES = 128
NEG = -1e30

WEIGHTS = ['norm_mix_g', 'norm_ffn_g', 'pool_w', 'pool_scale', 'swa_w_qkv', 'swa_q_gain', 'swa_k_gain', 'swa_sinks',
           'swa_w_o', 'mla_w_down', 'mla_q_a_gain', 'mla_kv_a_gain', 'mla_w_uq', 'mla_w_ukv', 'mla_qn_gain',
           'mla_qr_gain', 'mla_kn_gain', 'mla_kr_gain', 'mla_w_o', 'ffn_w_in', 'ffn_conv_w', 'ffn_conv_b', 'ffn_w_out']
SMALL = {'norm_mix_g': None, 'norm_ffn_g': None, 'pool_scale': 1, 'swa_q_gain': None, 'swa_k_gain': None,
         'swa_sinks': None, 'mla_q_a_gain': 1, 'mla_kv_a_gain': 1, 'mla_qn_gain': None, 'mla_qr_gain': None,
         'mla_kn_gain': None, 'mla_kr_gain': None, 'ffn_conv_w': 2, 'ffn_conv_b': None}


def _cparams(sem):
    return pltpu.CompilerParams(dimension_semantics=sem, vmem_limit_bytes=VMEM_LIMIT)


def _tile(n, cands):
    for c in cands:
        if c <= n and n % c == 0:
            return c
    return n


WIDE = (1408, 1024, 768, 640, 512, 384, 256, 128)


def _dims(arr):
    if len(arr.shape) == 2:
        return arr.shape[0], arr.shape[1], arr.shape[1]
    return arr.shape[1], arr.shape[0] * arr.shape[2], arr.shape[2]


def _pspec(arr, tr, tc, fn):
    if len(arr.shape) == 2:
        return pl.BlockSpec((tr, tc), fn)
    per = arr.shape[2] // tc

    def im(*g):
        r, c = fn(*g)
        return (c // per, r, c % per)

    return pl.BlockSpec((None, tr, tc), im)


class _Comm:
    def __init__(self, ins, outs, aliases, sems, start, finish):
        self.ins, self.outs, self.aliases, self.sems, self.start, self.finish = ins, outs, aliases, sems, start, finish


def _call(body, grid, sem, in_specs, out_specs, out_shape, scratch, args, name, comm=None):
    if comm is None:
        return list(pl.pallas_call(body, grid=grid, in_specs=in_specs, out_specs=out_specs, out_shape=out_shape,
                                   scratch_shapes=scratch, compiler_params=_cparams(sem), name=name)(*args))
    ni, no, ns, nci, nco = len(args), len(out_shape), len(scratch), len(comm.ins), len(comm.outs)

    def wrapped(*refs):
        core_in, c_in = refs[:ni], refs[ni:ni + nci]
        core_out, c_out = refs[ni + nci:ni + nci + no], refs[ni + nci + no:ni + nci + no + nco]
        rest = refs[ni + nci + no + nco:]
        ids = [pl.program_id(d) for d in range(len(grid))]
        first = functools.reduce(lambda p, q: p & q, [i == 0 for i in ids])
        last = functools.reduce(lambda p, q: p & q, [i == n - 1 for i, n in zip(ids, grid)])

        @pl.when(first)
        def _():
            comm.start(c_in, c_out, rest[ns:])

        body(*core_in, *core_out, *rest[:ns])

        @pl.when(last)
        def _():
            comm.finish(c_in, c_out, rest[ns:])

    hbm = pl.BlockSpec(memory_space=pl.ANY)
    res = pl.pallas_call(
        wrapped, grid=grid, in_specs=list(in_specs) + [hbm] * nci, out_specs=list(out_specs) + [hbm] * nco,
        out_shape=list(out_shape) + list(comm.outs), scratch_shapes=list(scratch) + list(comm.sems),
        input_output_aliases={ni + i: no + o for i, o in comm.aliases.items()},
        compiler_params=_cparams(("arbitrary",) * len(grid)), name=name)(*args, *comm.ins)
    return list(res[:no]), list(res[no:])


MM_VMEM_BUDGET = 40 << 20
_CONTRACT = {"nn": (((1,), (0,)), ((), ())), "nt": (((1,), (1,)), ((), ())), "tn": (((0,), (0,)), ((), ()))}


def _mm(kind, a, b, out_dtype, name, res=None, out_panels=1, comm=None):
    ar, ac, aw = _dims(a)
    br, bc, bw = _dims(b)
    if kind == "nn":
        M, K, N, mw, kw, nw = ar, ac, bc, ar, aw, bw
    elif kind == "nt":
        M, K, N, mw, kw, nw = ar, ac, br, ar, math.gcd(aw, bw), br
    else:
        M, K, N, mw, kw, nw = ac, ar, bc, aw, ar, bw
    assert K == (br if kind != "nt" else bc)
    no = N // out_panels
    tm = _tile(mw, (1024, 512, 256, 128))
    tn = _tile(math.gcd(nw, no), WIDE if kind != "nt" else (1024, 512, 256, 128))
    fixed = tm * tn * (2 * jnp.dtype(out_dtype).itemsize + 4 + (8 if res is not None else 0))
    per_k = 2 * (tm * a.dtype.itemsize + tn * b.dtype.itemsize)
    tk = next((c for c in (kw, 4096, 2816, 2048, 1408, 1024, 768, 640, 512, 384, 256, 128)
               if c <= kw and kw % c == 0 and fixed + c * per_k <= MM_VMEM_BUDGET), _tile(kw, (128,)))
    nk = K // tk
    o_sds = jax.ShapeDtypeStruct((M, N) if out_panels == 1 else (out_panels, M, no), out_dtype)

    def body(*refs):
        a_ref, b_ref = refs[:2]

        def out(r):
            if res is not None:
                r = r + refs[2][...]
            return r.astype(out_dtype)

        prod = lax.dot_general(a_ref[...].astype(BF16), b_ref[...].astype(BF16), _CONTRACT[kind], preferred_element_type=F32)
        if nk == 1:
            refs[-1][...] = out(prod)
            return
        o_ref, acc = refs[-2:]
        k = pl.program_id(2)

        @pl.when(k == 0)
        def _():
            acc[...] = prod

        @pl.when(k > 0)
        def _():
            acc[...] += prod

        @pl.when(k == nk - 1)
        def _():
            o_ref[...] = out(acc[...])

    if kind == "nn":
        specs = [_pspec(a, tm, tk, lambda i, j, k: (i, k)), _pspec(b, tk, tn, lambda i, j, k: (k, j))]
    elif kind == "nt":
        specs = [_pspec(a, tm, tk, lambda i, j, k: (i, k)), _pspec(b, tn, tk, lambda i, j, k: (j, k))]
    else:
        specs = [_pspec(a, tk, tm, lambda i, j, k: (k, i)), _pspec(b, tk, tn, lambda i, j, k: (k, j))]
    args = [a, b]
    if res is not None:
        specs.append(_pspec(res, tm, tn, lambda i, j, k: (i, j)))
        args.append(res)
    got = _call(body, (M // tm, N // tn, nk), ("parallel", "parallel", "arbitrary"), specs,
                [_pspec(o_sds, tm, tn, lambda i, j, k: (i, j))], [o_sds], [pltpu.VMEM((tm, tn), F32)] if nk > 1 else [],
                args, name, comm)
    return got[0] if comm is None else (got[0][0], got[1])


def _mm_nn(a, b, out_dtype, name, **kw):
    return _mm("nn", a, b, out_dtype, name, **kw)


def _mm_nt(a, b, out_dtype, name, **kw):
    return _mm("nt", a, b, out_dtype, name, **kw)


def _mm_tn(a, b, out_dtype, name, **kw):
    return _mm("tn", a, b, out_dtype, name, **kw)


def _swap_halves(y):
    h = y.shape[-1] // 2
    return jnp.concatenate([y[:, h:], y[:, :h]], axis=1)


def _row_tile(R, d, limit=None):
    cap = max(8, (1 << 19) // d)
    cands = [c for c in (4096, 2048, 1024, 512, 256, 128, 64, 32, 16, 8) if c <= cap]
    if limit is not None:
        cands = [c for c in cands if limit % c == 0]
    return _tile(R, cands)


def _rms_fwd(x, g, out_dtype, name, scale=1.0, rope=None):
    R, d = x.shape
    tr = _row_tile(R, d, None if rope is None else rope[0].shape[0])

    def body(*refs):
        x_ref, g_ref = refs[:2]
        o_ref = refs[-1]
        xv = x_ref[...].astype(F32)
        y = xv * lax.rsqrt(jnp.mean(xv * xv, axis=-1, keepdims=True) + EPS)
        y = y * g_ref[...]
        if rope is not None:
            y = y * refs[2][...] + _swap_halves(y) * refs[3][...]
        if scale != 1.0:
            y = y * scale
        o_ref[...] = y.astype(o_ref.dtype)

    in_specs = [pl.BlockSpec((tr, d), lambda i: (i, 0)), pl.BlockSpec((1, d), lambda i: (0, 0))]
    args = [x, g]
    if rope is not None:
        nrt = rope[0].shape[0] // tr
        in_specs += [pl.BlockSpec((tr, d), lambda i: (i % nrt, 0))] * 2
        args += list(rope)
    return pl.pallas_call(
        body, grid=(R // tr,), in_specs=in_specs, out_specs=pl.BlockSpec((tr, d), lambda i: (i, 0)),
        out_shape=jax.ShapeDtypeStruct((R, d), out_dtype), compiler_params=_cparams(("parallel",)), name=name)(*args)


def _rms_bwd(x, g, dy, name, scale=1.0, rope=None, res=None):
    R, d = x.shape
    tr = _row_tile(R, d, None if rope is None else rope[0].shape[0])
    panels = dy.shape[0] if len(dy.shape) == 3 else 0

    def body(*refs):
        x_ref, g_ref, dy_ref = refs[:3]
        dx_ref, dg_ref = refs[-2:]
        i = pl.program_id(0)
        xv = x_ref[...].astype(F32)
        r = lax.rsqrt(jnp.mean(xv * xv, axis=-1, keepdims=True) + EPS)
        xhat = xv * r
        dyv = (jnp.concatenate([dy_ref[p] for p in range(panels)], axis=1) if panels else dy_ref[...]).astype(F32)
        if scale != 1.0:
            dyv = dyv * scale
        if rope is not None:
            dyv = dyv * refs[3][...] + _swap_halves(dyv * refs[4][...])

        @pl.when(i == 0)
        def _():
            dg_ref[...] = jnp.zeros_like(dg_ref)

        dg_ref[...] += jnp.sum(dyv * xhat, axis=0, keepdims=True)
        dxh = dyv * g_ref[...]
        dx = r * (dxh - xhat * jnp.mean(dxh * xhat, axis=-1, keepdims=True))
        if res is not None:
            dx = dx + refs[-3][...]
        dx_ref[...] = dx

    row = pl.BlockSpec((tr, d), lambda i: (i, 0))
    vec = pl.BlockSpec((1, d), lambda i: (0, 0))
    in_specs = [row, vec, pl.BlockSpec((panels, tr, d // panels), lambda i: (0, i, 0)) if panels else row]
    args = [x, g, dy]
    if rope is not None:
        nrt = rope[0].shape[0] // tr
        in_specs += [pl.BlockSpec((tr, d), lambda i: (i % nrt, 0))] * 2
        args += list(rope)
    if res is not None:
        in_specs.append(row)
        args.append(res)
    return pl.pallas_call(
        body, grid=(R // tr,), in_specs=in_specs, out_specs=[row, vec],
        out_shape=[jax.ShapeDtypeStruct((R, d), F32), jax.ShapeDtypeStruct((1, d), F32)],
        compiler_params=_cparams(("arbitrary",)), name=name)(*args)


PAD = 16


def _pool_win(x, fwd, out_dtype, name):
    G = len(POOL_WINDOWS)
    if fwd:
        S, D = x.shape
        dg = D // G
    else:
        _, S, dg = x.shape
    tc = _tile(dg, (128,))
    nt = dg // tc
    rc = _tile(S, (512,))

    def body(*refs):
        scr = refs[-1]
        o_refs = [refs[-2].at[gi] for gi in range(G)]
        x_refs = refs[:G] if fwd else [refs[0].at[gi] for gi in range(G)]
        zeros = jnp.zeros((PAD, tc), F32)
        scr[pl.ds(0, PAD), :] = zeros
        scr[pl.ds(PAD + S, PAD), :] = zeros
        for gi, w in enumerate(POOL_WINDOWS):
            left, right = w // 2, w - 1 - w // 2

            def count(r0):
                t = r0 + lax.broadcasted_iota(jnp.int32, (rc, 1), 0)
                return (jnp.minimum(t + right + 1, S) - jnp.maximum(t - left, 0)).astype(F32)

            for r0 in range(0, S, rc):
                xv = x_refs[gi][pl.ds(r0, rc), :]
                scr[pl.ds(PAD + r0, rc), :] = xv if fwd else xv / count(r0)
            lo, hi = (left, right) if fwd else (right, left)
            for r0 in range(0, S, rc):
                acc = scr[pl.ds(PAD + r0 - lo, rc), :]
                for o in range(-lo + 1, hi + 1):
                    acc = acc + scr[pl.ds(PAD + r0 + o, rc), :]
                xv = x_refs[gi][pl.ds(r0, rc), :]
                out = acc / count(r0) - xv if fwd else acc - xv
                o_refs[gi][pl.ds(r0, rc), :] = out.astype(out_dtype)

    panel = pl.BlockSpec((G, S, tc), lambda j: (0, 0, j))
    if fwd:
        in_specs = [pl.BlockSpec((S, tc), functools.partial(lambda j, gi: (0, gi * nt + j), gi=gi)) for gi in range(G)]
    else:
        in_specs = [panel]
    return pl.pallas_call(
        body, grid=(nt,), in_specs=in_specs, out_specs=panel, out_shape=jax.ShapeDtypeStruct((G, S, dg), out_dtype),
        scratch_shapes=[pltpu.VMEM((S + 2 * PAD, tc), F32)],
        compiler_params=_cparams(("parallel",)), name=name)(*([x] * G if fwd else [x]))


def _group_mm(kind, a, b, out_dtype, name):
    G = len(POOL_WINDOWS)
    if kind == "nt":
        S, dg = a.shape[0], a.shape[1] // G
    else:
        _, S, dg = a.shape
    tm = _tile(S, (1024, 512, 256, 128))
    nm = S // tm
    pan = pl.BlockSpec((None, tm, dg), lambda g, i: (g, i, 0))
    col = pl.BlockSpec((tm, dg), lambda g, i: (i, g))
    sq = pl.BlockSpec((None, dg, dg), lambda g, i: (g, 0, 0))

    def body(a_ref, b_ref, o_ref):
        prod = lax.dot_general(a_ref[...].astype(BF16), b_ref[...].astype(BF16), _CONTRACT[kind], preferred_element_type=F32)
        if kind != "tn":
            o_ref[...] = prod.astype(out_dtype)
            return
        i = pl.program_id(1)

        @pl.when(i == 0)
        def _():
            o_ref[...] = prod

        @pl.when(i > 0)
        def _():
            o_ref[...] += prod

    specs, out_spec, shape = {"nn": ([pan, sq], col, (S, G * dg)), "nt": ([col, sq], pan, (G, S, dg)),
                              "tn": ([pan, col], sq, (G, dg, dg))}[kind]
    assert kind != "tn" or out_dtype == F32
    return pl.pallas_call(body, grid=(G, nm), in_specs=specs, out_specs=out_spec,
                          out_shape=jax.ShapeDtypeStruct(shape, out_dtype),
                          compiler_params=_cparams(("parallel", "arbitrary")), name=name)(a, b)


def _scale_res(x, y, scale, name):
    S, D = x.shape
    tr = _row_tile(S, D)
    row = pl.BlockSpec((tr, D), lambda i: (i, 0))

    def body(x_ref, y_ref, s_ref, o_ref):
        o_ref[...] = x_ref[...] + y_ref[...] * s_ref[...]

    return pl.pallas_call(body, grid=(S // tr,), in_specs=[row, row, pl.BlockSpec((1, D), lambda i: (0, 0))],
                          out_specs=row, out_shape=jax.ShapeDtypeStruct((S, D), F32),
                          compiler_params=_cparams(("parallel",)), name=name)(x, y, scale)


def _scale_bwd(dy, y, scale, name):
    S, D = dy.shape
    tr = _row_tile(S, D)
    row = pl.BlockSpec((tr, D), lambda i: (i, 0))
    vec = pl.BlockSpec((1, D), lambda i: (0, 0))

    def body(dy_ref, y_ref, s_ref, o_ref, ds_ref):
        @pl.when(pl.program_id(0) == 0)
        def _():
            ds_ref[...] = jnp.zeros_like(ds_ref)

        d = dy_ref[...]
        ds_ref[...] += jnp.sum(d * y_ref[...], axis=0, keepdims=True)
        o_ref[...] = (d * s_ref[...]).astype(BF16)

    return pl.pallas_call(body, grid=(S // tr,), in_specs=[row, row, vec], out_specs=[row, vec],
                          out_shape=[jax.ShapeDtypeStruct((S, D), BF16), jax.ShapeDtypeStruct((1, D), F32)],
                          compiler_params=_cparams(("arbitrary",)), name=name)(dy, y, scale)


GPAD = 8


def _sigmoid(z):
    return 1.0 / (1.0 + jnp.exp(-z))


def _glu_fwd(u, cw, cb, name):
    _, S, F = u.shape
    tc = _tile(F, (128,))
    rc = _tile(S, (512,))

    def body(u_ref, w_ref, b_ref, o_ref, scr):
        zeros = jnp.zeros((GPAD, tc), F32)
        scr[pl.ds(0, GPAD), :] = zeros
        scr[pl.ds(GPAD + S, GPAD), :] = zeros
        for r0 in range(0, S, rc):
            scr[pl.ds(GPAD + r0, rc), :] = u_ref[0, pl.ds(r0, rc), :]
        w0, w1, w2, b = w_ref[0:1, :], w_ref[1:2, :], w_ref[2:3, :], b_ref[...]
        for r0 in range(0, S, rc):
            gc = (scr[pl.ds(GPAD + r0 - 1, rc), :] * w0 + scr[pl.ds(GPAD + r0, rc), :] * w1
                  + scr[pl.ds(GPAD + r0 + 1, rc), :] * w2 + b)
            o_ref[pl.ds(r0, rc), :] = (gc * _sigmoid(gc) * u_ref[1, pl.ds(r0, rc), :]).astype(BF16)

    return pl.pallas_call(
        body, grid=(F // tc,),
        in_specs=[pl.BlockSpec((2, S, tc), lambda j: (0, 0, j)), pl.BlockSpec((3, tc), lambda j: (0, j)),
                  pl.BlockSpec((1, tc), lambda j: (0, j))],
        out_specs=pl.BlockSpec((S, tc), lambda j: (0, j)), out_shape=jax.ShapeDtypeStruct((S, F), BF16),
        scratch_shapes=[pltpu.VMEM((S + 2 * GPAD, tc), F32)],
        compiler_params=_cparams(("parallel",)), name=name)(u, cw, cb)


def _glu_bwd(u, cw, cb, dact, name, comm=None):
    _, S, F = u.shape
    tc = _tile(F, (128,))
    rc = _tile(S, (512,))

    def body(u_ref, w_ref, b_ref, da_ref, du_ref, dw_ref, db_ref, scr_g, scr_d):
        zeros = jnp.zeros((GPAD, tc), F32)
        for scr in (scr_g, scr_d):
            scr[pl.ds(0, GPAD), :] = zeros
            scr[pl.ds(GPAD + S, GPAD), :] = zeros
        for r0 in range(0, S, rc):
            scr_g[pl.ds(GPAD + r0, rc), :] = u_ref[0, pl.ds(r0, rc), :]
        w0, w1, w2, b = w_ref[0:1, :], w_ref[1:2, :], w_ref[2:3, :], b_ref[...]
        sums = [jnp.zeros((1, tc), F32) for _ in range(4)]
        for r0 in range(0, S, rc):
            gp = scr_g[pl.ds(GPAD + r0 - 1, rc), :]
            g0 = scr_g[pl.ds(GPAD + r0, rc), :]
            gn = scr_g[pl.ds(GPAD + r0 + 1, rc), :]
            gc = gp * w0 + g0 * w1 + gn * w2 + b
            sig = _sigmoid(gc)
            da = da_ref[pl.ds(r0, rc), :]
            du_ref[1, pl.ds(r0, rc), :] = (da * (gc * sig)).astype(BF16)
            dgc = da * u_ref[1, pl.ds(r0, rc), :] * (sig * (1.0 + gc * (1.0 - sig)))
            scr_d[pl.ds(GPAD + r0, rc), :] = dgc
            for n, t in enumerate((dgc * gp, dgc * g0, dgc * gn, dgc)):
                sums[n] = sums[n] + jnp.sum(t, axis=0, keepdims=True)
        dw_ref[...] = jnp.concatenate(sums[:3], axis=0)
        db_ref[...] = sums[3]
        for r0 in range(0, S, rc):
            dg = (scr_d[pl.ds(GPAD + r0 + 1, rc), :] * w0 + scr_d[pl.ds(GPAD + r0, rc), :] * w1
                  + scr_d[pl.ds(GPAD + r0 - 1, rc), :] * w2)
            du_ref[0, pl.ds(r0, rc), :] = dg.astype(BF16)

    col = pl.BlockSpec((S, tc), lambda j: (0, j))
    return _call(
        body, (F // tc,), ("parallel",),
        [pl.BlockSpec((2, S, tc), lambda j: (0, 0, j)), pl.BlockSpec((3, tc), lambda j: (0, j)),
         pl.BlockSpec((1, tc), lambda j: (0, j)), col],
        [pl.BlockSpec((2, S, tc), lambda j: (0, 0, j)), pl.BlockSpec((3, tc), lambda j: (0, j)),
         pl.BlockSpec((1, tc), lambda j: (0, j))],
        [jax.ShapeDtypeStruct((2, S, F), BF16), jax.ShapeDtypeStruct((3, F), F32), jax.ShapeDtypeStruct((1, F), F32)],
        [pltpu.VMEM((S + 2 * GPAD, tc), F32), pltpu.VMEM((S + 2 * GPAD, tc), F32)], [u, cw, cb, dact], name, comm)


def _dot_nt(a, b):
    return lax.dot_general(a, b, (((1,), (1,)), ((), ())), preferred_element_type=F32)


def _dot_tn(a, b):
    return lax.dot_general(a, b, (((0,), (0,)), ((), ())), preferred_element_type=F32)


def _swa_specs(S, nq, G, hd, bq):
    prev = lambda j: jnp.maximum(j - 1, 0)
    nxt = lambda j: jnp.minimum(j + 1, nq - 1)
    kv = [pl.BlockSpec((None, bq, hd), functools.partial(lambda kh, j, f: (kh, f(j), 0), f=f))
          for f in (prev, lambda j: j, nxt)]
    pk = [pl.BlockSpec((1, bq), functools.partial(lambda kh, j, f: (0, f(j)), f=f)) for f in (prev, lambda j: j, nxt)]
    smem = pl.BlockSpec(memory_space=pltpu.SMEM)
    return ([pl.BlockSpec((bq, G * hd), lambda kh, j: (j, kh))] + kv + kv
            + [pl.BlockSpec((bq, 1), lambda kh, j: (j, 0))] + pk + [smem, smem])


def _swa_scores(j, kh, g, S, bq, G, hd, q_ref, kspan, dist, slopes, sinks):
    qi = j * bq + lax.broadcasted_iota(jnp.int32, (bq, 1), 0)
    ki = (j - 1) * bq + lax.broadcasted_iota(jnp.int32, (1, 3 * bq), 1)
    ok = (jnp.abs(qi - ki) <= SWA_WINDOW) & (ki >= 0) & (ki < S)
    qg = q_ref[:, g * hd:(g + 1) * hd]
    s = _dot_nt(qg, kspan) - slopes[kh * G + g] * dist
    return qg, jnp.where(ok, s, NEG), sinks[kh * G + g]


def _swa_fwd_attn(qn, kT, vT, posq, posk, slopes, sinks, name, comm=None):
    Hkv, S, hd = kT.shape
    G = qn.shape[1] // (Hkv * hd)
    bq = SWA_WINDOW
    nq = S // bq

    def body(q_ref, k0, k1, k2, v0, v1, v2, pq, p0, p1, p2, slopes_ref, sinks_ref, o_ref, l_ref):
        kh, j = pl.program_id(0), pl.program_id(1)
        kspan = jnp.concatenate([k0[...], k1[...], k2[...]], axis=0)
        vspan = jnp.concatenate([v0[...], v1[...], v2[...]], axis=0)
        dist = jnp.abs(pq[...] - jnp.concatenate([p0[...], p1[...], p2[...]], axis=1))
        outs, lses = [], []
        for g in range(G):
            _, s, sink = _swa_scores(j, kh, g, S, bq, G, hd, q_ref, kspan, dist, slopes_ref, sinks_ref)
            m = jnp.maximum(jnp.max(s, axis=-1, keepdims=True), sink)
            p = jnp.exp(s - m)
            den = jnp.sum(p, axis=-1, keepdims=True) + jnp.exp(sink - m)
            outs.append(jnp.dot((p / den).astype(BF16), vspan, preferred_element_type=F32))
            lses.append(m + jnp.log(den))
        o_ref[...] = jnp.concatenate(outs, axis=1)
        l_ref[...] = jnp.concatenate(lses, axis=1)

    return _call(
        body, (Hkv, nq), ("parallel", "parallel"), _swa_specs(S, nq, G, hd, bq),
        [pl.BlockSpec((bq, G * hd), lambda kh, j: (j, kh)), pl.BlockSpec((None, bq, G), lambda kh, j: (kh, j, 0))],
        [jax.ShapeDtypeStruct(qn.shape, F32), jax.ShapeDtypeStruct((Hkv, S, G), F32)], [],
        [qn, kT, kT, kT, vT, vT, vT, posq, posk, posk, posk, slopes, sinks], name, comm)


def _swa_bwd_attn(qn, kT, vT, posq, posk, slopes, sinks, o, lse, do, name, comm=None):
    Hkv, S, hd = kT.shape
    G = qn.shape[1] // (Hkv * hd)
    bq = SWA_WINDOW
    nq = S // bq

    def body(q_ref, k0, k1, k2, v0, v1, v2, pq, p0, p1, p2, slopes_ref, sinks_ref, o_ref, l_ref, do_ref,
             dq_ref, dk_ref, dv_ref, ds_ref):
        kh, j = pl.program_id(0), pl.program_id(1)
        kspan = jnp.concatenate([k0[...], k1[...], k2[...]], axis=0)
        vspan = jnp.concatenate([v0[...], v1[...], v2[...]], axis=0)
        dist = jnp.abs(pq[...] - jnp.concatenate([p0[...], p1[...], p2[...]], axis=1))
        dk = jnp.zeros((3 * bq, hd), F32)
        dv = jnp.zeros((3 * bq, hd), F32)
        dqs, dsinks = [], []
        for g in range(G):
            qg, s, sink = _swa_scores(j, kh, g, S, bq, G, hd, q_ref, kspan, dist, slopes_ref, sinks_ref)
            lg = l_ref[:, g:g + 1]
            p = jnp.exp(s - lg)
            dog = do_ref[:, g * hd:(g + 1) * hd]
            delta = jnp.sum(dog * o_ref[:, g * hd:(g + 1) * hd], axis=-1, keepdims=True)
            dog = dog.astype(BF16)
            dsc = (p * (_dot_nt(dog, vspan) - delta)).astype(BF16)
            dqs.append(jnp.dot(dsc, kspan, preferred_element_type=F32))
            dk = dk + _dot_tn(dsc, qg)
            dv = dv + _dot_tn(p.astype(BF16), dog)
            dsinks.append(jnp.sum(-jnp.exp(sink - lg) * delta, axis=0, keepdims=True))
        dq_ref[...] = jnp.concatenate(dqs, axis=1)
        dk_ref[...] = dk
        dv_ref[...] = dv

        @pl.when(j == 0)
        def _():
            ds_ref[...] = jnp.zeros_like(ds_ref)

        ds_ref[...] += jnp.concatenate(dsinks, axis=1)

    qblk = pl.BlockSpec((bq, G * hd), lambda kh, j: (j, kh))
    span = pl.BlockSpec((None, None, 3 * bq, hd), lambda kh, j: (kh, j, 0, 0))
    return _call(
        body, (Hkv, nq), ("parallel", "arbitrary"),
        _swa_specs(S, nq, G, hd, bq) + [qblk, pl.BlockSpec((None, bq, G), lambda kh, j: (kh, j, 0)), qblk],
        [qblk, span, span, pl.BlockSpec((None, 1, G), lambda kh, j: (kh, 0, 0))],
        [jax.ShapeDtypeStruct(qn.shape, F32), jax.ShapeDtypeStruct((Hkv, nq, 3 * bq, hd), F32),
         jax.ShapeDtypeStruct((Hkv, nq, 3 * bq, hd), F32), jax.ShapeDtypeStruct((Hkv, 1, G), F32)], [],
        [qn, kT, kT, kT, vT, vT, vT, posq, posk, posk, posk, slopes, sinks, o, lse, do], name, comm)


def _overlap3(spans, name):
    Hkv, nq, bq3, hd = spans.shape
    bq = bq3 // 3
    sp = spans.reshape(Hkv, nq, 3, bq, hd)

    def body(a_ref, b_ref, c_ref, o_ref):
        b = pl.program_id(1)
        acc = b_ref[...]
        acc = acc + jnp.where(b > 0, a_ref[...], 0.0)
        acc = acc + jnp.where(b < nq - 1, c_ref[...], 0.0)
        o_ref[...] = acc

    def part(f, slot):
        return pl.BlockSpec((None, None, None, bq, hd), lambda kh, b: (kh, f(b), slot, 0, 0))

    return pl.pallas_call(
        body, grid=(Hkv, nq),
        in_specs=[part(lambda b: jnp.maximum(b - 1, 0), 2), part(lambda b: b, 1), part(lambda b: jnp.minimum(b + 1, nq - 1), 0)],
        out_specs=pl.BlockSpec((None, bq, hd), lambda kh, b: (kh, b, 0)),
        out_shape=jax.ShapeDtypeStruct((Hkv, nq * bq, hd), F32),
        compiler_params=_cparams(("parallel", "parallel")), name=name)(sp, sp, sp)


def _mla_fwd_attn(q, k, vT, name, comm=None):
    H, S, dk = q.shape
    dv = vT.shape[1]
    tq = _tile(S, (1024, 512, 256, 128))
    tk = _tile(S, (1024, 512, 256, 128))
    cq = _tile(tq, (256, 128))
    nk = S // tk

    def body(q_ref, k_ref, v_ref, o_ref, l_ref, m_s, l_s, acc):
        kk = pl.program_id(2)

        @pl.when(kk == 0)
        def _():
            m_s[...] = jnp.full_like(m_s, NEG)
            l_s[...] = jnp.zeros_like(l_s)
            acc[...] = jnp.zeros_like(acc)

        kb, vb = k_ref[...], v_ref[...]
        for c0 in range(0, tq, cq):
            cols = pl.ds(c0, cq)
            sT = _dot_nt(kb, q_ref[cols, :])
            m_old = m_s[:, cols]
            m_new = jnp.maximum(m_old, jnp.max(sT, axis=0, keepdims=True))
            a = jnp.exp(m_old - m_new)
            p = jnp.exp(sT - m_new)
            l_s[:, cols] = a * l_s[:, cols] + jnp.sum(p, axis=0, keepdims=True)
            acc[:, cols] = a * acc[:, cols] + jnp.dot(vb, p.astype(BF16), preferred_element_type=F32)
            m_s[:, cols] = m_new

        @pl.when(kk == nk - 1)
        def _():
            o_ref[...] = acc[...] / l_s[...]
            l_ref[...] = m_s[...] + jnp.log(l_s[...])

    return _call(
        body, (H, S // tq, nk), ("parallel", "parallel", "arbitrary"),
        [pl.BlockSpec((None, tq, dk), lambda h, i, kk: (h, i, 0)),
         pl.BlockSpec((None, tk, dk), lambda h, i, kk: (h, kk, 0)),
         pl.BlockSpec((None, dv, tk), lambda h, i, kk: (h, 0, kk))],
        [pl.BlockSpec((None, dv, tq), lambda h, i, kk: (h, 0, i)), pl.BlockSpec((None, 1, tq), lambda h, i, kk: (h, 0, i))],
        [jax.ShapeDtypeStruct((H, dv, S), F32), jax.ShapeDtypeStruct((H, 1, S), F32)],
        [pltpu.VMEM((1, tq), F32), pltpu.VMEM((1, tq), F32), pltpu.VMEM((dv, tq), F32)], [q, k, vT], name, comm)


def _mla_bwd_attn(q, k, kT, v, oT, doT, lse, name):
    H, S, dk = q.shape
    dv = v.shape[2]
    tq = _tile(S, (512, 256, 128))
    tk = _tile(S, (1024, 512, 256, 128))
    nq = S // tq

    def body(q_ref, k_ref, kT_ref, v_ref, o_ref, do_ref, l_ref, dq_ref, dk_ref, dv_ref, dk_acc, dv_acc):
        j, i = pl.program_id(1), pl.program_id(2)

        @pl.when(i == 0)
        def _():
            dk_acc[...] = jnp.zeros_like(dk_acc)
            dv_acc[...] = jnp.zeros_like(dv_acc)

        @pl.when((i == 0) & (j == 0))
        def _():
            dq_ref[...] = jnp.zeros_like(dq_ref)

        qb, dob = q_ref[...], do_ref[...]
        delta = jnp.sum(dob * o_ref[...], axis=0, keepdims=True)
        dob = dob.astype(BF16)
        pT = jnp.exp(_dot_nt(k_ref[...], qb) - l_ref[...])
        dpT = jnp.dot(v_ref[...], dob, preferred_element_type=F32)
        dsT = (pT * (dpT - delta)).astype(BF16)
        dk_acc[...] += jnp.dot(dsT, qb, preferred_element_type=F32)
        dv_acc[...] += _dot_nt(pT.astype(BF16), dob)
        cols = pl.ds(pl.multiple_of(i * tq, tq), tq)
        dq_ref[:, cols] += jnp.dot(kT_ref[...], dsT, preferred_element_type=F32)

        @pl.when(i == nq - 1)
        def _():
            dk_ref[...] = dk_acc[...]
            dv_ref[...] = dv_acc[...]

    qrow = lambda d: pl.BlockSpec((None, tq, d), lambda h, j, i: (h, i, 0))
    krow = lambda d: pl.BlockSpec((None, tk, d), lambda h, j, i: (h, j, 0))
    qcol = lambda d: pl.BlockSpec((None, d, tq), lambda h, j, i: (h, 0, i))
    return pl.pallas_call(
        body, grid=(H, S // tk, nq),
        in_specs=[qrow(dk), krow(dk), pl.BlockSpec((None, dk, tk), lambda h, j, i: (h, 0, j)), krow(dv), qcol(dv), qcol(dv),
                  qcol(1)],
        out_specs=[pl.BlockSpec((None, dk, S), lambda h, j, i: (h, 0, 0)), krow(dk), krow(dv)],
        out_shape=[jax.ShapeDtypeStruct((H, dk, S), F32), jax.ShapeDtypeStruct((H, S, dk), F32),
                   jax.ShapeDtypeStruct((H, S, dv), F32)],
        scratch_shapes=[pltpu.VMEM((tk, dk), F32), pltpu.VMEM((tk, dv), F32)],
        compiler_params=_cparams(("parallel", "arbitrary", "arbitrary")), name=name)(q, k, kT, v, oT, doT, lse)


def _sum_heads(x, name):
    H, S, d = x.shape
    ts = _tile(S, (1024, 512, 256, 128))

    def body(x_ref, o_ref):
        acc = x_ref[0]
        for h in range(1, H):
            acc = acc + x_ref[h]
        o_ref[...] = acc

    return pl.pallas_call(body, grid=(S // ts,), in_specs=[pl.BlockSpec((H, ts, d), lambda i: (0, i, 0))],
                          out_specs=pl.BlockSpec((ts, d), lambda i: (i, 0)), out_shape=jax.ShapeDtypeStruct((S, d), F32),
                          compiler_params=_cparams(("parallel",)), name=name)(x)


def _loss_head(y, target, name):
    S, D = y.shape
    tr = _row_tile(S, D)
    row = pl.BlockSpec((tr, D), lambda i: (i, 0))
    vec = pl.BlockSpec((1, D), lambda i: (0, 0))

    def body(y_ref, t_ref, sq_ref, dy_ref):
        @pl.when(pl.program_id(0) == 0)
        def _():
            sq_ref[...] = jnp.zeros_like(sq_ref)

        e = y_ref[...] - t_ref[...]
        sq_ref[...] += jnp.sum(e * e, axis=0, keepdims=True)
        dy_ref[...] = e / D

    return pl.pallas_call(body, grid=(S // tr,), in_specs=[row, row], out_specs=[vec, row],
                          out_shape=[jax.ShapeDtypeStruct((1, D), F32), jax.ShapeDtypeStruct((S, D), F32)],
                          compiler_params=_cparams(("arbitrary",)), name=name)(y, target)


def _sum_chips(recv, name):
    _, R, C = recv.shape
    tr = _row_tile(R, C)

    def body(r_ref, o_ref):
        acc = r_ref[0].astype(F32)
        for i in range(1, N_CHIPS):
            acc = acc + r_ref[i].astype(F32)
        o_ref[...] = acc

    return pl.pallas_call(body, grid=(R // tr,), in_specs=[pl.BlockSpec((N_CHIPS, tr, C), lambda i: (0, i, 0))],
                          out_specs=pl.BlockSpec((tr, C), lambda i: (i, 0)), out_shape=jax.ShapeDtypeStruct((R, C), F32),
                          compiler_params=_cparams(("parallel",)), name=name)(recv)


def _adamw(w, m, v, gs, name):
    R, C = w.shape
    tr = _row_tile(R, 2 * C)
    row = pl.BlockSpec((tr, C), lambda i: (i, 0))
    n = len(gs)

    def body(*refs):
        w_ref, m_ref, v_ref = refs[:3]
        g_ref, d_ref, nm_ref, nv_ref = refs[3 + n:]
        g = refs[3][...]
        for r in refs[4:3 + n]:
            g = g + r[...]
        mm = ADAM_B1 * m_ref[...] + (1.0 - ADAM_B1) * g
        vv = ADAM_B2 * v_ref[...] + (1.0 - ADAM_B2) * (g * g)
        m_hat = mm / (1.0 - ADAM_B1 ** ADAM_STEP)
        v_hat = vv / (1.0 - ADAM_B2 ** ADAM_STEP)
        g_ref[...] = g
        d_ref[...] = -ADAM_LR * (m_hat / (jnp.sqrt(v_hat) + ADAM_EPS) + ADAM_WD * w_ref[...])
        nm_ref[...] = mm
        nv_ref[...] = vv

    sds = jax.ShapeDtypeStruct((R, C), F32)
    return pl.pallas_call(body, grid=(R // tr,), in_specs=[row] * (3 + n), out_specs=[row] * 4, out_shape=[sds] * 4,
                          compiler_params=_cparams(("parallel",)), name=name)(w, m, v, *gs)


def _chip_peers():
    x, y, c = lax.axis_index("x"), lax.axis_index("y"), lax.axis_index("c")
    others = [(1 - x, y), (x, 1 - y), (1 - x, 1 - y)]
    return x, y, c, 2 * x + y, [(px, py, 2 * px + py) for px, py in others]


HALVE_MIN_BYTES = 1 << 20


def _gather_comm(srcs):
    n = len(srcs)
    shapes = [tuple(a.shape[1:] if l is not None else a.shape) for a, l in srcs]
    halved = [s[0] % 2 == 0 and int(np.prod(s)) * a.dtype.itemsize >= HALVE_MIN_BYTES for s, (a, _) in zip(shapes, srcs)]
    outs = [jax.ShapeDtypeStruct((N_CHIPS,) + s, a.dtype) for s, (a, _) in zip(shapes, srcs)]
    dma = pltpu.SemaphoreType.DMA
    sems = [dma((3 * n,)), dma((3 * n,)), dma((3 * n,)), dma((3 * n,)), dma((n,))]

    def rows(ref, t, half):
        if not halved[t]:
            return ref
        h = shapes[t][0] // 2
        return ref.at[pl.ds(half * h, h)]

    def copies(in_refs, out_refs, sem_refs):
        send, recv, fsend, frecv, lsem = sem_refs
        x, y, c, me, peers = _chip_peers()
        src = [in_refs[t] if l is None else in_refs[t].at[l] for t, (_, l) in enumerate(srcs)]
        local = [pltpu.make_async_copy(src[t], out_refs[t].at[me], lsem.at[t]) for t in range(n)]

        def ici(t, j, origin):
            px, py, _ = peers[j]
            return pltpu.make_async_remote_copy(src_ref=rows(src[t], t, c), dst_ref=rows(out_refs[t].at[origin], t, c),
                                                send_sem=send.at[3 * t + j], recv_sem=recv.at[3 * t + j],
                                                device_id=(px, py, c), device_id_type=MESH)

        def hand(t, j, half):
            blk = rows(out_refs[t].at[peers[j][2]], t, half)
            return pltpu.make_async_remote_copy(src_ref=blk, dst_ref=blk, send_sem=fsend.at[3 * t + j],
                                                recv_sem=frecv.at[3 * t + j], device_id=(x, y, 1 - c), device_id_type=MESH)

        return c, me, peers, local, ici, hand

    def start(in_refs, out_refs, sem_refs):
        c, me, peers, local, ici, hand = copies(in_refs, out_refs, sem_refs)
        for t in range(n):
            local[t].start()
            for j in range(3):
                ici(t, j, me).start()

    def finish(in_refs, out_refs, sem_refs):
        c, me, peers, local, ici, hand = copies(in_refs, out_refs, sem_refs)
        for t in range(n):
            for j in range(3):
                ici(t, j, peers[j][2]).wait_recv()
                if halved[t]:
                    hand(t, j, c).start()
        for t in range(n):
            for j in range(3):
                if halved[t]:
                    hand(t, j, 1 - c).wait_recv()
        for t in range(n):
            for j in range(3):
                ici(t, j, me).wait_send()
                if halved[t]:
                    hand(t, j, c).wait_send()
            local[t].wait()

    return _Comm([a for a, _ in srcs], outs, {}, sems, start, finish)


def _scatter_comm(items, bufs):
    n = len(items)
    dma = pltpu.SemaphoreType.DMA
    sems = [dma((3 * n,)), dma((3 * n,)), dma((n,))]

    def piece(ref, rows):
        return ref if rows is None else ref.at[pl.ds(rows[0], rows[1])]

    def copies(in_refs, out_refs, sem_refs):
        send, recv, lsem = sem_refs
        x, y, c, me, peers = _chip_peers()

        def local(t):
            _, bi, l, rows, cut = items[t]
            return pltpu.make_async_copy(piece(in_refs[t].at[me], rows if cut else None),
                                         piece(out_refs[bi].at[me, l], rows), lsem.at[t])

        def ici(t, j, origin):
            _, bi, l, rows, cut = items[t]
            px, py, pi = peers[j]
            return pltpu.make_async_remote_copy(src_ref=piece(in_refs[t].at[pi], rows if cut else None),
                                                dst_ref=piece(out_refs[bi].at[origin, l], rows),
                                                send_sem=send.at[3 * t + j], recv_sem=recv.at[3 * t + j],
                                                device_id=(px, py, c), device_id_type=MESH)

        return me, peers, local, ici

    def start(in_refs, out_refs, sem_refs):
        me, peers, local, ici = copies(in_refs, out_refs, sem_refs)
        for t in range(n):
            local(t).start()
            for j in range(3):
                ici(t, j, me).start()

    def finish(in_refs, out_refs, sem_refs):
        me, peers, local, ici = copies(in_refs, out_refs, sem_refs)
        for t in range(n):
            for j in range(3):
                ici(t, j, peers[j][2]).wait_recv()
        for t in range(n):
            for j in range(3):
                ici(t, j, me).wait_send()
            local(t).wait()

    return _Comm([it[0] for it in items] + list(bufs), [jax.ShapeDtypeStruct(b.shape, b.dtype) for b in bufs],
                 {n + k: k for k in range(len(bufs))}, sems, start, finish)


def _run_comm(comm, name):
    ni, no = len(comm.ins), len(comm.outs)

    def body(*refs):
        comm.start(refs[:ni], refs[ni:ni + no], refs[ni + no:])
        comm.finish(refs[:ni], refs[ni:ni + no], refs[ni + no:])

    hbm = pl.BlockSpec(memory_space=pl.ANY)
    return pl.pallas_call(body, in_specs=[hbm] * ni, out_specs=[hbm] * no, out_shape=list(comm.outs),
                          scratch_shapes=list(comm.sems), input_output_aliases=dict(comm.aliases), name=name)(*comm.ins)


def _swap_cores(arrs, name):
    n = len(arrs)

    def body(*refs):
        src_refs, out_refs = refs[:n], refs[n:2 * n]
        send_sems, recv_sems = refs[2 * n:]
        x, y, c = lax.axis_index("x"), lax.axis_index("y"), lax.axis_index("c")
        cps = []
        for t in range(n):
            cp = pltpu.make_async_remote_copy(src_ref=src_refs[t], dst_ref=out_refs[t], send_sem=send_sems.at[t],
                                              recv_sem=recv_sems.at[t], device_id=(x, y, 1 - c), device_id_type=MESH)
            cp.start()
            cps.append(cp)
        for cp in cps:
            cp.wait()

    hbm = pl.BlockSpec(memory_space=pl.ANY)
    return pl.pallas_call(
        body, in_specs=[hbm] * n, out_specs=[hbm] * n, out_shape=[jax.ShapeDtypeStruct(a.shape, a.dtype) for a in arrs],
        scratch_shapes=[pltpu.SemaphoreType.DMA((n,)), pltpu.SemaphoreType.DMA((n,))],
        name=name)(*arrs)


def _allreduce_small(buf, name):
    R, C = buf.shape

    def body(b_ref, o_ref, recv, send_sems, recv_sems):
        x, y, c = lax.axis_index("x"), lax.axis_index("y"), lax.axis_index("c")
        me = 4 * x + 2 * y + c
        recv[me] = b_ref[...]
        cps = []
        for mask in range(1, N_DEV):
            px, py, pc = x ^ (mask >> 2 & 1), y ^ (mask >> 1 & 1), c ^ (mask & 1)
            cp = pltpu.make_async_remote_copy(src_ref=b_ref, dst_ref=recv.at[me], send_sem=send_sems.at[mask - 1],
                                              recv_sem=recv_sems.at[mask - 1], device_id=(px, py, pc), device_id_type=MESH)
            cp.start()
            cps.append((cp, 4 * px + 2 * py + pc, (px, py, pc), mask))
        for cp, pi, dev, mask in cps:
            pltpu.make_async_remote_copy(src_ref=b_ref, dst_ref=recv.at[pi], send_sem=send_sems.at[mask - 1],
                                         recv_sem=recv_sems.at[mask - 1], device_id=dev, device_id_type=MESH).wait_recv()
        for cp, _, _, _ in cps:
            cp.wait_send()
        acc = recv[0]
        for i in range(1, N_DEV):
            acc = acc + recv[i]
        o_ref[...] = acc

    vm = pl.BlockSpec(memory_space=pltpu.VMEM)
    return pl.pallas_call(
        body, in_specs=[vm], out_specs=vm, out_shape=jax.ShapeDtypeStruct((R, C), F32),
        scratch_shapes=[pltpu.VMEM((N_DEV, R, C), F32), pltpu.SemaphoreType.DMA((N_DEV - 1,)),
                        pltpu.SemaphoreType.DMA((N_DEV - 1,))],
        compiler_params=pltpu.CompilerParams(vmem_limit_bytes=VMEM_LIMIT), name=name)(buf)


def _pack(arrs):
    parts, offs, n = [], [], 0
    for a in arrs:
        f = a.reshape(-1).astype(F32)
        k = -(-f.shape[0] // LANES) * LANES
        parts.append(jnp.pad(f, (0, k - f.shape[0])))
        offs.append(n)
        n += k
    total = -(-n // (8 * LANES)) * (8 * LANES)
    if total > n:
        parts.append(jnp.zeros((total - n,), F32))
    return jnp.concatenate(parts).reshape(-1, LANES), offs


def _unpack(buf, offs, shapes):
    flat = buf.reshape(-1)
    return [flat[o:o + int(np.prod(s))].reshape(s) for o, s in zip(offs, shapes)]


def _ride(call, riders, name, made=None):
    rider = riders.get(name)
    if rider is None:
        return call(name, None)
    build, done = rider if isinstance(rider, tuple) else rider(made)
    out, res = call(name, build())
    done(res)
    return out


def _ffn_fwd(xm, g, w_in, cw, cb, w_out, t, riders):
    h2 = _rms_fwd(xm, g, BF16, f"ffn_norm{t}")
    u = _ride(lambda n, c: _mm_nn(h2, w_in, F32, n, out_panels=2, comm=c), riders, f"ffn_in{t}")
    act = _glu_fwd(u, cw, cb, f"glu{t}")
    w_out = w_out()
    xo = _ride(lambda n, c: _mm_nn(act, w_out, F32, n, res=xm, comm=c), riders, f"ffn_out{t}")
    return xo, (h2, u, act)


def _ffn_bwd(dxo, xm, g, w_in, cw, cb, w_out, saved, t, riders, split_dwin=False):
    h2, u, act = saved
    w_out = w_out()
    made = {}
    made['ffn_w_out'] =_ride(lambda n, c: _mm_tn(act, dxo, BF16, n, comm=c), riders, f"ffn_dwout{t}", made)
    dact = _ride(lambda n, c: _mm_nt(dxo, w_out, F32, n, comm=c), riders, f"ffn_dact{t}", made)
    du, dcw, dcb = _ride(lambda n, c: _glu_bwd(u, cw, cb, dact, n, comm=c), riders, f"glu_bwd{t}", made)
    dh2 = _ride(lambda n, c: _mm_nt(du, w_in, F32, n, comm=c), riders, f"ffn_dh{t}", made)
    if split_dwin:
        half = h2.shape[1] // 2
        made['ffn_w_in_a'] = _ride(lambda n, c: _mm_tn(h2[:, :half], du, BF16, n, out_panels=N_CHIPS, comm=c),
                                   riders, f"ffn_dwin{t}a", made)
        dw_in = (made['ffn_w_in_a'], _ride(lambda n, c: _mm_tn(h2[:, half:], du, BF16, n, out_panels=N_CHIPS, comm=c),
                                           riders, f"ffn_dwin{t}b", made))
    else:
        dw_in = _ride(lambda n, c: _mm_tn(h2, du, BF16, n, out_panels=N_CHIPS, comm=c), riders, f"ffn_dwin{t}", made)
    dxm, dg = _rms_bwd(xm, g, dh2, f"ffn_norm_bwd{t}", res=dxo)
    return dxm, dict(ffn_w_in=dw_in, ffn_conv_w=dcw, ffn_conv_b=dcb, ffn_w_out=made['ffn_w_out'], norm_ffn_g=dg)


def _pool_fwd(x, g, wg, scale, t, riders):
    h = _rms_fwd(x, g, F32, f"pool_norm{t}")
    pooled = _pool_win(h, True, BF16, f"pool_win{t}")
    yraw = _group_mm("nn", pooled, wg, F32, f"pool_mm{t}")
    xm = _scale_res(x, yraw, scale, f"pool_out{t}")
    return xm, (pooled, yraw)


def _pool_bwd(dxm, x, g, wg, scale, saved, t, riders):
    pooled, yraw = saved
    dyraw, dscale = _scale_bwd(dxm, yraw, scale, f"pool_out_bwd{t}")
    dpool = _group_mm("nt", dyraw, wg, F32, f"pool_dp{t}")
    dw = _group_mm("tn", pooled, dyraw, F32, f"pool_dw{t}").astype(BF16)
    dh = _pool_win(dpool, False, F32, f"pool_win_bwd{t}")
    dx, dgn = _rms_bwd(x, g, dh, f"pool_norm_bwd{t}", res=dxm)
    return dx, dict(pool_w=dw, pool_scale=dscale, norm_mix_g=dgn)


def _swa_fwd(x, g, w_qkv, q_gain, k_gain, sinks, w_o, pos, t, riders):
    S = x.shape[0]
    Hq, Hkv, hd = SWA_HEADS, SWA_KV_HEADS, SWA_HEAD_DIM
    nq, nkv = Hq * hd, Hkv * hd
    posq, posk, slopes = pos
    h = _rms_fwd(x, g, BF16, f"swa_norm{t}")
    qkv = _mm_nn(h, w_qkv, F32, f"swa_qkv{t}")
    q = qkv[:, :nq].reshape(S * Hq, hd)
    k = qkv[:, nq:nq + nkv].reshape(S * Hkv, hd)
    qn = _rms_fwd(q, q_gain, BF16, f"swa_qnorm{t}", scale=hd ** -0.5).reshape(S, nq)
    kT = _rms_fwd(k, k_gain, BF16, f"swa_knorm{t}").reshape(S, Hkv, hd).transpose(1, 0, 2)
    vT = qkv[:, nq + nkv:].astype(BF16).reshape(S, Hkv, hd).transpose(1, 0, 2)
    o, lse = _ride(lambda n, c: _swa_fwd_attn(qn, kT, vT, posq, posk, slopes, sinks.reshape(-1), n, comm=c), riders,
                   f"swa_attn{t}")
    xm = _mm_nn(o, w_o, F32, f"swa_o{t}", res=x)
    return xm, (h, q, k, qn, kT, vT, o, lse)


def _swa_bwd(dxm, x, g, w_qkv, q_gain, k_gain, sinks, w_o, pos, saved, t, riders):
    S = x.shape[0]
    Hq, Hkv, hd = SWA_HEADS, SWA_KV_HEADS, SWA_HEAD_DIM
    nq, nkv = Hq * hd, Hkv * hd
    posq, posk, slopes = pos
    h, q, k, qn, kT, vT, o, lse = saved
    do = _mm_nt(dxm, w_o, F32, f"swa_do{t}")
    dw_o = _mm_tn(o, dxm, BF16, f"swa_dwo{t}")
    dqn, dkp, dvp, dsink = _ride(
        lambda n, c: _swa_bwd_attn(qn, kT, vT, posq, posk, slopes, sinks.reshape(-1), o, lse, do, n, comm=c), riders,
        f"swa_attn_bwd{t}")
    dkn = _overlap3(dkp, f"swa_dk{t}").transpose(1, 0, 2).reshape(S * Hkv, hd)
    dv = _overlap3(dvp, f"swa_dv{t}").transpose(1, 0, 2).reshape(S, nkv)
    dq, dqg = _rms_bwd(q, q_gain, dqn.reshape(S * Hq, hd), f"swa_qnorm_bwd{t}", scale=hd ** -0.5)
    dk, dkg = _rms_bwd(k, k_gain, dkn, f"swa_knorm_bwd{t}")
    dqkv = jnp.concatenate([dq.reshape(S, nq), dk.reshape(S, nkv), dv], axis=1)
    dh = _mm_nt(dqkv, w_qkv, F32, f"swa_dh{t}")
    dw_qkv = _mm_tn(h, dqkv, BF16, f"swa_dwqkv{t}", out_panels=N_CHIPS)
    dx, dgn = _rms_bwd(x, g, dh, f"swa_norm_bwd{t}", res=dxm)
    return dx, dict(swa_w_qkv=dw_qkv, swa_q_gain=dqg, swa_k_gain=dkg, swa_sinks=dsink.reshape(1, Hq), swa_w_o=dw_o,
                    norm_mix_g=dgn)


def _mla_fwd(x, g, w_down, q_a_gain, kv_a_gain, w_uq, w_ukv, qn_gain, qr_gain, kn_gain, kr_gain, w_o, rope, t, riders):
    S = x.shape[0]
    H, dn, R, dv, qr_, kvr = MLA_HEADS, MLA_NOPE, MLA_ROPE, MLA_V, MLA_Q_RANK, MLA_KV_RANK
    sc = (dn + R) ** -0.5
    h = _rms_fwd(x, g, BF16, f"mla_norm{t}")
    d = _mm_nn(h, w_down, F32, f"mla_down{t}")
    cq_pre, ckv_pre, kpe_pre = d[:, :qr_], d[:, qr_:qr_ + kvr], d[:, qr_ + kvr:]
    cq = _rms_fwd(cq_pre, q_a_gain, BF16, f"mla_cq{t}")
    ckv = _rms_fwd(ckv_pre, kv_a_gain, BF16, f"mla_ckv{t}")
    q3 = _mm_nn(cq, w_uq, F32, f"mla_uq{t}").reshape(S, H, dn + R).transpose(1, 0, 2)
    kv3 = _mm_nn(ckv, w_ukv, F32, f"mla_ukv{t}").reshape(S, H, dn + dv).transpose(1, 0, 2)
    qn_in, qp_in = q3[..., :dn].reshape(H * S, dn), q3[..., dn:].reshape(H * S, R)
    kn_in, v = kv3[..., :dn].reshape(H * S, dn), kv3[..., dn:].astype(BF16)
    qn = _rms_fwd(qn_in, qn_gain, BF16, f"mla_qn{t}", scale=sc).reshape(H, S, dn)
    qp = _rms_fwd(qp_in, qr_gain, BF16, f"mla_qp{t}", scale=sc, rope=rope).reshape(H, S, R)
    kn = _rms_fwd(kn_in, kn_gain, BF16, f"mla_kn{t}").reshape(H, S, dn)
    kp = _rms_fwd(kpe_pre, kr_gain, BF16, f"mla_kp{t}", rope=rope)
    qf = jnp.concatenate([qn, qp], axis=-1)
    kf = jnp.concatenate([kn, jnp.broadcast_to(kp[None], (H, S, R))], axis=-1)
    vT = v.transpose(0, 2, 1)
    oT, lse = _ride(lambda n, c: _mla_fwd_attn(qf, kf, vT, n, comm=c), riders, f"mla_attn{t}")
    oT2 = oT.reshape(H * dv, S)
    xm = _mm_tn(oT2, w_o, F32, f"mla_o{t}", res=x)
    return xm, (h, cq_pre, ckv_pre, kpe_pre, cq, ckv, qn_in, qp_in, kn_in, v, qf, kf, oT, lse)


def _mla_bwd(dxm, x, g, w_down, q_a_gain, kv_a_gain, w_uq, w_ukv, qn_gain, qr_gain, kn_gain, kr_gain, w_o, rope,
             saved, t, riders):
    S = x.shape[0]
    H, dn, R, dv = MLA_HEADS, MLA_NOPE, MLA_ROPE, MLA_V
    sc = (dn + R) ** -0.5
    h, cq_pre, ckv_pre, kpe_pre, cq, ckv, qn_in, qp_in, kn_in, v, qf, kf, oT, lse = saved
    oT2 = oT.reshape(H * dv, S)
    doT = _mm_nt(w_o, dxm, F32, f"mla_do{t}").reshape(H, dv, S)
    dw_o = _mm_nn(oT2, dxm, BF16, f"mla_dwo{t}")
    dqT, dkf, dvv = _mla_bwd_attn(qf, kf, kf.transpose(0, 2, 1), v, oT, doT, lse, f"mla_attn_bwd{t}")
    dqf = dqT.transpose(0, 2, 1)
    dqn, dqp, dkn = dqf[..., :dn], dqf[..., dn:], dkf[..., :dn]
    dkp = _sum_heads(dkf[..., dn:], f"mla_dkp{t}")
    dqn_in, dqng = _rms_bwd(qn_in, qn_gain, dqn.reshape(H * S, dn), f"mla_qn_bwd{t}", scale=sc)
    dqp_in, dqrg = _rms_bwd(qp_in, qr_gain, dqp.reshape(H * S, R), f"mla_qp_bwd{t}", scale=sc, rope=rope)
    dkn_in, dkng = _rms_bwd(kn_in, kn_gain, dkn.reshape(H * S, dn), f"mla_kn_bwd{t}")
    dkpe_pre, dkrg = _rms_bwd(kpe_pre, kr_gain, dkp, f"mla_kp_bwd{t}", rope=rope)
    dq = jnp.concatenate([dqn_in.reshape(H, S, dn), dqp_in.reshape(H, S, R)], axis=-1).transpose(1, 0, 2).reshape(S, -1)
    dkv = jnp.concatenate([dkn_in.reshape(H, S, dn), dvv], axis=-1).transpose(1, 0, 2).reshape(S, -1)
    dcq = _mm_nt(dq, w_uq, F32, f"mla_dcq{t}")
    dw_uq = _mm_tn(cq, dq, BF16, f"mla_dwuq{t}", out_panels=N_CHIPS)
    dckv = _mm_nt(dkv, w_ukv, F32, f"mla_dckv{t}")
    dw_ukv = _mm_tn(ckv, dkv, BF16, f"mla_dwukv{t}", out_panels=N_CHIPS)
    dcq_pre, dqag = _rms_bwd(cq_pre, q_a_gain, dcq, f"mla_cq_bwd{t}")
    dckv_pre, dkvag = _rms_bwd(ckv_pre, kv_a_gain, dckv, f"mla_ckv_bwd{t}")
    dd = jnp.concatenate([dcq_pre, dckv_pre, dkpe_pre], axis=1)
    dh = _mm_nt(dd, w_down, F32, f"mla_dh{t}")
    dw_down = _mm_tn(h, dd, BF16, f"mla_dwdown{t}")
    dx, dgn = _rms_bwd(x, g, dh, f"mla_norm_bwd{t}", res=dxm)
    return dx, dict(mla_w_down=dw_down, mla_q_a_gain=dqag, mla_kv_a_gain=dkvag, mla_w_uq=dw_uq, mla_w_ukv=dw_ukv,
                    mla_qn_gain=dqng, mla_qr_gain=dqrg, mla_kn_gain=dkng, mla_kr_gain=dkrg, mla_w_o=dw_o, norm_mix_g=dgn)


def _chips_to_axis(gathered, axis):
    moved = jnp.moveaxis(gathered, 0, axis)
    shape = list(moved.shape)
    shape[axis:axis + 2] = [shape[axis] * shape[axis + 1]]
    return moved.reshape(shape)


def kernel(x, positions, norm_mix_g, norm_ffn_g, pool_w, pool_scale, swa_w_qkv, swa_q_gain, swa_k_gain, swa_sinks, swa_w_o, mla_w_down, mla_q_a_gain, mla_kv_a_gain, mla_w_uq, mla_w_ukv, mla_qn_gain, mla_qr_gain, mla_kn_gain, mla_kr_gain, mla_w_o, ffn_w_in, ffn_conv_w, ffn_conv_b, ffn_w_out, loss_target, m_norm_mix_g, m_norm_ffn_g, m_pool_w, m_pool_scale, m_swa_w_qkv, m_swa_q_gain, m_swa_k_gain, m_swa_sinks, m_swa_w_o, m_mla_w_down, m_mla_q_a_gain, m_mla_kv_a_gain, m_mla_w_uq, m_mla_w_ukv, m_mla_qn_gain, m_mla_qr_gain, m_mla_kn_gain, m_mla_kr_gain, m_mla_w_o, m_ffn_w_in, m_ffn_conv_w, m_ffn_conv_b, m_ffn_w_out, v_norm_mix_g, v_norm_ffn_g, v_pool_w, v_pool_scale, v_swa_w_qkv, v_swa_q_gain, v_swa_k_gain, v_swa_sinks, v_swa_w_o, v_mla_w_down, v_mla_q_a_gain, v_mla_kv_a_gain, v_mla_w_uq, v_mla_w_ukv, v_mla_qn_gain, v_mla_qr_gain, v_mla_kn_gain, v_mla_kr_gain, v_mla_w_o, v_ffn_w_in, v_ffn_conv_w, v_ffn_conv_b, v_ffn_w_out):
    args = dict(locals())
    W = {n: args[n] for n in WEIGHTS}
    M = {n: args["m_" + n] for n in WEIGHTS}
    V = {n: args["v_" + n] for n in WEIGHTS}
    xs = x[0]
    S, D = xs.shape
    chip = 2 * lax.axis_index("x") + lax.axis_index("y")

    big = ['pool_w', 'swa_w_qkv', 'swa_w_o', 'mla_w_down', 'mla_w_uq', 'mla_w_ukv', 'mla_w_o', 'ffn_w_in', 'ffn_w_out']
    small_sharded = [n for n in WEIGHTS if SMALL.get(n) is not None]
    mixer_w = {0: ['pool_w'], 1: ['swa_w_qkv', 'swa_w_o'], 2: ['mla_w_down', 'mla_w_uq', 'mla_w_ukv', 'mla_w_o']}
    Wb = {n: W[n].astype(BF16) for n in big}
    Wg = {}

    def mixer_keys(i):
        return [(n, i // 3) for n in mixer_w[i % 3]]

    def gather_rider(keys):
        return (lambda: _gather_comm([(Wb[n], l) for n, l in keys])), (lambda res: Wg.update(zip(keys, res)))

    assert DEPTH == 4
    keys0 = mixer_keys(0) + [('ffn_w_in', 0)]
    riders = {
        'ffn_in0': gather_rider([('ffn_w_out', 0)] + mixer_keys(1)),
        'ffn_out0': gather_rider([('ffn_w_out', 1)]),
        'swa_attn1': gather_rider([('ffn_w_in', 1)]),
        'ffn_in1': gather_rider(mixer_keys(2)),
        'ffn_out1': gather_rider([('ffn_w_out', 2)]),
        'mla_attn2': gather_rider([('ffn_w_in', 2), ('ffn_w_in', 3), ('ffn_w_out', 3)] + mixer_keys(3)),
    }
    first = _run_comm(_gather_comm([(Wb[n], l) for n, l in keys0] + [(W[n], None) for n in small_sharded]), "gather_first")
    Wg.update(zip(keys0, first))
    full = dict(W)
    for n, r in zip(small_sharded, first[len(keys0):]):
        full[n] = _chips_to_axis(r, SMALL[n])
    rows = lambda a: a.reshape((a.shape[0] * a.shape[1],) + a.shape[2:])

    posf = positions.astype(F32)
    slopes = jnp.asarray(2.0 ** (-8.0 * np.arange(1, SWA_HEADS + 1) / SWA_HEADS), dtype=F32)
    pos = (posf.reshape(S, 1), posf.reshape(1, S), slopes)
    inv = ROPE_THETA ** (-jnp.arange(0, MLA_ROPE, 2, dtype=F32) / MLA_ROPE)
    ang = posf[:, None] * inv[None, :]
    cos, sin = jnp.cos(ang), jnp.sin(ang)
    rope = (jnp.concatenate([cos, cos], axis=1), jnp.concatenate([-sin, sin], axis=1))

    margs_cache = {}

    def mixer_args(i):
        if i in margs_cache:
            return margs_cache[i]
        kind, j = i % 3, i // 3
        g = full['norm_mix_g'][i:i + 1]
        if kind == 0:
            byg = jnp.swapaxes(Wg[('pool_w', j)], 0, 1)
            a = (g, byg.reshape(byg.shape[0], -1, byg.shape[3]), full['pool_scale'][j:j + 1])
        elif kind == 1:
            a = (g, Wg[('swa_w_qkv', j)], full['swa_q_gain'][j:j + 1], full['swa_k_gain'][j:j + 1],
                 full['swa_sinks'][j:j + 1], rows(Wg[('swa_w_o', j)]), pos)
        else:
            a = (g, rows(Wg[('mla_w_down', j)]), full['mla_q_a_gain'][j:j + 1], full['mla_kv_a_gain'][j:j + 1],
                 Wg[('mla_w_uq', j)], Wg[('mla_w_ukv', j)], full['mla_qn_gain'][j:j + 1], full['mla_qr_gain'][j:j + 1],
                 full['mla_kn_gain'][j:j + 1], full['mla_kr_gain'][j:j + 1], rows(Wg[('mla_w_o', j)]), rope)
        margs_cache[i] = (kind, a)
        return kind, a

    def ffn_args(i):
        return (full['norm_ffn_g'][i:i + 1], Wg[('ffn_w_in', i)], full['ffn_conv_w'][i], full['ffn_conv_b'][i:i + 1],
                lambda: rows(Wg[('ffn_w_out', i)]))

    fwd = (_pool_fwd, _swa_fwd, _mla_fwd)
    bwd = (_pool_bwd, _swa_bwd, _mla_bwd)
    tape = []
    cur = xs
    for i in range(DEPTH):
        kind, margs = mixer_args(i)
        xm, msaved = fwd[kind](cur, *margs, i, riders)
        xo, fsaved = _ffn_fwd(xm, *ffn_args(i), i, riders)
        tape.append((cur, xm, msaved, fsaved))
        cur = xo
    sq, dcur = _loss_head(cur, loss_target[0], "loss_head")
    loss = lax.psum(0.5 / D * jnp.sum(sq), ("x", "y", "c"))

    def by_chip(n, gl):
        if n == 'pool_w':
            G, dg_ = gl.shape[0], gl.shape[1]
            return gl.reshape(G, N_CHIPS, dg_ // N_CHIPS, dg_).swapaxes(0, 1)
        if len(gl.shape) == 3:
            return gl
        return gl.reshape((N_CHIPS, gl.shape[0] // N_CHIPS) + gl.shape[1:])

    bufs = {n: lax.empty((N_CHIPS,) + tuple(W[n].shape), BF16) for n in big}

    def scatter_rider(group):
        def make(made):
            grp = group(made) if callable(group) else group
            names = list(dict.fromkeys(it[0] for it in grp))
            build = lambda: _scatter_comm([(gl, names.index(n), l, r, cut) for n, l, gl, r, cut in grp],
                                          [bufs[n] for n in names])
            return build, (lambda res: bufs.update(zip(names, res)))
        return make

    def own_w_out(i):
        return scatter_rider(lambda made: [('ffn_w_out', i, by_chip('ffn_w_out', made['ffn_w_out']), None, True)])

    grads = {n: [None] * W[n].shape[0] for n in WEIGHTS}
    half = W['ffn_w_in'].shape[1] // 2
    quarter = half // 2
    riders = {f'ffn_dwin{DEPTH - 1}': own_w_out(DEPTH - 1)}
    for i in reversed(range(DEPTH)):
        kind, margs = mixer_args(i)
        x_in, xm, msaved, fsaved = tape[i]
        dxm, fg = _ffn_bwd(dcur, xm, *ffn_args(i), fsaved, i, riders, split_dwin=(i == 0))
        dcur, mg = bwd[kind](dxm, x_in, *margs, msaved, i, riders)
        for n, gval in fg.items():
            grads[n][i] = gval
        for n, gval in mg.items():
            grads[n][i if n == 'norm_mix_g' else i // 3] = gval
        if i > 0:
            g_in = by_chip('ffn_w_in', fg['ffn_w_in'])
            piece = lambda q: ('ffn_w_in', i, g_in, (q * quarter, quarter), True)
            mixer_g = [(n, i // 3, by_chip(n, mg[n]), None, True) for n in mixer_w[kind]]
            on_attn = (i - 1) % 3 == 1
            riders = {f'ffn_dwout{i - 1}': scatter_rider([piece(0)]),
                      f'ffn_dact{i - 1}': scatter_rider([piece(1)]),
                      f'glu_bwd{i - 1}': scatter_rider([piece(2)]),
                      f'ffn_dh{i - 1}': scatter_rider([piece(3)] + ([] if on_attn else mixer_g)),
                      f'ffn_dwin{i - 1}' + ('a' if i == 1 else ''): own_w_out(i - 1)}
            if on_attn:
                riders[f'swa_attn_bwd{i - 1}'] = scatter_rider(mixer_g)
            if i == 1:
                riders['ffn_dwin0b'] = scatter_rider(
                    lambda made: [('ffn_w_in', 0, by_chip('ffn_w_in', made['ffn_w_in_a']), (0, half), False)])
    grad_x = dcur[None]
    last = [('ffn_w_in', 0, by_chip('ffn_w_in', grads['ffn_w_in'][0][1]), (half, half), False)]
    last += [(n, 0, by_chip(n, grads[n][0]), None, True) for n in mixer_w[0]]
    build, done = scatter_rider(last)(None)
    done(_run_comm(build(), "scatter_last"))

    recvs = [bufs[n] for n in big]
    partial = []
    for n, r in zip(big, recvs):
        C = r.shape[-1]
        partial.append(_sum_chips(r.reshape(N_CHIPS, -1, C), f"sum_{n}"))
    other = _swap_cores(partial, "swap_cores")
    out = {}
    for n, p, q in zip(big, partial, other):
        C = p.shape[1]
        res = _adamw(W[n].reshape(-1, C), M[n].reshape(-1, C), V[n].reshape(-1, C), [p, q], f"adamw_{n}")
        out[n] = [r.reshape(W[n].shape) for r in res]

    small = [n for n in WEIGHTS if n in SMALL]
    full_shapes = [tuple(full[n].shape) for n in small]
    sg = []
    for n in small:
        parts = grads[n]
        if n == 'ffn_conv_w':
            sg.append(jnp.stack(parts))
        else:
            sg.append(jnp.concatenate(parts, axis=0))
    buf, offs = _pack(sg)
    summed = _unpack(_allreduce_small(buf, "allreduce_small"), offs, full_shapes)
    gsm = []
    for n, gfull in zip(small, summed):
        ax = SMALL[n]
        if ax is not None:
            size = W[n].shape[ax]
            gfull = lax.dynamic_slice_in_dim(gfull, chip * size, size, axis=ax)
        gsm.append(gfull)
    gb, goffs = _pack(gsm)
    wb, _ = _pack([W[n] for n in small])
    mb, _ = _pack([M[n] for n in small])
    vb, _ = _pack([V[n] for n in small])
    res = _adamw(wb, mb, vb, [gb], "adamw_small")
    shapes = [tuple(W[n].shape) for n in small]
    unp = [_unpack(r, goffs, shapes) for r in res]
    for k, n in enumerate(small):
        out[n] = [u[k] for u in unp]

    return (loss, grad_x, *[out[n][0] for n in WEIGHTS], *[out[n][1] for n in WEIGHTS],
            *[out[n][2] for n in WEIGHTS], *[out[n][3] for n in WEIGHTS])
```

```python
import functools
import math

import numpy as np
import jax
import jax.numpy as jnp
from jax import lax
from jax.experimental import pallas as pl
from jax.experimental.pallas import tpu as pltpu

F32 = jnp.float32
BF16 = jnp.bfloat16

D_MODEL = 2048
SEQ = 4096
DEPTH = 4
EPS = 1e-6
POOL_WINDOWS = (2, 4, 8, 16)
SWA_HEADS = 32
SWA_KV_HEADS = 4
SWA_HEAD_DIM = 64
SWA_WINDOW = 128
MLA_HEADS = 16
MLA_NOPE = 128
MLA_ROPE = 64
MLA_V = 128
MLA_Q_RANK = 512
MLA_KV_RANK = 512
ROPE_THETA = 10000.0
D_FF = 5632
ADAM_LR = 0.001
ADAM_B1 = 0.9
ADAM_B2 = 0.999
ADAM_EPS = 1e-08
ADAM_WD = 0.01
ADAM_STEP = 10

N_CHIPS = 4
N_DEV = 8
MESH = pl.DeviceIdType.MESH
VMEM_LIMIT = 48 << 20
LANES = 128
NEG = -1e30

WEIGHTS = ['norm_mix_g', 'norm_ffn_g', 'pool_w', 'pool_scale', 'swa_w_qkv', 'swa_q_gain', 'swa_k_gain', 'swa_sinks',
           'swa_w_o', 'mla_w_down', 'mla_q_a_gain', 'mla_kv_a_gain', 'mla_w_uq', 'mla_w_ukv', 'mla_qn_gain',
           'mla_qr_gain', 'mla_kn_gain', 'mla_kr_gain', 'mla_w_o', 'ffn_w_in', 'ffn_conv_w', 'ffn_conv_b', 'ffn_w_out']
SMALL = {'norm_mix_g': None, 'norm_ffn_g': None, 'pool_scale': 1, 'swa_q_gain': None, 'swa_k_gain': None,
         'swa_sinks': None, 'mla_q_a_gain': 1, 'mla_kv_a_gain': 1, 'mla_qn_gain': None, 'mla_qr_gain': None,
         'mla_kn_gain': None, 'mla_kr_gain': None, 'ffn_conv_w': 2, 'ffn_conv_b': None}


def _cparams(sem):
    return pltpu.CompilerParams(dimension_semantics=sem, vmem_limit_bytes=VMEM_LIMIT)


def _tile(n, cands):
    for c in cands:
        if c <= n and n % c == 0:
            return c
    return n


WIDE = (1408, 1024, 768, 640, 512, 384, 256, 128)


def _dims(arr):
    if len(arr.shape) == 2:
        return arr.shape[0], arr.shape[1], arr.shape[1]
    return arr.shape[1], arr.shape[0] * arr.shape[2], arr.shape[2]


def _pspec(arr, tr, tc, fn):
    if len(arr.shape) == 2:
        return pl.BlockSpec((tr, tc), fn)
    per = arr.shape[2] // tc

    def im(*g):
        r, c = fn(*g)
        return (c // per, r, c % per)

    return pl.BlockSpec((None, tr, tc), im)


class _Comm:
    def __init__(self, ins, outs, aliases, sems, start, finish):
        self.ins, self.outs, self.aliases, self.sems, self.start, self.finish = ins, outs, aliases, sems, start, finish


def _call(body, grid, sem, in_specs, out_specs, out_shape, scratch, args, name, comm=None):
    if comm is None:
        return list(pl.pallas_call(body, grid=grid, in_specs=in_specs, out_specs=out_specs, out_shape=out_shape,
                                   scratch_shapes=scratch, compiler_params=_cparams(sem), name=name)(*args))
    ni, no, ns, nci, nco = len(args), len(out_shape), len(scratch), len(comm.ins), len(comm.outs)

    def wrapped(*refs):
        core_in, c_in = refs[:ni], refs[ni:ni + nci]
        core_out, c_out = refs[ni + nci:ni + nci + no], refs[ni + nci + no:ni + nci + no + nco]
        rest = refs[ni + nci + no + nco:]
        ids = [pl.program_id(d) for d in range(len(grid))]
        first = functools.reduce(lambda p, q: p & q, [i == 0 for i in ids])
        last = functools.reduce(lambda p, q: p & q, [i == n - 1 for i, n in zip(ids, grid)])

        @pl.when(first)
        def _():
            comm.start(c_in, c_out, rest[ns:])

        body(*core_in, *core_out, *rest[:ns])

        @pl.when(last)
        def _():
            comm.finish(c_in, c_out, rest[ns:])

    hbm = pl.BlockSpec(memory_space=pl.ANY)
    res = pl.pallas_call(
        wrapped, grid=grid, in_specs=list(in_specs) + [hbm] * nci, out_specs=list(out_specs) + [hbm] * nco,
        out_shape=list(out_shape) + list(comm.outs), scratch_shapes=list(scratch) + list(comm.sems),
        input_output_aliases={ni + i: no + o for i, o in comm.aliases.items()},
        compiler_params=_cparams(("arbitrary",) * len(grid)), name=name)(*args, *comm.ins)
    return list(res[:no]), list(res[no:])


MM_VMEM_BUDGET = 40 << 20
_CONTRACT = {"nn": (((1,), (0,)), ((), ())), "nt": (((1,), (1,)), ((), ())), "tn": (((0,), (0,)), ((), ()))}


def _mm(kind, a, b, out_dtype, name, res=None, out_panels=1, comm=None):
    ar, ac, aw = _dims(a)
    br, bc, bw = _dims(b)
    if kind == "nn":
        M, K, N, mw, kw, nw = ar, ac, bc, ar, aw, bw
    elif kind == "nt":
        M, K, N, mw, kw, nw = ar, ac, br, ar, math.gcd(aw, bw), br
    else:
        M, K, N, mw, kw, nw = ac, ar, bc, aw, ar, bw
    assert K == (br if kind != "nt" else bc)
    no = N // out_panels
    tm = _tile(mw, (1024, 512, 256, 128))
    tn = _tile(math.gcd(nw, no), WIDE if kind != "nt" else (1024, 512, 256, 128))
    fixed = tm * tn * (2 * jnp.dtype(out_dtype).itemsize + 4 + (8 if res is not None else 0))
    per_k = 2 * (tm * a.dtype.itemsize + tn * b.dtype.itemsize)
    tk = next((c for c in (kw, 4096, 2816, 2048, 1408, 1024, 768, 640, 512, 384, 256, 128)
               if c <= kw and kw % c == 0 and fixed + c * per_k <= MM_VMEM_BUDGET), _tile(kw, (128,)))
    nk = K // tk
    o_sds = jax.ShapeDtypeStruct((M, N) if out_panels == 1 else (out_panels, M, no), out_dtype)

    def body(*refs):
        a_ref, b_ref = refs[:2]

        def out(r):
            if res is not None:
                r = r + refs[2][...]
            return r.astype(out_dtype)

        prod = lax.dot_general(a_ref[...].astype(BF16), b_ref[...].astype(BF16), _CONTRACT[kind], preferred_element_type=F32)
        if nk == 1:
            refs[-1][...] = out(prod)
            return
        o_ref, acc = refs[-2:]
        k = pl.program_id(2)

        @pl.when(k == 0)
        def _():
            acc[...] = prod

        @pl.when(k > 0)
        def _():
            acc[...] += prod

        @pl.when(k == nk - 1)
        def _():
            o_ref[...] = out(acc[...])

    if kind == "nn":
        specs = [_pspec(a, tm, tk, lambda i, j, k: (i, k)), _pspec(b, tk, tn, lambda i, j, k: (k, j))]
    elif kind == "nt":
        specs = [_pspec(a, tm, tk, lambda i, j, k: (i, k)), _pspec(b, tn, tk, lambda i, j, k: (j, k))]
    else:
        specs = [_pspec(a, tk, tm, lambda i, j, k: (k, i)), _pspec(b, tk, tn, lambda i, j, k: (k, j))]
    args = [a, b]
    if res is not None:
        specs.append(_pspec(res, tm, tn, lambda i, j, k: (i, j)))
        args.append(res)
    got = _call(body, (M // tm, N // tn, nk), ("parallel", "parallel", "arbitrary"), specs,
                [_pspec(o_sds, tm, tn, lambda i, j, k: (i, j))], [o_sds], [pltpu.VMEM((tm, tn), F32)] if nk > 1 else [],
                args, name, comm)
    return got[0] if comm is None else (got[0][0], got[1])


def _mm_nn(a, b, out_dtype, name, **kw):
    return _mm("nn", a, b, out_dtype, name, **kw)


def _mm_nt(a, b, out_dtype, name, **kw):
    return _mm("nt", a, b, out_dtype, name, **kw)


def _mm_tn(a, b, out_dtype, name, **kw):
    return _mm("tn", a, b, out_dtype, name, **kw)


def _swap_halves(y):
    h = y.shape[-1] // 2
    return jnp.concatenate([y[:, h:], y[:, :h]], axis=1)


def _row_tile(R, d, limit=None):
    cap = max(8, (1 << 19) // d)
    cands = [c for c in (4096, 2048, 1024, 512, 256, 128, 64, 32, 16, 8) if c <= cap]
    if limit is not None:
        cands = [c for c in cands if limit % c == 0]
    return _tile(R, cands)


def _rms_fwd(x, g, out_dtype, name, scale=1.0, rope=None):
    R, d = x.shape
    tr = _row_tile(R, d, None if rope is None else rope[0].shape[0])

    def body(*refs):
        x_ref, g_ref = refs[:2]
        o_ref = refs[-1]
        xv = x_ref[...].astype(F32)
        y = xv * lax.rsqrt(jnp.mean(xv * xv, axis=-1, keepdims=True) + EPS)
        y = y * g_ref[...]
        if rope is not None:
            y = y * refs[2][...] + _swap_halves(y) * refs[3][...]
        if scale != 1.0:
            y = y * scale
        o_ref[...] = y.astype(o_ref.dtype)

    in_specs = [pl.BlockSpec((tr, d), lambda i: (i, 0)), pl.BlockSpec((1, d), lambda i: (0, 0))]
    args = [x, g]
    if rope is not None:
        nrt = rope[0].shape[0] // tr
        in_specs += [pl.BlockSpec((tr, d), lambda i: (i % nrt, 0))] * 2
        args += list(rope)
    return pl.pallas_call(
        body, grid=(R // tr,), in_specs=in_specs, out_specs=pl.BlockSpec((tr, d), lambda i: (i, 0)),
        out_shape=jax.ShapeDtypeStruct((R, d), out_dtype), compiler_params=_cparams(("parallel",)), name=name)(*args)


def _rms_bwd(x, g, dy, name, scale=1.0, rope=None, res=None):
    R, d = x.shape
    tr = _row_tile(R, d, None if rope is None else rope[0].shape[0])
    panels = dy.shape[0] if len(dy.shape) == 3 else 0

    def body(*refs):
        x_ref, g_ref, dy_ref = refs[:3]
        dx_ref, dg_ref = refs[-2:]
        i = pl.program_id(0)
        xv = x_ref[...].astype(F32)
        r = lax.rsqrt(jnp.mean(xv * xv, axis=-1, keepdims=True) + EPS)
        xhat = xv * r
        dyv = (jnp.concatenate([dy_ref[p] for p in range(panels)], axis=1) if panels else dy_ref[...]).astype(F32)
        if scale != 1.0:
            dyv = dyv * scale
        if rope is not None:
            dyv = dyv * refs[3][...] + _swap_halves(dyv * refs[4][...])

        @pl.when(i == 0)
        def _():
            dg_ref[...] = jnp.zeros_like(dg_ref)

        dg_ref[...] += jnp.sum(dyv * xhat, axis=0, keepdims=True)
        dxh = dyv * g_ref[...]
        dx = r * (dxh - xhat * jnp.mean(dxh * xhat, axis=-1, keepdims=True))
        if res is not None:
            dx = dx + refs[-3][...]
        dx_ref[...] = dx

    row = pl.BlockSpec((tr, d), lambda i: (i, 0))
    vec = pl.BlockSpec((1, d), lambda i: (0, 0))
    in_specs = [row, vec, pl.BlockSpec((panels, tr, d // panels), lambda i: (0, i, 0)) if panels else row]
    args = [x, g, dy]
    if rope is not None:
        nrt = rope[0].shape[0] // tr
        in_specs += [pl.BlockSpec((tr, d), lambda i: (i % nrt, 0))] * 2
        args += list(rope)
    if res is not None:
        in_specs.append(row)
        args.append(res)
    return pl.pallas_call(
        body, grid=(R // tr,), in_specs=in_specs, out_specs=[row, vec],
        out_shape=[jax.ShapeDtypeStruct((R, d), F32), jax.ShapeDtypeStruct((1, d), F32)],
        compiler_params=_cparams(("arbitrary",)), name=name)(*args)


PAD = 16


def _pool_win(x, fwd, out_dtype, name):
    G = len(POOL_WINDOWS)
    if fwd:
        S, D = x.shape
        dg = D // G
    else:
        _, S, dg = x.shape
    tc = _tile(dg, (128,))
    nt = dg // tc
    rc = _tile(S, (512,))

    def body(*refs):
        scr = refs[-1]
        o_refs = [refs[-2].at[gi] for gi in range(G)]
        x_refs = refs[:G] if fwd else [refs[0].at[gi] for gi in range(G)]
        zeros = jnp.zeros((PAD, tc), F32)
        scr[pl.ds(0, PAD), :] = zeros
        scr[pl.ds(PAD + S, PAD), :] = zeros
        for gi, w in enumerate(POOL_WINDOWS):
            left, right = w // 2, w - 1 - w // 2

            def count(r0):
                t = r0 + lax.broadcasted_iota(jnp.int32, (rc, 1), 0)
                return (jnp.minimum(t + right + 1, S) - jnp.maximum(t - left, 0)).astype(F32)

            for r0 in range(0, S, rc):
                xv = x_refs[gi][pl.ds(r0, rc), :]
                scr[pl.ds(PAD + r0, rc), :] = xv if fwd else xv / count(r0)
            lo, hi = (left, right) if fwd else (right, left)
            for r0 in range(0, S, rc):
                acc = scr[pl.ds(PAD + r0 - lo, rc), :]
                for o in range(-lo + 1, hi + 1):
                    acc = acc + scr[pl.ds(PAD + r0 + o, rc), :]
                xv = x_refs[gi][pl.ds(r0, rc), :]
                out = acc / count(r0) - xv if fwd else acc - xv
                o_refs[gi][pl.ds(r0, rc), :] = out.astype(out_dtype)

    panel = pl.BlockSpec((G, S, tc), lambda j: (0, 0, j))
    if fwd:
        in_specs = [pl.BlockSpec((S, tc), functools.partial(lambda j, gi: (0, gi * nt + j), gi=gi)) for gi in range(G)]
    else:
        in_specs = [panel]
    return pl.pallas_call(
        body, grid=(nt,), in_specs=in_specs, out_specs=panel, out_shape=jax.ShapeDtypeStruct((G, S, dg), out_dtype),
        scratch_shapes=[pltpu.VMEM((S + 2 * PAD, tc), F32)],
        compiler_params=_cparams(("parallel",)), name=name)(*([x] * G if fwd else [x]))


def _group_mm(kind, a, b, out_dtype, name):
    G = len(POOL_WINDOWS)
    if kind == "nt":
        S, dg = a.shape[0], a.shape[1] // G
    else:
        _, S, dg = a.shape
    tm = _tile(S, (1024, 512, 256, 128))
    nm = S // tm
    pan = pl.BlockSpec((None, tm, dg), lambda g, i: (g, i, 0))
    col = pl.BlockSpec((tm, dg), lambda g, i: (i, g))
    sq = pl.BlockSpec((None, dg, dg), lambda g, i: (g, 0, 0))

    def body(a_ref, b_ref, o_ref):
        prod = lax.dot_general(a_ref[...].astype(BF16), b_ref[...].astype(BF16), _CONTRACT[kind], preferred_element_type=F32)
        if kind != "tn":
            o_ref[...] = prod.astype(out_dtype)
            return
        i = pl.program_id(1)

        @pl.when(i == 0)
        def _():
            o_ref[...] = prod

        @pl.when(i > 0)
        def _():
            o_ref[...] += prod

    specs, out_spec, shape = {"nn": ([pan, sq], col, (S, G * dg)), "nt": ([col, sq], pan, (G, S, dg)),
                              "tn": ([pan, col], sq, (G, dg, dg))}[kind]
    assert kind != "tn" or out_dtype == F32
    return pl.pallas_call(body, grid=(G, nm), in_specs=specs, out_specs=out_spec,
                          out_shape=jax.ShapeDtypeStruct(shape, out_dtype),
                          compiler_params=_cparams(("parallel", "arbitrary")), name=name)(a, b)


def _scale_res(x, y, scale, name):
    S, D = x.shape
    tr = _row_tile(S, D)
    row = pl.BlockSpec((tr, D), lambda i: (i, 0))

    def body(x_ref, y_ref, s_ref, o_ref):
        o_ref[...] = x_ref[...] + y_ref[...] * s_ref[...]

    return pl.pallas_call(body, grid=(S // tr,), in_specs=[row, row, pl.BlockSpec((1, D), lambda i: (0, 0))],
                          out_specs=row, out_shape=jax.ShapeDtypeStruct((S, D), F32),
                          compiler_params=_cparams(("parallel",)), name=name)(x, y, scale)


def _scale_bwd(dy, y, scale, name):
    S, D = dy.shape
    tr = _row_tile(S, D)
    row = pl.BlockSpec((tr, D), lambda i: (i, 0))
    vec = pl.BlockSpec((1, D), lambda i: (0, 0))

    def body(dy_ref, y_ref, s_ref, o_ref, ds_ref):
        @pl.when(pl.program_id(0) == 0)
        def _():
            ds_ref[...] = jnp.zeros_like(ds_ref)

        d = dy_ref[...]
        ds_ref[...] += jnp.sum(d * y_ref[...], axis=0, keepdims=True)
        o_ref[...] = (d * s_ref[...]).astype(BF16)

    return pl.pallas_call(body, grid=(S // tr,), in_specs=[row, row, vec], out_specs=[row, vec],
                          out_shape=[jax.ShapeDtypeStruct((S, D), BF16), jax.ShapeDtypeStruct((1, D), F32)],
                          compiler_params=_cparams(("arbitrary",)), name=name)(dy, y, scale)


GPAD = 8


def _sigmoid(z):
    return 1.0 / (1.0 + jnp.exp(-z))


def _glu_fwd(u, cw, cb, name):
    _, S, F = u.shape
    tc = _tile(F, (128,))
    rc = _tile(S, (512,))

    def body(u_ref, w_ref, b_ref, o_ref, scr):
        zeros = jnp.zeros((GPAD, tc), F32)
        scr[pl.ds(0, GPAD), :] = zeros
        scr[pl.ds(GPAD + S, GPAD), :] = zeros
        for r0 in range(0, S, rc):
            scr[pl.ds(GPAD + r0, rc), :] = u_ref[0, pl.ds(r0, rc), :]
        w0, w1, w2, b = w_ref[0:1, :], w_ref[1:2, :], w_ref[2:3, :], b_ref[...]
        for r0 in range(0, S, rc):
            gc = (scr[pl.ds(GPAD + r0 - 1, rc), :] * w0 + scr[pl.ds(GPAD + r0, rc), :] * w1
                  + scr[pl.ds(GPAD + r0 + 1, rc), :] * w2 + b)
            o_ref[pl.ds(r0, rc), :] = (gc * _sigmoid(gc) * u_ref[1, pl.ds(r0, rc), :]).astype(BF16)

    return pl.pallas_call(
        body, grid=(F // tc,),
        in_specs=[pl.BlockSpec((2, S, tc), lambda j: (0, 0, j)), pl.BlockSpec((3, tc), lambda j: (0, j)),
                  pl.BlockSpec((1, tc), lambda j: (0, j))],
        out_specs=pl.BlockSpec((S, tc), lambda j: (0, j)), out_shape=jax.ShapeDtypeStruct((S, F), BF16),
        scratch_shapes=[pltpu.VMEM((S + 2 * GPAD, tc), F32)],
        compiler_params=_cparams(("parallel",)), name=name)(u, cw, cb)


def _glu_bwd(u, cw, cb, dact, name, comm=None):
    _, S, F = u.shape
    tc = _tile(F, (128,))
    rc = _tile(S, (512,))

    def body(u_ref, w_ref, b_ref, da_ref, du_ref, dw_ref, db_ref, scr_g, scr_d):
        zeros = jnp.zeros((GPAD, tc), F32)
        for scr in (scr_g, scr_d):
            scr[pl.ds(0, GPAD), :] = zeros
            scr[pl.ds(GPAD + S, GPAD), :] = zeros
        for r0 in range(0, S, rc):
            scr_g[pl.ds(GPAD + r0, rc), :] = u_ref[0, pl.ds(r0, rc), :]
        w0, w1, w2, b = w_ref[0:1, :], w_ref[1:2, :], w_ref[2:3, :], b_ref[...]
        sums = [jnp.zeros((1, tc), F32) for _ in range(4)]
        for r0 in range(0, S, rc):
            gp = scr_g[pl.ds(GPAD + r0 - 1, rc), :]
            g0 = scr_g[pl.ds(GPAD + r0, rc), :]
            gn = scr_g[pl.ds(GPAD + r0 + 1, rc), :]
            gc = gp * w0 + g0 * w1 + gn * w2 + b
            sig = _sigmoid(gc)
            da = da_ref[pl.ds(r0, rc), :]
            du_ref[1, pl.ds(r0, rc), :] = (da * (gc * sig)).astype(BF16)
            dgc = da * u_ref[1, pl.ds(r0, rc), :] * (sig * (1.0 + gc * (1.0 - sig)))
            scr_d[pl.ds(GPAD + r0, rc), :] = dgc
            for n, t in enumerate((dgc * gp, dgc * g0, dgc * gn, dgc)):
                sums[n] = sums[n] + jnp.sum(t, axis=0, keepdims=True)
        dw_ref[...] = jnp.concatenate(sums[:3], axis=0)
        db_ref[...] = sums[3]
        for r0 in range(0, S, rc):
            dg = (scr_d[pl.ds(GPAD + r0 + 1, rc), :] * w0 + scr_d[pl.ds(GPAD + r0, rc), :] * w1
                  + scr_d[pl.ds(GPAD + r0 - 1, rc), :] * w2)
            du_ref[0, pl.ds(r0, rc), :] = dg.astype(BF16)

    col = pl.BlockSpec((S, tc), lambda j: (0, j))
    return _call(
        body, (F // tc,), ("parallel",),
        [pl.BlockSpec((2, S, tc), lambda j: (0, 0, j)), pl.BlockSpec((3, tc), lambda j: (0, j)),
         pl.BlockSpec((1, tc), lambda j: (0, j)), col],
        [pl.BlockSpec((2, S, tc), lambda j: (0, 0, j)), pl.BlockSpec((3, tc), lambda j: (0, j)),
         pl.BlockSpec((1, tc), lambda j: (0, j))],
        [jax.ShapeDtypeStruct((2, S, F), BF16), jax.ShapeDtypeStruct((3, F), F32), jax.ShapeDtypeStruct((1, F), F32)],
        [pltpu.VMEM((S + 2 * GPAD, tc), F32), pltpu.VMEM((S + 2 * GPAD, tc), F32)], [u, cw, cb, dact], name, comm)


def _dot_nt(a, b):
    return lax.dot_general(a, b, (((1,), (1,)), ((), ())), preferred_element_type=F32)


def _dot_tn(a, b):
    return lax.dot_general(a, b, (((0,), (0,)), ((), ())), preferred_element_type=F32)


def _swa_specs(S, nq, G, hd, bq):
    prev = lambda j: jnp.maximum(j - 1, 0)
    nxt = lambda j: jnp.minimum(j + 1, nq - 1)
    kv = [pl.BlockSpec((None, bq, hd), functools.partial(lambda kh, j, f: (kh, f(j), 0), f=f))
          for f in (prev, lambda j: j, nxt)]
    pk = [pl.BlockSpec((1, bq), functools.partial(lambda kh, j, f: (0, f(j)), f=f)) for f in (prev, lambda j: j, nxt)]
    smem = pl.BlockSpec(memory_space=pltpu.SMEM)
    return ([pl.BlockSpec((bq, G * hd), lambda kh, j: (j, kh))] + kv + kv
            + [pl.BlockSpec((bq, 1), lambda kh, j: (j, 0))] + pk + [smem, smem])


def _swa_stack(x, G):
    w = x.shape[1] // G
    return jnp.concatenate([x[:, g * w:(g + 1) * w] for g in range(G)], axis=0)


def _swa_unstack(x, G):
    bq = x.shape[0] // G
    return jnp.concatenate([x[g * bq:(g + 1) * bq] for g in range(G)], axis=1)


def _swa_scores(j, kh, S, bq, G, qs, kspan, pq, pk, slopes):
    qi = j * bq + lax.broadcasted_iota(jnp.int32, (bq, 1), 0)
    ki = (j - 1) * bq + lax.broadcasted_iota(jnp.int32, (1, 3 * bq), 1)
    bias = jnp.where((jnp.abs(qi - ki) <= SWA_WINDOW) & (ki >= 0) & (ki < S), 0.0, NEG)
    dist = jnp.abs(pq - pk)
    raw = _dot_nt(qs, kspan)
    return [raw[g * bq:(g + 1) * bq] - slopes[kh * G + g] * dist + bias for g in range(G)]


def _swa_fwd_attn(qn, kT, vT, posq, posk, slopes, sinks, name, comm=None):
    Hkv, S, hd = kT.shape
    G = qn.shape[1] // (Hkv * hd)
    bq = SWA_WINDOW
    nq = S // bq

    def body(q_ref, k0, k1, k2, v0, v1, v2, pq, p0, p1, p2, slopes_ref, sinks_ref, o_ref, l_ref):
        kh, j = pl.program_id(0), pl.program_id(1)
        kspan = jnp.concatenate([k0[...], k1[...], k2[...]], axis=0)
        vspan = jnp.concatenate([v0[...], v1[...], v2[...]], axis=0)
        pk = jnp.concatenate([p0[...], p1[...], p2[...]], axis=1)
        ps, lses = [], []
        for g, s in enumerate(_swa_scores(j, kh, S, bq, G, _swa_stack(q_ref[...], G), kspan, pq[...], pk, slopes_ref)):
            sink = sinks_ref[kh * G + g]
            m = jnp.maximum(jnp.max(s, axis=-1, keepdims=True), sink)
            p = jnp.exp(s - m)
            den = jnp.sum(p, axis=-1, keepdims=True) + jnp.exp(sink - m)
            ps.append((p / den).astype(BF16))
            lses.append(m + jnp.log(den))
        o_ref[...] = _swa_unstack(jnp.dot(jnp.concatenate(ps, axis=0), vspan, preferred_element_type=F32), G)
        l_ref[...] = jnp.concatenate(lses, axis=1)

    return _call(
        body, (Hkv, nq), ("parallel", "parallel"), _swa_specs(S, nq, G, hd, bq),
        [pl.BlockSpec((bq, G * hd), lambda kh, j: (j, kh)), pl.BlockSpec((None, bq, G), lambda kh, j: (kh, j, 0))],
        [jax.ShapeDtypeStruct(qn.shape, F32), jax.ShapeDtypeStruct((Hkv, S, G), F32)], [],
        [qn, kT, kT, kT, vT, vT, vT, posq, posk, posk, posk, slopes, sinks], name, comm)


def _swa_bwd_attn(qn, kT, vT, posq, posk, slopes, sinks, o, lse, do, name, comm=None):
    Hkv, S, hd = kT.shape
    G = qn.shape[1] // (Hkv * hd)
    bq = SWA_WINDOW
    nq = S // bq

    def body(q_ref, k0, k1, k2, v0, v1, v2, pq, p0, p1, p2, slopes_ref, sinks_ref, o_ref, l_ref, do_ref,
             dq_ref, dk_ref, dv_ref, ds_ref):
        kh, j = pl.program_id(0), pl.program_id(1)
        kspan = jnp.concatenate([k0[...], k1[...], k2[...]], axis=0)
        vspan = jnp.concatenate([v0[...], v1[...], v2[...]], axis=0)
        pk = jnp.concatenate([p0[...], p1[...], p2[...]], axis=1)
        qs = _swa_stack(q_ref[...], G)
        dos = _swa_stack(do_ref[...], G)
        delta = jnp.sum(dos * _swa_stack(o_ref[...], G), axis=-1, keepdims=True)
        dos = dos.astype(BF16)
        dp = _dot_nt(dos, vspan)
        ps, dss, dsinks = [], [], []
        for g, s in enumerate(_swa_scores(j, kh, S, bq, G, qs, kspan, pq[...], pk, slopes_ref)):
            rows = slice(g * bq, (g + 1) * bq)
            lg = l_ref[:, g:g + 1]
            p = jnp.exp(s - lg)
            ps.append(p.astype(BF16))
            dss.append((p * (dp[rows] - delta[rows])).astype(BF16))
            dsinks.append(jnp.sum(-jnp.exp(sinks_ref[kh * G + g] - lg) * delta[rows], axis=0, keepdims=True))
        dsc = jnp.concatenate(dss, axis=0)
        dq_ref[...] = _swa_unstack(jnp.dot(dsc, kspan, preferred_element_type=F32), G)
        dk_ref[...] = _dot_tn(dsc, qs)
        dv_ref[...] = _dot_tn(jnp.concatenate(ps, axis=0), dos)

        @pl.when(j == 0)
        def _():
            ds_ref[...] = jnp.zeros_like(ds_ref)

        ds_ref[...] += jnp.concatenate(dsinks, axis=1)

    qblk = pl.BlockSpec((bq, G * hd), lambda kh, j: (j, kh))
    span = pl.BlockSpec((None, None, 3 * bq, hd), lambda kh, j: (kh, j, 0, 0))
    return _call(
        body, (Hkv, nq), ("parallel", "arbitrary"),
        _swa_specs(S, nq, G, hd, bq) + [qblk, pl.BlockSpec((None, bq, G), lambda kh, j: (kh, j, 0)), qblk],
        [qblk, span, span, pl.BlockSpec((None, 1, G), lambda kh, j: (kh, 0, 0))],
        [jax.ShapeDtypeStruct(qn.shape, F32), jax.ShapeDtypeStruct((Hkv, nq, 3 * bq, hd), F32),
         jax.ShapeDtypeStruct((Hkv, nq, 3 * bq, hd), F32), jax.ShapeDtypeStruct((Hkv, 1, G), F32)], [],
        [qn, kT, kT, kT, vT, vT, vT, posq, posk, posk, posk, slopes, sinks, o, lse, do], name, comm)


def _overlap3(spans, name):
    Hkv, nq, bq3, hd = spans.shape
    bq = bq3 // 3
    sp = spans.reshape(Hkv, nq, 3, bq, hd)

    def body(a_ref, b_ref, c_ref, o_ref):
        b = pl.program_id(1)
        acc = b_ref[...]
        acc = acc + jnp.where(b > 0, a_ref[...], 0.0)
        acc = acc + jnp.where(b < nq - 1, c_ref[...], 0.0)
        o_ref[...] = acc

    def part(f, slot):
        return pl.BlockSpec((None, None, None, bq, hd), lambda kh, b: (kh, f(b), slot, 0, 0))

    return pl.pallas_call(
        body, grid=(Hkv, nq),
        in_specs=[part(lambda b: jnp.maximum(b - 1, 0), 2), part(lambda b: b, 1), part(lambda b: jnp.minimum(b + 1, nq - 1), 0)],
        out_specs=pl.BlockSpec((None, bq, hd), lambda kh, b: (kh, b, 0)),
        out_shape=jax.ShapeDtypeStruct((Hkv, nq * bq, hd), F32),
        compiler_params=_cparams(("parallel", "parallel")), name=name)(sp, sp, sp)


def _mla_fwd_attn(q, k, vT, name, comm=None):
    H, S, dk = q.shape
    dv = vT.shape[1]
    tq = _tile(S, (1024, 512, 256, 128))
    tk = _tile(S, (1024, 512, 256, 128))
    cq = _tile(tq, (256, 128))
    nk = S // tk

    def body(q_ref, k_ref, v_ref, o_ref, l_ref, m_s, l_s, acc):
        kk = pl.program_id(2)

        @pl.when(kk == 0)
        def _():
            m_s[...] = jnp.full_like(m_s, NEG)
            l_s[...] = jnp.zeros_like(l_s)
            acc[...] = jnp.zeros_like(acc)

        kb, vb = k_ref[...], v_ref[...]
        for c0 in range(0, tq, cq):
            cols = pl.ds(c0, cq)
            sT = _dot_nt(kb, q_ref[cols, :])
            m_old = m_s[:, cols]
            m_new = jnp.maximum(m_old, jnp.max(sT, axis=0, keepdims=True))
            a = jnp.exp(m_old - m_new)
            p = jnp.exp(sT - m_new)
            l_s[:, cols] = a * l_s[:, cols] + jnp.sum(p, axis=0, keepdims=True)
            acc[:, cols] = a * acc[:, cols] + jnp.dot(vb, p.astype(BF16), preferred_element_type=F32)
            m_s[:, cols] = m_new

        @pl.when(kk == nk - 1)
        def _():
            o_ref[...] = acc[...] / l_s[...]
            l_ref[...] = m_s[...] + jnp.log(l_s[...])

    return _call(
        body, (H, S // tq, nk), ("parallel", "parallel", "arbitrary"),
        [pl.BlockSpec((None, tq, dk), lambda h, i, kk: (h, i, 0)),
         pl.BlockSpec((None, tk, dk), lambda h, i, kk: (h, kk, 0)),
         pl.BlockSpec((None, dv, tk), lambda h, i, kk: (h, 0, kk))],
        [pl.BlockSpec((None, dv, tq), lambda h, i, kk: (h, 0, i)), pl.BlockSpec((None, 1, tq), lambda h, i, kk: (h, 0, i))],
        [jax.ShapeDtypeStruct((H, dv, S), F32), jax.ShapeDtypeStruct((H, 1, S), F32)],
        [pltpu.VMEM((1, tq), F32), pltpu.VMEM((1, tq), F32), pltpu.VMEM((dv, tq), F32)], [q, k, vT], name, comm)


def _mla_bwd_attn(q, k, kT, v, oT, doT, lse, name):
    H, S, dk = q.shape
    dv = v.shape[2]
    tq = _tile(S, (512, 256, 128))
    tk = _tile(S, (1024, 512, 256, 128))
    nq = S // tq

    def body(q_ref, k_ref, kT_ref, v_ref, o_ref, do_ref, l_ref, dq_ref, dk_ref, dv_ref, dk_acc, dv_acc):
        j, i = pl.program_id(1), pl.program_id(2)

        @pl.when(i == 0)
        def _():
            dk_acc[...] = jnp.zeros_like(dk_acc)
            dv_acc[...] = jnp.zeros_like(dv_acc)

        @pl.when((i == 0) & (j == 0))
        def _():
            dq_ref[...] = jnp.zeros_like(dq_ref)

        qb, dob = q_ref[...], do_ref[...]
        delta = jnp.sum(dob * o_ref[...], axis=0, keepdims=True)
        dob = dob.astype(BF16)
        pT = jnp.exp(_dot_nt(k_ref[...], qb) - l_ref[...])
        dpT = jnp.dot(v_ref[...], dob, preferred_element_type=F32)
        dsT = (pT * (dpT - delta)).astype(BF16)
        dk_acc[...] += jnp.dot(dsT, qb, preferred_element_type=F32)
        dv_acc[...] += _dot_nt(pT.astype(BF16), dob)
        cols = pl.ds(pl.multiple_of(i * tq, tq), tq)
        dq_ref[:, cols] += jnp.dot(kT_ref[...], dsT, preferred_element_type=F32)

        @pl.when(i == nq - 1)
        def _():
            dk_ref[...] = dk_acc[...]
            dv_ref[...] = dv_acc[...]

    qrow = lambda d: pl.BlockSpec((None, tq, d), lambda h, j, i: (h, i, 0))
    krow = lambda d: pl.BlockSpec((None, tk, d), lambda h, j, i: (h, j, 0))
    qcol = lambda d: pl.BlockSpec((None, d, tq), lambda h, j, i: (h, 0, i))
    return pl.pallas_call(
        body, grid=(H, S // tk, nq),
        in_specs=[qrow(dk), krow(dk), pl.BlockSpec((None, dk, tk), lambda h, j, i: (h, 0, j)), krow(dv), qcol(dv), qcol(dv),
                  qcol(1)],
        out_specs=[pl.BlockSpec((None, dk, S), lambda h, j, i: (h, 0, 0)), krow(dk), krow(dv)],
        out_shape=[jax.ShapeDtypeStruct((H, dk, S), F32), jax.ShapeDtypeStruct((H, S, dk), F32),
                   jax.ShapeDtypeStruct((H, S, dv), F32)],
        scratch_shapes=[pltpu.VMEM((tk, dk), F32), pltpu.VMEM((tk, dv), F32)],
        compiler_params=_cparams(("parallel", "arbitrary", "arbitrary")), name=name)(q, k, kT, v, oT, doT, lse)


def _sum_heads(x, name):
    H, S, d = x.shape
    ts = _tile(S, (1024, 512, 256, 128))

    def body(x_ref, o_ref):
        acc = x_ref[0]
        for h in range(1, H):
            acc = acc + x_ref[h]
        o_ref[...] = acc

    return pl.pallas_call(body, grid=(S // ts,), in_specs=[pl.BlockSpec((H, ts, d), lambda i: (0, i, 0))],
                          out_specs=pl.BlockSpec((ts, d), lambda i: (i, 0)), out_shape=jax.ShapeDtypeStruct((S, d), F32),
                          compiler_params=_cparams(("parallel",)), name=name)(x)


def _loss_head(y, target, name):
    S, D = y.shape
    tr = _row_tile(S, D)
    row = pl.BlockSpec((tr, D), lambda i: (i, 0))
    vec = pl.BlockSpec((1, D), lambda i: (0, 0))

    def body(y_ref, t_ref, sq_ref, dy_ref):
        @pl.when(pl.program_id(0) == 0)
        def _():
            sq_ref[...] = jnp.zeros_like(sq_ref)

        e = y_ref[...] - t_ref[...]
        sq_ref[...] += jnp.sum(e * e, axis=0, keepdims=True)
        dy_ref[...] = e / D

    return pl.pallas_call(body, grid=(S // tr,), in_specs=[row, row], out_specs=[vec, row],
                          out_shape=[jax.ShapeDtypeStruct((1, D), F32), jax.ShapeDtypeStruct((S, D), F32)],
                          compiler_params=_cparams(("arbitrary",)), name=name)(y, target)


def _sum_chips(recv, name):
    _, R, C = recv.shape
    tr = _row_tile(R, C)

    def body(r_ref, o_ref):
        acc = r_ref[0].astype(F32)
        for i in range(1, N_CHIPS):
            acc = acc + r_ref[i].astype(F32)
        o_ref[...] = acc

    return pl.pallas_call(body, grid=(R // tr,), in_specs=[pl.BlockSpec((N_CHIPS, tr, C), lambda i: (0, i, 0))],
                          out_specs=pl.BlockSpec((tr, C), lambda i: (i, 0)), out_shape=jax.ShapeDtypeStruct((R, C), F32),
                          compiler_params=_cparams(("parallel",)), name=name)(recv)


def _adamw(w, m, v, gs, name, comm=None):
    R, C = w.shape
    tr = _row_tile(R, 2 * C)
    row = pl.BlockSpec((tr, C), lambda i: (i, 0))
    n = len(gs)

    def body(*refs):
        w_ref, m_ref, v_ref = refs[:3]
        g_ref, d_ref, nm_ref, nv_ref = refs[3 + n:]
        g = refs[3][...]
        for r in refs[4:3 + n]:
            g = g + r[...]
        mm = ADAM_B1 * m_ref[...] + (1.0 - ADAM_B1) * g
        vv = ADAM_B2 * v_ref[...] + (1.0 - ADAM_B2) * (g * g)
        m_hat = mm / (1.0 - ADAM_B1 ** ADAM_STEP)
        v_hat = vv / (1.0 - ADAM_B2 ** ADAM_STEP)
        g_ref[...] = g
        d_ref[...] = -ADAM_LR * (m_hat / (jnp.sqrt(v_hat) + ADAM_EPS) + ADAM_WD * w_ref[...])
        nm_ref[...] = mm
        nv_ref[...] = vv

    sds = jax.ShapeDtypeStruct((R, C), F32)
    return _call(body, (R // tr,), ("parallel",), [row] * (3 + n), [row] * 4, [sds] * 4, [], [w, m, v, *gs], name, comm)


def _chip_peers():
    x, y, c = lax.axis_index("x"), lax.axis_index("y"), lax.axis_index("c")
    others = [(1 - x, y), (x, 1 - y), (1 - x, 1 - y)]
    return x, y, c, 2 * x + y, [(px, py, 2 * px + py) for px, py in others]


HALVE_MIN_BYTES = 1 << 20


def _gather_comm(srcs):
    n = len(srcs)
    shapes = [tuple(a.shape[1:] if l is not None else a.shape) for a, l in srcs]
    halved = [s[0] % 2 == 0 and int(np.prod(s)) * a.dtype.itemsize >= HALVE_MIN_BYTES for s, (a, _) in zip(shapes, srcs)]
    outs = [jax.ShapeDtypeStruct((N_CHIPS,) + s, a.dtype) for s, (a, _) in zip(shapes, srcs)]
    dma = pltpu.SemaphoreType.DMA
    sems = [dma((3 * n,)), dma((3 * n,)), dma((3 * n,)), dma((3 * n,)), dma((n,))]

    def rows(ref, t, half):
        if not halved[t]:
            return ref
        h = shapes[t][0] // 2
        return ref.at[pl.ds(half * h, h)]

    def copies(in_refs, out_refs, sem_refs):
        send, recv, fsend, frecv, lsem = sem_refs
        x, y, c, me, peers = _chip_peers()
        src = [in_refs[t] if l is None else in_refs[t].at[l] for t, (_, l) in enumerate(srcs)]
        local = [pltpu.make_async_copy(src[t], out_refs[t].at[me], lsem.at[t]) for t in range(n)]

        def ici(t, j, origin):
            px, py, _ = peers[j]
            return pltpu.make_async_remote_copy(src_ref=rows(src[t], t, c), dst_ref=rows(out_refs[t].at[origin], t, c),
                                                send_sem=send.at[3 * t + j], recv_sem=recv.at[3 * t + j],
                                                device_id=(px, py, c), device_id_type=MESH)

        def hand(t, j, half):
            blk = rows(out_refs[t].at[peers[j][2]], t, half)
            return pltpu.make_async_remote_copy(src_ref=blk, dst_ref=blk, send_sem=fsend.at[3 * t + j],
                                                recv_sem=frecv.at[3 * t + j], device_id=(x, y, 1 - c), device_id_type=MESH)

        return c, me, peers, local, ici, hand

    def start(in_refs, out_refs, sem_refs):
        c, me, peers, local, ici, hand = copies(in_refs, out_refs, sem_refs)
        for t in range(n):
            local[t].start()
            for j in range(3):
                ici(t, j, me).start()

    def finish(in_refs, out_refs, sem_refs):
        c, me, peers, local, ici, hand = copies(in_refs, out_refs, sem_refs)
        for t in range(n):
            for j in range(3):
                ici(t, j, peers[j][2]).wait_recv()
                if halved[t]:
                    hand(t, j, c).start()
        for t in range(n):
            for j in range(3):
                if halved[t]:
                    hand(t, j, 1 - c).wait_recv()
        for t in range(n):
            for j in range(3):
                ici(t, j, me).wait_send()
                if halved[t]:
                    hand(t, j, c).wait_send()
            local[t].wait()

    return _Comm([a for a, _ in srcs], outs, {}, sems, start, finish)


def _scatter_comm(items, bufs):
    n = len(items)
    dma = pltpu.SemaphoreType.DMA
    sems = [dma((3 * n,)), dma((3 * n,)), dma((n,))]

    def piece(ref, rows):
        return ref if rows is None else ref.at[pl.ds(rows[0], rows[1])]

    def copies(in_refs, out_refs, sem_refs):
        send, recv, lsem = sem_refs
        x, y, c, me, peers = _chip_peers()

        def local(t):
            _, bi, l, rows, cut = items[t]
            return pltpu.make_async_copy(piece(in_refs[t].at[me], rows if cut else None),
                                         piece(out_refs[bi].at[me, l], rows), lsem.at[t])

        def ici(t, j, origin):
            _, bi, l, rows, cut = items[t]
            px, py, pi = peers[j]
            return pltpu.make_async_remote_copy(src_ref=piece(in_refs[t].at[pi], rows if cut else None),
                                                dst_ref=piece(out_refs[bi].at[origin, l], rows),
                                                send_sem=send.at[3 * t + j], recv_sem=recv.at[3 * t + j],
                                                device_id=(px, py, c), device_id_type=MESH)

        return me, peers, local, ici

    def start(in_refs, out_refs, sem_refs):
        me, peers, local, ici = copies(in_refs, out_refs, sem_refs)
        for t in range(n):
            local(t).start()
            for j in range(3):
                ici(t, j, me).start()

    def finish(in_refs, out_refs, sem_refs):
        me, peers, local, ici = copies(in_refs, out_refs, sem_refs)
        for t in range(n):
            for j in range(3):
                ici(t, j, peers[j][2]).wait_recv()
        for t in range(n):
            for j in range(3):
                ici(t, j, me).wait_send()
            local(t).wait()

    return _Comm([it[0] for it in items] + list(bufs), [jax.ShapeDtypeStruct(b.shape, b.dtype) for b in bufs],
                 {n + k: k for k in range(len(bufs))}, sems, start, finish)


def _run_comm(comm, name):
    ni, no = len(comm.ins), len(comm.outs)

    def body(*refs):
        comm.start(refs[:ni], refs[ni:ni + no], refs[ni + no:])
        comm.finish(refs[:ni], refs[ni:ni + no], refs[ni + no:])

    hbm = pl.BlockSpec(memory_space=pl.ANY)
    return pl.pallas_call(body, in_specs=[hbm] * ni, out_specs=[hbm] * no, out_shape=list(comm.outs),
                          scratch_shapes=list(comm.sems), input_output_aliases=dict(comm.aliases), name=name)(*comm.ins)


def _swap_comm(arrs):
    n = len(arrs)

    def copies(in_refs, out_refs, sem_refs):
        x, y, c = lax.axis_index("x"), lax.axis_index("y"), lax.axis_index("c")
        return [pltpu.make_async_remote_copy(src_ref=in_refs[t], dst_ref=out_refs[t], send_sem=sem_refs[0].at[t],
                                             recv_sem=sem_refs[1].at[t], device_id=(x, y, 1 - c), device_id_type=MESH)
                for t in range(n)]

    def start(in_refs, out_refs, sem_refs):
        for cp in copies(in_refs, out_refs, sem_refs):
            cp.start()

    def finish(in_refs, out_refs, sem_refs):
        for cp in copies(in_refs, out_refs, sem_refs):
            cp.wait()

    return _Comm(list(arrs), [jax.ShapeDtypeStruct(a.shape, a.dtype) for a in arrs], {},
                 [pltpu.SemaphoreType.DMA((n,)), pltpu.SemaphoreType.DMA((n,))], start, finish)


def _allreduce_small(buf, name):
    R, C = buf.shape

    def body(b_ref, o_ref, recv, send_sems, recv_sems):
        x, y, c = lax.axis_index("x"), lax.axis_index("y"), lax.axis_index("c")
        me = 4 * x + 2 * y + c
        recv[me] = b_ref[...]
        cps = []
        for mask in range(1, N_DEV):
            px, py, pc = x ^ (mask >> 2 & 1), y ^ (mask >> 1 & 1), c ^ (mask & 1)
            cp = pltpu.make_async_remote_copy(src_ref=b_ref, dst_ref=recv.at[me], send_sem=send_sems.at[mask - 1],
                                              recv_sem=recv_sems.at[mask - 1], device_id=(px, py, pc), device_id_type=MESH)
            cp.start()
            cps.append((cp, 4 * px + 2 * py + pc, (px, py, pc), mask))
        for cp, pi, dev, mask in cps:
            pltpu.make_async_remote_copy(src_ref=b_ref, dst_ref=recv.at[pi], send_sem=send_sems.at[mask - 1],
                                         recv_sem=recv_sems.at[mask - 1], device_id=dev, device_id_type=MESH).wait_recv()
        for cp, _, _, _ in cps:
            cp.wait_send()
        acc = recv[0]
        for i in range(1, N_DEV):
            acc = acc + recv[i]
        o_ref[...] = acc

    vm = pl.BlockSpec(memory_space=pltpu.VMEM)
    return pl.pallas_call(
        body, in_specs=[vm], out_specs=vm, out_shape=jax.ShapeDtypeStruct((R, C), F32),
        scratch_shapes=[pltpu.VMEM((N_DEV, R, C), F32), pltpu.SemaphoreType.DMA((N_DEV - 1,)),
                        pltpu.SemaphoreType.DMA((N_DEV - 1,))],
        compiler_params=pltpu.CompilerParams(vmem_limit_bytes=VMEM_LIMIT), name=name)(buf)


def _pack(arrs):
    parts, offs, n = [], [], 0
    for a in arrs:
        f = a.reshape(-1).astype(F32)
        k = -(-f.shape[0] // LANES) * LANES
        parts.append(jnp.pad(f, (0, k - f.shape[0])))
        offs.append(n)
        n += k
    total = -(-n // (8 * LANES)) * (8 * LANES)
    if total > n:
        parts.append(jnp.zeros((total - n,), F32))
    return jnp.concatenate(parts).reshape(-1, LANES), offs


def _unpack(buf, offs, shapes):
    flat = buf.reshape(-1)
    return [flat[o:o + int(np.prod(s))].reshape(s) for o, s in zip(offs, shapes)]


def _ride(call, riders, name, made=None):
    rider = riders.get(name)
    if rider is None:
        return call(name, None)
    build, done = rider if isinstance(rider, tuple) else rider(made)
    out, res = call(name, build())
    done(res)
    return out


def _ffn_fwd(xm, g, w_in, cw, cb, w_out, t, riders):
    h2 = _rms_fwd(xm, g, BF16, f"ffn_norm{t}")
    u = _ride(lambda n, c: _mm_nn(h2, w_in, F32, n, out_panels=2, comm=c), riders, f"ffn_in{t}")
    act = _glu_fwd(u, cw, cb, f"glu{t}")
    w_out = w_out()
    xo = _ride(lambda n, c: _mm_nn(act, w_out, F32, n, res=xm, comm=c), riders, f"ffn_out{t}")
    return xo, (h2, u, act)


def _ffn_bwd(dxo, xm, g, w_in, cw, cb, w_out, saved, t, riders, split_dwin=False):
    h2, u, act = saved
    w_out = w_out()
    made = {}
    made['ffn_w_out'] =_ride(lambda n, c: _mm_tn(act, dxo, BF16, n, comm=c), riders, f"ffn_dwout{t}", made)
    dact = _ride(lambda n, c: _mm_nt(dxo, w_out, F32, n, comm=c), riders, f"ffn_dact{t}", made)
    du, dcw, dcb = _ride(lambda n, c: _glu_bwd(u, cw, cb, dact, n, comm=c), riders, f"glu_bwd{t}", made)
    dh2 = _ride(lambda n, c: _mm_nt(du, w_in, F32, n, comm=c), riders, f"ffn_dh{t}", made)
    if split_dwin:
        half = h2.shape[1] // 2
        made['ffn_w_in_a'] = _ride(lambda n, c: _mm_tn(h2[:, :half], du, BF16, n, out_panels=N_CHIPS, comm=c),
                                   riders, f"ffn_dwin{t}a", made)
        dw_in = (made['ffn_w_in_a'], _ride(lambda n, c: _mm_tn(h2[:, half:], du, BF16, n, out_panels=N_CHIPS, comm=c),
                                           riders, f"ffn_dwin{t}b", made))
    else:
        dw_in = _ride(lambda n, c: _mm_tn(h2, du, BF16, n, out_panels=N_CHIPS, comm=c), riders, f"ffn_dwin{t}", made)
    dxm, dg = _rms_bwd(xm, g, dh2, f"ffn_norm_bwd{t}", res=dxo)
    return dxm, dict(ffn_w_in=dw_in, ffn_conv_w=dcw, ffn_conv_b=dcb, ffn_w_out=made['ffn_w_out'], norm_ffn_g=dg)


def _pool_fwd(x, g, wg, scale, t, riders):
    h = _rms_fwd(x, g, F32, f"pool_norm{t}")
    pooled = _pool_win(h, True, BF16, f"pool_win{t}")
    yraw = _group_mm("nn", pooled, wg, F32, f"pool_mm{t}")
    xm = _scale_res(x, yraw, scale, f"pool_out{t}")
    return xm, (pooled, yraw)


def _pool_bwd(dxm, x, g, wg, scale, saved, t, riders):
    pooled, yraw = saved
    dyraw, dscale = _scale_bwd(dxm, yraw, scale, f"pool_out_bwd{t}")
    dpool = _group_mm("nt", dyraw, wg, F32, f"pool_dp{t}")
    dw = _group_mm("tn", pooled, dyraw, F32, f"pool_dw{t}").astype(BF16)
    dh = _pool_win(dpool, False, F32, f"pool_win_bwd{t}")
    dx, dgn = _rms_bwd(x, g, dh, f"pool_norm_bwd{t}", res=dxm)
    return dx, dict(pool_w=dw, pool_scale=dscale, norm_mix_g=dgn)


def _swa_fwd(x, g, w_qkv, q_gain, k_gain, sinks, w_o, pos, t, riders):
    S = x.shape[0]
    Hq, Hkv, hd = SWA_HEADS, SWA_KV_HEADS, SWA_HEAD_DIM
    nq, nkv = Hq * hd, Hkv * hd
    posq, posk, slopes = pos
    h = _rms_fwd(x, g, BF16, f"swa_norm{t}")
    qkv = _mm_nn(h, w_qkv, F32, f"swa_qkv{t}")
    q = qkv[:, :nq].reshape(S * Hq, hd)
    k = qkv[:, nq:nq + nkv].reshape(S * Hkv, hd)
    qn = _rms_fwd(q, q_gain, BF16, f"swa_qnorm{t}", scale=hd ** -0.5).reshape(S, nq)
    kT = _rms_fwd(k, k_gain, BF16, f"swa_knorm{t}").reshape(S, Hkv, hd).transpose(1, 0, 2)
    vT = qkv[:, nq + nkv:].astype(BF16).reshape(S, Hkv, hd).transpose(1, 0, 2)
    o, lse = _ride(lambda n, c: _swa_fwd_attn(qn, kT, vT, posq, posk, slopes, sinks.reshape(-1), n, comm=c), riders,
                   f"swa_attn{t}")
    xm = _mm_nn(o, w_o, F32, f"swa_o{t}", res=x)
    return xm, (h, q, k, qn, kT, vT, o, lse)


def _swa_bwd(dxm, x, g, w_qkv, q_gain, k_gain, sinks, w_o, pos, saved, t, riders):
    S = x.shape[0]
    Hq, Hkv, hd = SWA_HEADS, SWA_KV_HEADS, SWA_HEAD_DIM
    nq, nkv = Hq * hd, Hkv * hd
    posq, posk, slopes = pos
    h, q, k, qn, kT, vT, o, lse = saved
    do = _mm_nt(dxm, w_o, F32, f"swa_do{t}")
    dw_o = _mm_tn(o, dxm, BF16, f"swa_dwo{t}")
    dqn, dkp, dvp, dsink = _ride(
        lambda n, c: _swa_bwd_attn(qn, kT, vT, posq, posk, slopes, sinks.reshape(-1), o, lse, do, n, comm=c), riders,
        f"swa_attn_bwd{t}")
    dkn = _overlap3(dkp, f"swa_dk{t}").transpose(1, 0, 2).reshape(S * Hkv, hd)
    dv = _overlap3(dvp, f"swa_dv{t}").transpose(1, 0, 2).reshape(S, nkv)
    dq, dqg = _rms_bwd(q, q_gain, dqn.reshape(S * Hq, hd), f"swa_qnorm_bwd{t}", scale=hd ** -0.5)
    dk, dkg = _rms_bwd(k, k_gain, dkn, f"swa_knorm_bwd{t}")
    dqkv = jnp.concatenate([dq.reshape(S, nq), dk.reshape(S, nkv), dv], axis=1)
    dh = _mm_nt(dqkv, w_qkv, F32, f"swa_dh{t}")
    dw_qkv = _mm_tn(h, dqkv, BF16, f"swa_dwqkv{t}", out_panels=N_CHIPS)
    dx, dgn = _rms_bwd(x, g, dh, f"swa_norm_bwd{t}", res=dxm)
    return dx, dict(swa_w_qkv=dw_qkv, swa_q_gain=dqg, swa_k_gain=dkg, swa_sinks=dsink.reshape(1, Hq), swa_w_o=dw_o,
                    norm_mix_g=dgn)


def _mla_fwd(x, g, w_down, q_a_gain, kv_a_gain, w_uq, w_ukv, qn_gain, qr_gain, kn_gain, kr_gain, w_o, rope, t, riders):
    S = x.shape[0]
    H, dn, R, dv, qr_, kvr = MLA_HEADS, MLA_NOPE, MLA_ROPE, MLA_V, MLA_Q_RANK, MLA_KV_RANK
    sc = (dn + R) ** -0.5
    h = _rms_fwd(x, g, BF16, f"mla_norm{t}")
    d = _mm_nn(h, w_down, F32, f"mla_down{t}")
    cq_pre, ckv_pre, kpe_pre = d[:, :qr_], d[:, qr_:qr_ + kvr], d[:, qr_ + kvr:]
    cq = _rms_fwd(cq_pre, q_a_gain, BF16, f"mla_cq{t}")
    ckv = _rms_fwd(ckv_pre, kv_a_gain, BF16, f"mla_ckv{t}")
    q3 = _mm_nn(cq, w_uq, F32, f"mla_uq{t}").reshape(S, H, dn + R).transpose(1, 0, 2)
    kv3 = _mm_nn(ckv, w_ukv, F32, f"mla_ukv{t}").reshape(S, H, dn + dv).transpose(1, 0, 2)
    qn_in, qp_in = q3[..., :dn].reshape(H * S, dn), q3[..., dn:].reshape(H * S, R)
    kn_in, v = kv3[..., :dn].reshape(H * S, dn), kv3[..., dn:].astype(BF16)
    qn = _rms_fwd(qn_in, qn_gain, BF16, f"mla_qn{t}", scale=sc).reshape(H, S, dn)
    qp = _rms_fwd(qp_in, qr_gain, BF16, f"mla_qp{t}", scale=sc, rope=rope).reshape(H, S, R)
    kn = _rms_fwd(kn_in, kn_gain, BF16, f"mla_kn{t}").reshape(H, S, dn)
    kp = _rms_fwd(kpe_pre, kr_gain, BF16, f"mla_kp{t}", rope=rope)
    qf = jnp.concatenate([qn, qp], axis=-1)
    kf = jnp.concatenate([kn, jnp.broadcast_to(kp[None], (H, S, R))], axis=-1)
    vT = v.transpose(0, 2, 1)
    oT, lse = _ride(lambda n, c: _mla_fwd_attn(qf, kf, vT, n, comm=c), riders, f"mla_attn{t}")
    oT2 = oT.reshape(H * dv, S)
    xm = _mm_tn(oT2, w_o, F32, f"mla_o{t}", res=x)
    return xm, (h, cq_pre, ckv_pre, kpe_pre, cq, ckv, qn_in, qp_in, kn_in, v, qf, kf, oT, lse)


def _mla_bwd(dxm, x, g, w_down, q_a_gain, kv_a_gain, w_uq, w_ukv, qn_gain, qr_gain, kn_gain, kr_gain, w_o, rope,
             saved, t, riders):
    S = x.shape[0]
    H, dn, R, dv = MLA_HEADS, MLA_NOPE, MLA_ROPE, MLA_V
    sc = (dn + R) ** -0.5
    h, cq_pre, ckv_pre, kpe_pre, cq, ckv, qn_in, qp_in, kn_in, v, qf, kf, oT, lse = saved
    oT2 = oT.reshape(H * dv, S)
    doT = _mm_nt(w_o, dxm, F32, f"mla_do{t}").reshape(H, dv, S)
    dw_o = _mm_nn(oT2, dxm, BF16, f"mla_dwo{t}")
    dqT, dkf, dvv = _mla_bwd_attn(qf, kf, kf.transpose(0, 2, 1), v, oT, doT, lse, f"mla_attn_bwd{t}")
    dqf = dqT.transpose(0, 2, 1)
    dqn, dqp, dkn = dqf[..., :dn], dqf[..., dn:], dkf[..., :dn]
    dkp = _sum_heads(dkf[..., dn:], f"mla_dkp{t}")
    dqn_in, dqng = _rms_bwd(qn_in, qn_gain, dqn.reshape(H * S, dn), f"mla_qn_bwd{t}", scale=sc)
    dqp_in, dqrg = _rms_bwd(qp_in, qr_gain, dqp.reshape(H * S, R), f"mla_qp_bwd{t}", scale=sc, rope=rope)
    dkn_in, dkng = _rms_bwd(kn_in, kn_gain, dkn.reshape(H * S, dn), f"mla_kn_bwd{t}")
    dkpe_pre, dkrg = _rms_bwd(kpe_pre, kr_gain, dkp, f"mla_kp_bwd{t}", rope=rope)
    dq = jnp.concatenate([dqn_in.reshape(H, S, dn), dqp_in.reshape(H, S, R)], axis=-1).transpose(1, 0, 2).reshape(S, -1)
    dkv = jnp.concatenate([dkn_in.reshape(H, S, dn), dvv], axis=-1).transpose(1, 0, 2).reshape(S, -1)
    dcq = _mm_nt(dq, w_uq, F32, f"mla_dcq{t}")
    dw_uq = _mm_tn(cq, dq, BF16, f"mla_dwuq{t}", out_panels=N_CHIPS)
    dckv = _mm_nt(dkv, w_ukv, F32, f"mla_dckv{t}")
    dw_ukv = _mm_tn(ckv, dkv, BF16, f"mla_dwukv{t}", out_panels=N_CHIPS)
    dcq_pre, dqag = _rms_bwd(cq_pre, q_a_gain, dcq, f"mla_cq_bwd{t}")
    dckv_pre, dkvag = _rms_bwd(ckv_pre, kv_a_gain, dckv, f"mla_ckv_bwd{t}")
    dd = jnp.concatenate([dcq_pre, dckv_pre, dkpe_pre], axis=1)
    dh = _mm_nt(dd, w_down, F32, f"mla_dh{t}")
    dw_down = _mm_tn(h, dd, BF16, f"mla_dwdown{t}")
    dx, dgn = _rms_bwd(x, g, dh, f"mla_norm_bwd{t}", res=dxm)
    return dx, dict(mla_w_down=dw_down, mla_q_a_gain=dqag, mla_kv_a_gain=dkvag, mla_w_uq=dw_uq, mla_w_ukv=dw_ukv,
                    mla_qn_gain=dqng, mla_qr_gain=dqrg, mla_kn_gain=dkng, mla_kr_gain=dkrg, mla_w_o=dw_o, norm_mix_g=dgn)


def _chips_to_axis(gathered, axis):
    moved = jnp.moveaxis(gathered, 0, axis)
    shape = list(moved.shape)
    shape[axis:axis + 2] = [shape[axis] * shape[axis + 1]]
    return moved.reshape(shape)


def kernel(x, positions, norm_mix_g, norm_ffn_g, pool_w, pool_scale, swa_w_qkv, swa_q_gain, swa_k_gain, swa_sinks, swa_w_o, mla_w_down, mla_q_a_gain, mla_kv_a_gain, mla_w_uq, mla_w_ukv, mla_qn_gain, mla_qr_gain, mla_kn_gain, mla_kr_gain, mla_w_o, ffn_w_in, ffn_conv_w, ffn_conv_b, ffn_w_out, loss_target, m_norm_mix_g, m_norm_ffn_g, m_pool_w, m_pool_scale, m_swa_w_qkv, m_swa_q_gain, m_swa_k_gain, m_swa_sinks, m_swa_w_o, m_mla_w_down, m_mla_q_a_gain, m_mla_kv_a_gain, m_mla_w_uq, m_mla_w_ukv, m_mla_qn_gain, m_mla_qr_gain, m_mla_kn_gain, m_mla_kr_gain, m_mla_w_o, m_ffn_w_in, m_ffn_conv_w, m_ffn_conv_b, m_ffn_w_out, v_norm_mix_g, v_norm_ffn_g, v_pool_w, v_pool_scale, v_swa_w_qkv, v_swa_q_gain, v_swa_k_gain, v_swa_sinks, v_swa_w_o, v_mla_w_down, v_mla_q_a_gain, v_mla_kv_a_gain, v_mla_w_uq, v_mla_w_ukv, v_mla_qn_gain, v_mla_qr_gain, v_mla_kn_gain, v_mla_kr_gain, v_mla_w_o, v_ffn_w_in, v_ffn_conv_w, v_ffn_conv_b, v_ffn_w_out):
    args = dict(locals())
    W = {n: args[n] for n in WEIGHTS}
    M = {n: args["m_" + n] for n in WEIGHTS}
    V = {n: args["v_" + n] for n in WEIGHTS}
    xs = x[0]
    S, D = xs.shape
    chip = 2 * lax.axis_index("x") + lax.axis_index("y")

    big = ['pool_w', 'swa_w_qkv', 'swa_w_o', 'mla_w_down', 'mla_w_uq', 'mla_w_ukv', 'mla_w_o', 'ffn_w_in', 'ffn_w_out']
    small_sharded = [n for n in WEIGHTS if SMALL.get(n) is not None]
    mixer_w = {0: ['pool_w'], 1: ['swa_w_qkv', 'swa_w_o'], 2: ['mla_w_down', 'mla_w_uq', 'mla_w_ukv', 'mla_w_o']}
    Wb = {n: W[n].astype(BF16) for n in big}
    Wg = {}

    def mixer_keys(i):
        return [(n, i // 3) for n in mixer_w[i % 3]]

    def gather_rider(keys):
        return (lambda: _gather_comm([(Wb[n], l) for n, l in keys])), (lambda res: Wg.update(zip(keys, res)))

    assert DEPTH == 4
    keys0 = mixer_keys(0) + [('ffn_w_in', 0)]
    riders = {
        'ffn_in0': gather_rider([('ffn_w_out', 0)] + mixer_keys(1)),
        'ffn_out0': gather_rider([('ffn_w_out', 1)]),
        'swa_attn1': gather_rider([('ffn_w_in', 1)]),
        'ffn_in1': gather_rider(mixer_keys(2)),
        'ffn_out1': gather_rider([('ffn_w_out', 2)]),
        'mla_attn2': gather_rider([('ffn_w_in', 2), ('ffn_w_in', 3), ('ffn_w_out', 3)] + mixer_keys(3)),
    }
    first = _run_comm(_gather_comm([(Wb[n], l) for n, l in keys0] + [(W[n], None) for n in small_sharded]), "gather_first")
    Wg.update(zip(keys0, first))
    full = dict(W)
    for n, r in zip(small_sharded, first[len(keys0):]):
        full[n] = _chips_to_axis(r, SMALL[n])
    rows = lambda a: a.reshape((a.shape[0] * a.shape[1],) + a.shape[2:])

    posf = positions.astype(F32)
    slopes = jnp.asarray(2.0 ** (-8.0 * np.arange(1, SWA_HEADS + 1) / SWA_HEADS), dtype=F32)
    pos = (posf.reshape(S, 1), posf.reshape(1, S), slopes)
    inv = ROPE_THETA ** (-jnp.arange(0, MLA_ROPE, 2, dtype=F32) / MLA_ROPE)
    ang = posf[:, None] * inv[None, :]
    cos, sin = jnp.cos(ang), jnp.sin(ang)
    rope = (jnp.concatenate([cos, cos], axis=1), jnp.concatenate([-sin, sin], axis=1))

    margs_cache = {}

    def mixer_args(i):
        if i in margs_cache:
            return margs_cache[i]
        kind, j = i % 3, i // 3
        g = full['norm_mix_g'][i:i + 1]
        if kind == 0:
            byg = jnp.swapaxes(Wg[('pool_w', j)], 0, 1)
            a = (g, byg.reshape(byg.shape[0], -1, byg.shape[3]), full['pool_scale'][j:j + 1])
        elif kind == 1:
            a = (g, Wg[('swa_w_qkv', j)], full['swa_q_gain'][j:j + 1], full['swa_k_gain'][j:j + 1],
                 full['swa_sinks'][j:j + 1], rows(Wg[('swa_w_o', j)]), pos)
        else:
            a = (g, rows(Wg[('mla_w_down', j)]), full['mla_q_a_gain'][j:j + 1], full['mla_kv_a_gain'][j:j + 1],
                 Wg[('mla_w_uq', j)], Wg[('mla_w_ukv', j)], full['mla_qn_gain'][j:j + 1], full['mla_qr_gain'][j:j + 1],
                 full['mla_kn_gain'][j:j + 1], full['mla_kr_gain'][j:j + 1], rows(Wg[('mla_w_o', j)]), rope)
        margs_cache[i] = (kind, a)
        return kind, a

    def ffn_args(i):
        return (full['norm_ffn_g'][i:i + 1], Wg[('ffn_w_in', i)], full['ffn_conv_w'][i], full['ffn_conv_b'][i:i + 1],
                lambda: rows(Wg[('ffn_w_out', i)]))

    fwd = (_pool_fwd, _swa_fwd, _mla_fwd)
    bwd = (_pool_bwd, _swa_bwd, _mla_bwd)
    tape = []
    cur = xs
    for i in range(DEPTH):
        kind, margs = mixer_args(i)
        xm, msaved = fwd[kind](cur, *margs, i, riders)
        xo, fsaved = _ffn_fwd(xm, *ffn_args(i), i, riders)
        tape.append((cur, xm, msaved, fsaved))
        cur = xo
    sq, dcur = _loss_head(cur, loss_target[0], "loss_head")
    loss = lax.psum(0.5 / D * jnp.sum(sq), ("x", "y", "c"))

    def by_chip(n, gl):
        if n == 'pool_w':
            G, dg_ = gl.shape[0], gl.shape[1]
            return gl.reshape(G, N_CHIPS, dg_ // N_CHIPS, dg_).swapaxes(0, 1)
        if len(gl.shape) == 3:
            return gl
        return gl.reshape((N_CHIPS, gl.shape[0] // N_CHIPS) + gl.shape[1:])

    bufs = {n: lax.empty((N_CHIPS,) + tuple(W[n].shape), BF16) for n in big}

    def scatter_rider(group):
        def make(made):
            grp = group(made) if callable(group) else group
            names = list(dict.fromkeys(it[0] for it in grp))
            build = lambda: _scatter_comm([(gl, names.index(n), l, r, cut) for n, l, gl, r, cut in grp],
                                          [bufs[n] for n in names])
            return build, (lambda res: bufs.update(zip(names, res)))
        return make

    def own_w_out(i):
        return scatter_rider(lambda made: [('ffn_w_out', i, by_chip('ffn_w_out', made['ffn_w_out']), None, True)])

    grads = {n: [None] * W[n].shape[0] for n in WEIGHTS}
    half = W['ffn_w_in'].shape[1] // 2
    quarter = half // 2
    riders = {f'ffn_dwin{DEPTH - 1}': own_w_out(DEPTH - 1)}
    for i in reversed(range(DEPTH)):
        kind, margs = mixer_args(i)
        x_in, xm, msaved, fsaved = tape[i]
        dxm, fg = _ffn_bwd(dcur, xm, *ffn_args(i), fsaved, i, riders, split_dwin=(i == 0))
        dcur, mg = bwd[kind](dxm, x_in, *margs, msaved, i, riders)
        for n, gval in fg.items():
            grads[n][i] = gval
        for n, gval in mg.items():
            grads[n][i if n == 'norm_mix_g' else i // 3] = gval
        if i > 0:
            g_in = by_chip('ffn_w_in', fg['ffn_w_in'])
            piece = lambda q: ('ffn_w_in', i, g_in, (q * quarter, quarter), True)
            mixer_g = [(n, i // 3, by_chip(n, mg[n]), None, True) for n in mixer_w[kind]]
            on_attn = (i - 1) % 3 == 1
            riders = {f'ffn_dwout{i - 1}': scatter_rider([piece(0)]),
                      f'ffn_dact{i - 1}': scatter_rider([piece(1)]),
                      f'glu_bwd{i - 1}': scatter_rider([piece(2)]),
                      f'ffn_dh{i - 1}': scatter_rider([piece(3)] + ([] if on_attn else mixer_g)),
                      f'ffn_dwin{i - 1}' + ('a' if i == 1 else ''): own_w_out(i - 1)}
            if on_attn:
                riders[f'swa_attn_bwd{i - 1}'] = scatter_rider(mixer_g)
            if i == 1:
                riders['ffn_dwin0b'] = scatter_rider(
                    lambda made: [('ffn_w_in', 0, by_chip('ffn_w_in', made['ffn_w_in_a']), (0, half), False)])
    grad_x = dcur[None]
    last = [('ffn_w_in', 0, by_chip('ffn_w_in', grads['ffn_w_in'][0][1]), (half, half), False)]
    last += [(n, 0, by_chip(n, grads[n][0]), None, True) for n in mixer_w[0]]
    build, done = scatter_rider(last)(None)
    done(_run_comm(build(), "scatter_last"))

    partial = {n: _sum_chips(bufs[n].reshape(N_CHIPS, -1, bufs[n].shape[-1]), f"sum_{n}") for n in big}
    late = 'ffn_w_in'
    early = [n for n in big if n != late]
    other = dict(zip(early, _run_comm(_swap_comm([partial[n] for n in early]), "swap_cores")))
    out = {}
    for n in ['ffn_w_out'] + [n for n in early if n != 'ffn_w_out'] + [late]:
        C = partial[n].shape[1]
        comm = _swap_comm([partial[late]]) if n == 'ffn_w_out' else None
        res = _adamw(W[n].reshape(-1, C), M[n].reshape(-1, C), V[n].reshape(-1, C), [partial[n], other[n]], f"adamw_{n}",
                     comm=comm)
        if comm is not None:
            res, (other[late],) = res
        out[n] = [r.reshape(W[n].shape) for r in res]

    small = [n for n in WEIGHTS if n in SMALL]
    full_shapes = [tuple(full[n].shape) for n in small]
    sg = []
    for n in small:
        parts = grads[n]
        if n == 'ffn_conv_w':
            sg.append(jnp.stack(parts))
        else:
            sg.append(jnp.concatenate(parts, axis=0))
    buf, offs = _pack(sg)
    summed = _unpack(_allreduce_small(buf, "allreduce_small"), offs, full_shapes)
    gsm = []
    for n, gfull in zip(small, summed):
        ax = SMALL[n]
        if ax is not None:
            size = W[n].shape[ax]
            gfull = lax.dynamic_slice_in_dim(gfull, chip * size, size, axis=ax)
        gsm.append(gfull)
    gb, goffs = _pack(gsm)
    wb, _ = _pack([W[n] for n in small])
    mb, _ = _pack([M[n] for n in small])
    vb, _ = _pack([V[n] for n in small])
    res = _adamw(wb, mb, vb, [gb], "adamw_small")
    shapes = [tuple(W[n].shape) for n in small]
    unp = [_unpack(r, goffs, shapes) for r in res]
    for k, n in enumerate(small):
        out[n] = [u[k] for u in unp]

    return (loss, grad_x, *[out[n][0] for n in WEIGHTS], *[out[n][1] for n in WEIGHTS],
            *[out[n][2] for n in WEIGHTS], *[out[n][3] for n in WEIGHTS])
```

```python
import functools
import math

import numpy as np
import jax
import jax.numpy as jnp
from jax import lax
from jax.experimental import pallas as pl
from jax.experimental.pallas import tpu as pltpu

F32 = jnp.float32
BF16 = jnp.bfloat16

D_MODEL = 2048
SEQ = 4096
DEPTH = 4
EPS = 1e-6
POOL_WINDOWS = (2, 4, 8, 16)
SWA_HEADS = 32
SWA_KV_HEADS = 4
SWA_HEAD_DIM = 64
SWA_WINDOW = 128
MLA_HEADS = 16
MLA_NOPE = 128
MLA_ROPE = 64
MLA_V = 128
MLA_Q_RANK = 512
MLA_KV_RANK = 512
ROPE_THETA = 10000.0
D_FF = 5632
ADAM_LR = 0.001
ADAM_B1 = 0.9
ADAM_B2 = 0.999
ADAM_EPS = 1e-08
ADAM_WD = 0.01
ADAM_STEP = 10

N_CHIPS = 4
N_DEV = 8
MESH = pl.DeviceIdType.MESH
VMEM_LIMIT = 48 << 20
LANES = 128
NEG = -1e30

WEIGHTS = ['norm_mix_g', 'norm_ffn_g', 'pool_w', 'pool_scale', 'swa_w_qkv', 'swa_q_gain', 'swa_k_gain', 'swa_sinks',
           'swa_w_o', 'mla_w_down', 'mla_q_a_gain', 'mla_kv_a_gain', 'mla_w_uq', 'mla_w_ukv', 'mla_qn_gain',
           'mla_qr_gain', 'mla_kn_gain', 'mla_kr_gain', 'mla_w_o', 'ffn_w_in', 'ffn_conv_w', 'ffn_conv_b', 'ffn_w_out']
SMALL = {'norm_mix_g': None, 'norm_ffn_g': None, 'pool_scale': 1, 'swa_q_gain': None, 'swa_k_gain': None,
         'swa_sinks': None, 'mla_q_a_gain': 1, 'mla_kv_a_gain': 1, 'mla_qn_gain': None, 'mla_qr_gain': None,
         'mla_kn_gain': None, 'mla_kr_gain': None, 'ffn_conv_w': 2, 'ffn_conv_b': None}


def _cparams(sem):
    return pltpu.CompilerParams(dimension_semantics=sem, vmem_limit_bytes=VMEM_LIMIT)


def _tile(n, cands):
    for c in cands:
        if c <= n and n % c == 0:
            return c
    return n


WIDE = (1408, 1024, 768, 640, 512, 384, 256, 128)


def _dims(arr):
    if len(arr.shape) == 2:
        return arr.shape[0], arr.shape[1], arr.shape[1]
    return arr.shape[1], arr.shape[0] * arr.shape[2], arr.shape[2]


def _pspec(arr, tr, tc, fn):
    if len(arr.shape) == 2:
        return pl.BlockSpec((tr, tc), fn)
    per = arr.shape[2] // tc

    def im(*g):
        r, c = fn(*g)
        return (c // per, r, c % per)

    return pl.BlockSpec((None, tr, tc), im)


class _Comm:
    def __init__(self, ins, outs, aliases, sems, start, finish):
        self.ins, self.outs, self.aliases, self.sems, self.start, self.finish = ins, outs, aliases, sems, start, finish


def _call(body, grid, sem, in_specs, out_specs, out_shape, scratch, args, name, comm=None):
    if comm is None:
        return list(pl.pallas_call(body, grid=grid, in_specs=in_specs, out_specs=out_specs, out_shape=out_shape,
                                   scratch_shapes=scratch, compiler_params=_cparams(sem), name=name)(*args))
    ni, no, ns, nci, nco = len(args), len(out_shape), len(scratch), len(comm.ins), len(comm.outs)

    def wrapped(*refs):
        core_in, c_in = refs[:ni], refs[ni:ni + nci]
        core_out, c_out = refs[ni + nci:ni + nci + no], refs[ni + nci + no:ni + nci + no + nco]
        rest = refs[ni + nci + no + nco:]
        ids = [pl.program_id(d) for d in range(len(grid))]
        first = functools.reduce(lambda p, q: p & q, [i == 0 for i in ids])
        last = functools.reduce(lambda p, q: p & q, [i == n - 1 for i, n in zip(ids, grid)])

        @pl.when(first)
        def _():
            comm.start(c_in, c_out, rest[ns:])

        body(*core_in, *core_out, *rest[:ns])

        @pl.when(last)
        def _():
            comm.finish(c_in, c_out, rest[ns:])

    hbm = pl.BlockSpec(memory_space=pl.ANY)
    res = pl.pallas_call(
        wrapped, grid=grid, in_specs=list(in_specs) + [hbm] * nci, out_specs=list(out_specs) + [hbm] * nco,
        out_shape=list(out_shape) + list(comm.outs), scratch_shapes=list(scratch) + list(comm.sems),
        input_output_aliases={ni + i: no + o for i, o in comm.aliases.items()},
        compiler_params=_cparams(("arbitrary",) * len(grid)), name=name)(*args, *comm.ins)
    return list(res[:no]), list(res[no:])


MM_VMEM_BUDGET = 40 << 20
_CONTRACT = {"nn": (((1,), (0,)), ((), ())), "nt": (((1,), (1,)), ((), ())), "tn": (((0,), (0,)), ((), ()))}


def _mm(kind, a, b, out_dtype, name, res=None, out_panels=1, comm=None):
    ar, ac, aw = _dims(a)
    br, bc, bw = _dims(b)
    if kind == "nn":
        M, K, N, mw, kw, nw = ar, ac, bc, ar, aw, bw
    elif kind == "nt":
        M, K, N, mw, kw, nw = ar, ac, br, ar, math.gcd(aw, bw), br
    else:
        M, K, N, mw, kw, nw = ac, ar, bc, aw, ar, bw
    assert K == (br if kind != "nt" else bc)
    no = N // out_panels
    tm = _tile(mw, (1024, 512, 256, 128))
    tn = _tile(math.gcd(nw, no), WIDE if kind != "nt" else (1024, 512, 256, 128))
    fixed = tm * tn * (2 * jnp.dtype(out_dtype).itemsize + 4 + (8 if res is not None else 0))
    per_k = 2 * (tm * a.dtype.itemsize + tn * b.dtype.itemsize)
    tk = next((c for c in (kw, 4096, 2816, 2048, 1408, 1024, 768, 640, 512, 384, 256, 128)
               if c <= kw and kw % c == 0 and fixed + c * per_k <= MM_VMEM_BUDGET), _tile(kw, (128,)))
    nk = K // tk
    o_sds = jax.ShapeDtypeStruct((M, N) if out_panels == 1 else (out_panels, M, no), out_dtype)

    def body(*refs):
        a_ref, b_ref = refs[:2]

        def out(r):
            if res is not None:
                r = r + refs[2][...]
            return r.astype(out_dtype)

        prod = lax.dot_general(a_ref[...].astype(BF16), b_ref[...].astype(BF16), _CONTRACT[kind], preferred_element_type=F32)
        if nk == 1:
            refs[-1][...] = out(prod)
            return
        o_ref, acc = refs[-2:]
        k = pl.program_id(2)

        @pl.when(k == 0)
        def _():
            acc[...] = prod

        @pl.when(k > 0)
        def _():
            acc[...] += prod

        @pl.when(k == nk - 1)
        def _():
            o_ref[...] = out(acc[...])

    if kind == "nn":
        specs = [_pspec(a, tm, tk, lambda i, j, k: (i, k)), _pspec(b, tk, tn, lambda i, j, k: (k, j))]
    elif kind == "nt":
        specs = [_pspec(a, tm, tk, lambda i, j, k: (i, k)), _pspec(b, tn, tk, lambda i, j, k: (j, k))]
    else:
        specs = [_pspec(a, tk, tm, lambda i, j, k: (k, i)), _pspec(b, tk, tn, lambda i, j, k: (k, j))]
    args = [a, b]
    if res is not None:
        specs.append(_pspec(res, tm, tn, lambda i, j, k: (i, j)))
        args.append(res)
    got = _call(body, (M // tm, N // tn, nk), ("parallel", "parallel", "arbitrary"), specs,
                [_pspec(o_sds, tm, tn, lambda i, j, k: (i, j))], [o_sds], [pltpu.VMEM((tm, tn), F32)] if nk > 1 else [],
                args, name, comm)
    return got[0] if comm is None else (got[0][0], got[1])


def _mm_nn(a, b, out_dtype, name, **kw):
    return _mm("nn", a, b, out_dtype, name, **kw)


def _mm_nt(a, b, out_dtype, name, **kw):
    return _mm("nt", a, b, out_dtype, name, **kw)


def _mm_tn(a, b, out_dtype, name, **kw):
    return _mm("tn", a, b, out_dtype, name, **kw)


def _swap_halves(y):
    h = y.shape[-1] // 2
    return jnp.concatenate([y[:, h:], y[:, :h]], axis=1)


def _row_tile(R, d, limit=None):
    cap = max(8, (1 << 19) // d)
    cands = [c for c in (4096, 2048, 1024, 512, 256, 128, 64, 32, 16, 8) if c <= cap]
    if limit is not None:
        cands = [c for c in cands if limit % c == 0]
    return _tile(R, cands)


def _rms_fwd(x, g, out_dtype, name, scale=1.0, rope=None):
    R, d = x.shape
    tr = _row_tile(R, d, None if rope is None else rope[0].shape[0])

    def body(*refs):
        x_ref, g_ref = refs[:2]
        o_ref = refs[-1]
        xv = x_ref[...].astype(F32)
        y = xv * lax.rsqrt(jnp.mean(xv * xv, axis=-1, keepdims=True) + EPS)
        y = y * g_ref[...]
        if rope is not None:
            y = y * refs[2][...] + _swap_halves(y) * refs[3][...]
        if scale != 1.0:
            y = y * scale
        o_ref[...] = y.astype(o_ref.dtype)

    in_specs = [pl.BlockSpec((tr, d), lambda i: (i, 0)), pl.BlockSpec((1, d), lambda i: (0, 0))]
    args = [x, g]
    if rope is not None:
        nrt = rope[0].shape[0] // tr
        in_specs += [pl.BlockSpec((tr, d), lambda i: (i % nrt, 0))] * 2
        args += list(rope)
    return pl.pallas_call(
        body, grid=(R // tr,), in_specs=in_specs, out_specs=pl.BlockSpec((tr, d), lambda i: (i, 0)),
        out_shape=jax.ShapeDtypeStruct((R, d), out_dtype), compiler_params=_cparams(("parallel",)), name=name)(*args)


def _rms_bwd(x, g, dy, name, scale=1.0, rope=None, res=None):
    R, d = x.shape
    tr = _row_tile(R, d, None if rope is None else rope[0].shape[0])
    panels = dy.shape[0] if len(dy.shape) == 3 else 0

    def body(*refs):
        x_ref, g_ref, dy_ref = refs[:3]
        dx_ref, dg_ref = refs[-2:]
        i = pl.program_id(0)
        xv = x_ref[...].astype(F32)
        r = lax.rsqrt(jnp.mean(xv * xv, axis=-1, keepdims=True) + EPS)
        xhat = xv * r
        dyv = (jnp.concatenate([dy_ref[p] for p in range(panels)], axis=1) if panels else dy_ref[...]).astype(F32)
        if scale != 1.0:
            dyv = dyv * scale
        if rope is not None:
            dyv = dyv * refs[3][...] + _swap_halves(dyv * refs[4][...])

        @pl.when(i == 0)
        def _():
            dg_ref[...] = jnp.zeros_like(dg_ref)

        dg_ref[...] += jnp.sum(dyv * xhat, axis=0, keepdims=True)
        dxh = dyv * g_ref[...]
        dx = r * (dxh - xhat * jnp.mean(dxh * xhat, axis=-1, keepdims=True))
        if res is not None:
            dx = dx + refs[-3][...]
        dx_ref[...] = dx

    row = pl.BlockSpec((tr, d), lambda i: (i, 0))
    vec = pl.BlockSpec((1, d), lambda i: (0, 0))
    in_specs = [row, vec, pl.BlockSpec((panels, tr, d // panels), lambda i: (0, i, 0)) if panels else row]
    args = [x, g, dy]
    if rope is not None:
        nrt = rope[0].shape[0] // tr
        in_specs += [pl.BlockSpec((tr, d), lambda i: (i % nrt, 0))] * 2
        args += list(rope)
    if res is not None:
        in_specs.append(row)
        args.append(res)
    return pl.pallas_call(
        body, grid=(R // tr,), in_specs=in_specs, out_specs=[row, vec],
        out_shape=[jax.ShapeDtypeStruct((R, d), F32), jax.ShapeDtypeStruct((1, d), F32)],
        compiler_params=_cparams(("arbitrary",)), name=name)(*args)


PAD = 16


def _pool_win(x, fwd, out_dtype, name):
    G = len(POOL_WINDOWS)
    if fwd:
        S, D = x.shape
        dg = D // G
    else:
        _, S, dg = x.shape
    tc = _tile(dg, (128,))
    nt = dg // tc
    rc = _tile(S, (512,))

    def body(*refs):
        scr = refs[-1]
        o_refs = [refs[-2].at[gi] for gi in range(G)]
        x_refs = refs[:G] if fwd else [refs[0].at[gi] for gi in range(G)]
        zeros = jnp.zeros((PAD, tc), F32)
        scr[pl.ds(0, PAD), :] = zeros
        scr[pl.ds(PAD + S, PAD), :] = zeros
        for gi, w in enumerate(POOL_WINDOWS):
            left, right = w // 2, w - 1 - w // 2

            def count(r0):
                t = r0 + lax.broadcasted_iota(jnp.int32, (rc, 1), 0)
                return (jnp.minimum(t + right + 1, S) - jnp.maximum(t - left, 0)).astype(F32)

            for r0 in range(0, S, rc):
                xv = x_refs[gi][pl.ds(r0, rc), :]
                scr[pl.ds(PAD + r0, rc), :] = xv if fwd else xv / count(r0)
            lo, hi = (left, right) if fwd else (right, left)
            for r0 in range(0, S, rc):
                acc = scr[pl.ds(PAD + r0 - lo, rc), :]
                for o in range(-lo + 1, hi + 1):
                    acc = acc + scr[pl.ds(PAD + r0 + o, rc), :]
                xv = x_refs[gi][pl.ds(r0, rc), :]
                out = acc / count(r0) - xv if fwd else acc - xv
                o_refs[gi][pl.ds(r0, rc), :] = out.astype(out_dtype)

    panel = pl.BlockSpec((G, S, tc), lambda j: (0, 0, j))
    if fwd:
        in_specs = [pl.BlockSpec((S, tc), functools.partial(lambda j, gi: (0, gi * nt + j), gi=gi)) for gi in range(G)]
    else:
        in_specs = [panel]
    return pl.pallas_call(
        body, grid=(nt,), in_specs=in_specs, out_specs=panel, out_shape=jax.ShapeDtypeStruct((G, S, dg), out_dtype),
        scratch_shapes=[pltpu.VMEM((S + 2 * PAD, tc), F32)],
        compiler_params=_cparams(("parallel",)), name=name)(*([x] * G if fwd else [x]))


def _group_mm(kind, a, b, out_dtype, name):
    G = len(POOL_WINDOWS)
    if kind == "nt":
        S, dg = a.shape[0], a.shape[1] // G
    else:
        _, S, dg = a.shape
    tm = _tile(S, (1024, 512, 256, 128))
    nm = S // tm
    pan = pl.BlockSpec((None, tm, dg), lambda g, i: (g, i, 0))
    col = pl.BlockSpec((tm, dg), lambda g, i: (i, g))
    sq = pl.BlockSpec((None, dg, dg), lambda g, i: (g, 0, 0))

    def body(a_ref, b_ref, o_ref):
        prod = lax.dot_general(a_ref[...].astype(BF16), b_ref[...].astype(BF16), _CONTRACT[kind], preferred_element_type=F32)
        if kind != "tn":
            o_ref[...] = prod.astype(out_dtype)
            return
        i = pl.program_id(1)

        @pl.when(i == 0)
        def _():
            o_ref[...] = prod

        @pl.when(i > 0)
        def _():
            o_ref[...] += prod

    specs, out_spec, shape = {"nn": ([pan, sq], col, (S, G * dg)), "nt": ([col, sq], pan, (G, S, dg)),
                              "tn": ([pan, col], sq, (G, dg, dg))}[kind]
    assert kind != "tn" or out_dtype == F32
    return pl.pallas_call(body, grid=(G, nm), in_specs=specs, out_specs=out_spec,
                          out_shape=jax.ShapeDtypeStruct(shape, out_dtype),
                          compiler_params=_cparams(("parallel", "arbitrary")), name=name)(a, b)


def _scale_res(x, y, scale, name):
    S, D = x.shape
    tr = _row_tile(S, D)
    row = pl.BlockSpec((tr, D), lambda i: (i, 0))

    def body(x_ref, y_ref, s_ref, o_ref):
        o_ref[...] = x_ref[...] + y_ref[...] * s_ref[...]

    return pl.pallas_call(body, grid=(S // tr,), in_specs=[row, row, pl.BlockSpec((1, D), lambda i: (0, 0))],
                          out_specs=row, out_shape=jax.ShapeDtypeStruct((S, D), F32),
                          compiler_params=_cparams(("parallel",)), name=name)(x, y, scale)


def _scale_bwd(dy, y, scale, name):
    S, D = dy.shape
    tr = _row_tile(S, D)
    row = pl.BlockSpec((tr, D), lambda i: (i, 0))
    vec = pl.BlockSpec((1, D), lambda i: (0, 0))

    def body(dy_ref, y_ref, s_ref, o_ref, ds_ref):
        @pl.when(pl.program_id(0) == 0)
        def _():
            ds_ref[...] = jnp.zeros_like(ds_ref)

        d = dy_ref[...]
        ds_ref[...] += jnp.sum(d * y_ref[...], axis=0, keepdims=True)
        o_ref[...] = (d * s_ref[...]).astype(BF16)

    return pl.pallas_call(body, grid=(S // tr,), in_specs=[row, row, vec], out_specs=[row, vec],
                          out_shape=[jax.ShapeDtypeStruct((S, D), BF16), jax.ShapeDtypeStruct((1, D), F32)],
                          compiler_params=_cparams(("arbitrary",)), name=name)(dy, y, scale)


GPAD = 8


def _sigmoid(z):
    return 1.0 / (1.0 + jnp.exp(-z))


def _glu_fwd(u, cw, cb, name):
    _, S, F = u.shape
    tc = _tile(F, (256, 128))
    rc = _tile(S, (512,))

    def body(u_ref, w_ref, b_ref, o_ref, scr):
        zeros = jnp.zeros((GPAD, tc), F32)
        scr[pl.ds(0, GPAD), :] = zeros
        scr[pl.ds(GPAD + S, GPAD), :] = zeros
        for r0 in range(0, S, rc):
            scr[pl.ds(GPAD + r0, rc), :] = u_ref[0, pl.ds(r0, rc), :]
        w0, w1, w2, b = w_ref[0:1, :], w_ref[1:2, :], w_ref[2:3, :], b_ref[...]
        for r0 in range(0, S, rc):
            gc = (scr[pl.ds(GPAD + r0 - 1, rc), :] * w0 + scr[pl.ds(GPAD + r0, rc), :] * w1
                  + scr[pl.ds(GPAD + r0 + 1, rc), :] * w2 + b)
            o_ref[pl.ds(r0, rc), :] = (gc * _sigmoid(gc) * u_ref[1, pl.ds(r0, rc), :]).astype(BF16)

    return pl.pallas_call(
        body, grid=(F // tc,),
        in_specs=[pl.BlockSpec((2, S, tc), lambda j: (0, 0, j)), pl.BlockSpec((3, tc), lambda j: (0, j)),
                  pl.BlockSpec((1, tc), lambda j: (0, j))],
        out_specs=pl.BlockSpec((S, tc), lambda j: (0, j)), out_shape=jax.ShapeDtypeStruct((S, F), BF16),
        scratch_shapes=[pltpu.VMEM((S + 2 * GPAD, tc), F32)],
        compiler_params=_cparams(("parallel",)), name=name)(u, cw, cb)


def _glu_bwd(u, cw, cb, dact, name, comm=None):
    _, S, F = u.shape
    tc = _tile(F, (128,))
    rc = _tile(S, (512,))

    def body(u_ref, w_ref, b_ref, da_ref, du_ref, dw_ref, db_ref, scr_g, scr_d):
        zeros = jnp.zeros((GPAD, tc), F32)
        for scr in (scr_g, scr_d):
            scr[pl.ds(0, GPAD), :] = zeros
            scr[pl.ds(GPAD + S, GPAD), :] = zeros
        for r0 in range(0, S, rc):
            scr_g[pl.ds(GPAD + r0, rc), :] = u_ref[0, pl.ds(r0, rc), :]
        w0, w1, w2, b = w_ref[0:1, :], w_ref[1:2, :], w_ref[2:3, :], b_ref[...]
        sums = [jnp.zeros((1, tc), F32) for _ in range(4)]
        for r0 in range(0, S, rc):
            gp = scr_g[pl.ds(GPAD + r0 - 1, rc), :]
            g0 = scr_g[pl.ds(GPAD + r0, rc), :]
            gn = scr_g[pl.ds(GPAD + r0 + 1, rc), :]
            gc = gp * w0 + g0 * w1 + gn * w2 + b
            sig = _sigmoid(gc)
            da = da_ref[pl.ds(r0, rc), :]
            du_ref[1, pl.ds(r0, rc), :] = (da * (gc * sig)).astype(BF16)
            dgc = da * u_ref[1, pl.ds(r0, rc), :] * (sig * (1.0 + gc * (1.0 - sig)))
            scr_d[pl.ds(GPAD + r0, rc), :] = dgc
            for n, t in enumerate((dgc * gp, dgc * g0, dgc * gn, dgc)):
                sums[n] = sums[n] + jnp.sum(t, axis=0, keepdims=True)
        dw_ref[...] = jnp.concatenate(sums[:3], axis=0)
        db_ref[...] = sums[3]
        for r0 in range(0, S, rc):
            dg = (scr_d[pl.ds(GPAD + r0 + 1, rc), :] * w0 + scr_d[pl.ds(GPAD + r0, rc), :] * w1
                  + scr_d[pl.ds(GPAD + r0 - 1, rc), :] * w2)
            du_ref[0, pl.ds(r0, rc), :] = dg.astype(BF16)

    col = pl.BlockSpec((S, tc), lambda j: (0, j))
    return _call(
        body, (F // tc,), ("parallel",),
        [pl.BlockSpec((2, S, tc), lambda j: (0, 0, j)), pl.BlockSpec((3, tc), lambda j: (0, j)),
         pl.BlockSpec((1, tc), lambda j: (0, j)), col],
        [pl.BlockSpec((2, S, tc), lambda j: (0, 0, j)), pl.BlockSpec((3, tc), lambda j: (0, j)),
         pl.BlockSpec((1, tc), lambda j: (0, j))],
        [jax.ShapeDtypeStruct((2, S, F), BF16), jax.ShapeDtypeStruct((3, F), F32), jax.ShapeDtypeStruct((1, F), F32)],
        [pltpu.VMEM((S + 2 * GPAD, tc), F32), pltpu.VMEM((S + 2 * GPAD, tc), F32)], [u, cw, cb, dact], name, comm)


def _dot_nt(a, b):
    return lax.dot_general(a, b, (((1,), (1,)), ((), ())), preferred_element_type=F32)


def _dot_tn(a, b):
    return lax.dot_general(a, b, (((0,), (0,)), ((), ())), preferred_element_type=F32)


def _swa_specs(S, nq, G, hd, bq):
    prev = lambda j: jnp.maximum(j - 1, 0)
    nxt = lambda j: jnp.minimum(j + 1, nq - 1)
    kv = [pl.BlockSpec((None, bq, hd), functools.partial(lambda kh, j, f: (kh, f(j), 0), f=f))
          for f in (prev, lambda j: j, nxt)]
    pk = [pl.BlockSpec((1, bq), functools.partial(lambda kh, j, f: (0, f(j)), f=f)) for f in (prev, lambda j: j, nxt)]
    smem = pl.BlockSpec(memory_space=pltpu.SMEM)
    return ([pl.BlockSpec((bq, G * hd), lambda kh, j: (j, kh))] + kv + kv
            + [pl.BlockSpec((bq, 1), lambda kh, j: (j, 0))] + pk + [smem, smem])


def _swa_stack(x, G):
    w = x.shape[1] // G
    return jnp.concatenate([x[:, g * w:(g + 1) * w] for g in range(G)], axis=0)


def _swa_unstack(x, G):
    bq = x.shape[0] // G
    return jnp.concatenate([x[g * bq:(g + 1) * bq] for g in range(G)], axis=1)


def _swa_scores(j, kh, S, bq, G, qs, kspan, pq, pk, slopes):
    qi = j * bq + lax.broadcasted_iota(jnp.int32, (bq, 1), 0)
    ki = (j - 1) * bq + lax.broadcasted_iota(jnp.int32, (1, 3 * bq), 1)
    bias = jnp.where((jnp.abs(qi - ki) <= SWA_WINDOW) & (ki >= 0) & (ki < S), 0.0, NEG)
    dist = jnp.abs(pq - pk)
    raw = _dot_nt(qs, kspan)
    return [raw[g * bq:(g + 1) * bq] - slopes[kh * G + g] * dist + bias for g in range(G)]


def _swa_fwd_attn(qn, kT, vT, posq, posk, slopes, sinks, name, comm=None):
    Hkv, S, hd = kT.shape
    G = qn.shape[1] // (Hkv * hd)
    bq = SWA_WINDOW
    nq = S // bq

    def body(q_ref, k0, k1, k2, v0, v1, v2, pq, p0, p1, p2, slopes_ref, sinks_ref, o_ref, l_ref):
        kh, j = pl.program_id(0), pl.program_id(1)
        kspan = jnp.concatenate([k0[...], k1[...], k2[...]], axis=0)
        vspan = jnp.concatenate([v0[...], v1[...], v2[...]], axis=0)
        pk = jnp.concatenate([p0[...], p1[...], p2[...]], axis=1)
        ps, lses = [], []
        for g, s in enumerate(_swa_scores(j, kh, S, bq, G, _swa_stack(q_ref[...], G), kspan, pq[...], pk, slopes_ref)):
            sink = sinks_ref[kh * G + g]
            m = jnp.maximum(jnp.max(s, axis=-1, keepdims=True), sink)
            p = jnp.exp(s - m)
            den = jnp.sum(p, axis=-1, keepdims=True) + jnp.exp(sink - m)
            ps.append((p / den).astype(BF16))
            lses.append(m + jnp.log(den))
        o_ref[...] = _swa_unstack(jnp.dot(jnp.concatenate(ps, axis=0), vspan, preferred_element_type=F32), G)
        l_ref[...] = jnp.concatenate(lses, axis=1)

    return _call(
        body, (Hkv, nq), ("parallel", "parallel"), _swa_specs(S, nq, G, hd, bq),
        [pl.BlockSpec((bq, G * hd), lambda kh, j: (j, kh)), pl.BlockSpec((None, bq, G), lambda kh, j: (kh, j, 0))],
        [jax.ShapeDtypeStruct(qn.shape, F32), jax.ShapeDtypeStruct((Hkv, S, G), F32)], [],
        [qn, kT, kT, kT, vT, vT, vT, posq, posk, posk, posk, slopes, sinks], name, comm)


def _swa_bwd_attn(qn, kT, vT, posq, posk, slopes, sinks, o, lse, do, name, comm=None):
    Hkv, S, hd = kT.shape
    G = qn.shape[1] // (Hkv * hd)
    bq = SWA_WINDOW
    nq = S // bq

    def body(q_ref, k0, k1, k2, v0, v1, v2, pq, p0, p1, p2, slopes_ref, sinks_ref, o_ref, l_ref, do_ref,
             dq_ref, dk_ref, dv_ref, ds_ref):
        kh, j = pl.program_id(0), pl.program_id(1)
        kspan = jnp.concatenate([k0[...], k1[...], k2[...]], axis=0)
        vspan = jnp.concatenate([v0[...], v1[...], v2[...]], axis=0)
        pk = jnp.concatenate([p0[...], p1[...], p2[...]], axis=1)
        qs = _swa_stack(q_ref[...], G)
        dos = _swa_stack(do_ref[...], G)
        delta = jnp.sum(dos * _swa_stack(o_ref[...], G), axis=-1, keepdims=True)
        dos = dos.astype(BF16)
        dp = _dot_nt(dos, vspan)
        ps, dss, dsinks = [], [], []
        for g, s in enumerate(_swa_scores(j, kh, S, bq, G, qs, kspan, pq[...], pk, slopes_ref)):
            rows = slice(g * bq, (g + 1) * bq)
            lg = l_ref[:, g:g + 1]
            p = jnp.exp(s - lg)
            ps.append(p.astype(BF16))
            dss.append((p * (dp[rows] - delta[rows])).astype(BF16))
            dsinks.append(jnp.sum(-jnp.exp(sinks_ref[kh * G + g] - lg) * delta[rows], axis=0, keepdims=True))
        dsc = jnp.concatenate(dss, axis=0)
        dq_ref[...] = _swa_unstack(jnp.dot(dsc, kspan, preferred_element_type=F32), G)
        dk_ref[...] = _dot_tn(dsc, qs)
        dv_ref[...] = _dot_tn(jnp.concatenate(ps, axis=0), dos)

        @pl.when(j == 0)
        def _():
            ds_ref[...] = jnp.zeros_like(ds_ref)

        ds_ref[...] += jnp.concatenate(dsinks, axis=1)

    qblk = pl.BlockSpec((bq, G * hd), lambda kh, j: (j, kh))
    span = pl.BlockSpec((None, None, 3 * bq, hd), lambda kh, j: (kh, j, 0, 0))
    return _call(
        body, (Hkv, nq), ("parallel", "arbitrary"),
        _swa_specs(S, nq, G, hd, bq) + [qblk, pl.BlockSpec((None, bq, G), lambda kh, j: (kh, j, 0)), qblk],
        [qblk, span, span, pl.BlockSpec((None, 1, G), lambda kh, j: (kh, 0, 0))],
        [jax.ShapeDtypeStruct(qn.shape, F32), jax.ShapeDtypeStruct((Hkv, nq, 3 * bq, hd), F32),
         jax.ShapeDtypeStruct((Hkv, nq, 3 * bq, hd), F32), jax.ShapeDtypeStruct((Hkv, 1, G), F32)], [],
        [qn, kT, kT, kT, vT, vT, vT, posq, posk, posk, posk, slopes, sinks, o, lse, do], name, comm)


def _overlap3(spans, name):
    Hkv, nq, bq3, hd = spans.shape
    bq = bq3 // 3
    sp = spans.reshape(Hkv, nq, 3, bq, hd)

    def body(a_ref, b_ref, c_ref, o_ref):
        b = pl.program_id(1)
        acc = b_ref[...]
        acc = acc + jnp.where(b > 0, a_ref[...], 0.0)
        acc = acc + jnp.where(b < nq - 1, c_ref[...], 0.0)
        o_ref[...] = acc

    def part(f, slot):
        return pl.BlockSpec((None, None, None, bq, hd), lambda kh, b: (kh, f(b), slot, 0, 0))

    return pl.pallas_call(
        body, grid=(Hkv, nq),
        in_specs=[part(lambda b: jnp.maximum(b - 1, 0), 2), part(lambda b: b, 1), part(lambda b: jnp.minimum(b + 1, nq - 1), 0)],
        out_specs=pl.BlockSpec((None, bq, hd), lambda kh, b: (kh, b, 0)),
        out_shape=jax.ShapeDtypeStruct((Hkv, nq * bq, hd), F32),
        compiler_params=_cparams(("parallel", "parallel")), name=name)(sp, sp, sp)


def _mla_fwd_attn(q, k, vT, name, comm=None):
    H, S, dk = q.shape
    dv = vT.shape[1]
    tq = _tile(S, (1024, 512, 256, 128))
    tk = _tile(S, (1024, 512, 256, 128))
    cq = _tile(tq, (1024, 512, 256, 128))
    nk = S // tk

    def body(q_ref, k_ref, v_ref, o_ref, l_ref, m_s, l_s, acc):
        kk = pl.program_id(2)

        @pl.when(kk == 0)
        def _():
            m_s[...] = jnp.full_like(m_s, NEG)
            l_s[...] = jnp.zeros_like(l_s)
            acc[...] = jnp.zeros_like(acc)

        kb, vb = k_ref[...], v_ref[...]
        for c0 in range(0, tq, cq):
            cols = pl.ds(c0, cq)
            sT = _dot_nt(kb, q_ref[cols, :])
            m_old = m_s[:, cols]
            m_new = jnp.maximum(m_old, jnp.max(sT, axis=0, keepdims=True))
            a = jnp.exp(m_old - m_new)
            p = jnp.exp(sT - m_new)
            l_s[:, cols] = a * l_s[:, cols] + jnp.sum(p, axis=0, keepdims=True)
            acc[:, cols] = a * acc[:, cols] + jnp.dot(vb, p.astype(BF16), preferred_element_type=F32)
            m_s[:, cols] = m_new

        @pl.when(kk == nk - 1)
        def _():
            o_ref[...] = acc[...] / l_s[...]
            l_ref[...] = m_s[...] + jnp.log(l_s[...])

    return _call(
        body, (H, S // tq, nk), ("parallel", "parallel", "arbitrary"),
        [pl.BlockSpec((None, tq, dk), lambda h, i, kk: (h, i, 0)),
         pl.BlockSpec((None, tk, dk), lambda h, i, kk: (h, kk, 0)),
         pl.BlockSpec((None, dv, tk), lambda h, i, kk: (h, 0, kk))],
        [pl.BlockSpec((None, dv, tq), lambda h, i, kk: (h, 0, i)), pl.BlockSpec((None, 1, tq), lambda h, i, kk: (h, 0, i))],
        [jax.ShapeDtypeStruct((H, dv, S), F32), jax.ShapeDtypeStruct((H, 1, S), F32)],
        [pltpu.VMEM((1, tq), F32), pltpu.VMEM((1, tq), F32), pltpu.VMEM((dv, tq), F32)], [q, k, vT], name, comm)


def _mla_bwd_attn(q, k, kT, v, oT, doT, lse, name):
    H, S, dk = q.shape
    dv = v.shape[2]
    tq = _tile(S, (1024, 512, 256, 128))
    tk = _tile(S, (1024, 512, 256, 128))
    nq = S // tq

    def body(q_ref, k_ref, kT_ref, v_ref, o_ref, do_ref, l_ref, dq_ref, dk_ref, dv_ref, dk_acc, dv_acc):
        j, i = pl.program_id(1), pl.program_id(2)

        @pl.when(i == 0)
        def _():
            dk_acc[...] = jnp.zeros_like(dk_acc)
            dv_acc[...] = jnp.zeros_like(dv_acc)

        @pl.when((i == 0) & (j == 0))
        def _():
            dq_ref[...] = jnp.zeros_like(dq_ref)

        qb, dob = q_ref[...], do_ref[...]
        delta = jnp.sum(dob * o_ref[...], axis=0, keepdims=True)
        dob = dob.astype(BF16)
        pT = jnp.exp(_dot_nt(k_ref[...], qb) - l_ref[...])
        dpT = jnp.dot(v_ref[...], dob, preferred_element_type=F32)
        dsT = (pT * (dpT - delta)).astype(BF16)
        dk_acc[...] += jnp.dot(dsT, qb, preferred_element_type=F32)
        dv_acc[...] += _dot_nt(pT.astype(BF16), dob)
        cols = pl.ds(pl.multiple_of(i * tq, tq), tq)
        dq_ref[:, cols] += jnp.dot(kT_ref[...], dsT, preferred_element_type=F32)

        @pl.when(i == nq - 1)
        def _():
            dk_ref[...] = dk_acc[...]
            dv_ref[...] = dv_acc[...]

    qrow = lambda d: pl.BlockSpec((None, tq, d), lambda h, j, i: (h, i, 0))
    krow = lambda d: pl.BlockSpec((None, tk, d), lambda h, j, i: (h, j, 0))
    qcol = lambda d: pl.BlockSpec((None, d, tq), lambda h, j, i: (h, 0, i))
    return pl.pallas_call(
        body, grid=(H, S // tk, nq),
        in_specs=[qrow(dk), krow(dk), pl.BlockSpec((None, dk, tk), lambda h, j, i: (h, 0, j)), krow(dv), qcol(dv), qcol(dv),
                  qcol(1)],
        out_specs=[pl.BlockSpec((None, dk, S), lambda h, j, i: (h, 0, 0)), krow(dk), krow(dv)],
        out_shape=[jax.ShapeDtypeStruct((H, dk, S), F32), jax.ShapeDtypeStruct((H, S, dk), F32),
                   jax.ShapeDtypeStruct((H, S, dv), F32)],
        scratch_shapes=[pltpu.VMEM((tk, dk), F32), pltpu.VMEM((tk, dv), F32)],
        compiler_params=_cparams(("parallel", "arbitrary", "arbitrary")), name=name)(q, k, kT, v, oT, doT, lse)


def _sum_heads(x, name):
    H, S, d = x.shape
    ts = _tile(S, (1024, 512, 256, 128))

    def body(x_ref, o_ref):
        acc = x_ref[0]
        for h in range(1, H):
            acc = acc + x_ref[h]
        o_ref[...] = acc

    return pl.pallas_call(body, grid=(S // ts,), in_specs=[pl.BlockSpec((H, ts, d), lambda i: (0, i, 0))],
                          out_specs=pl.BlockSpec((ts, d), lambda i: (i, 0)), out_shape=jax.ShapeDtypeStruct((S, d), F32),
                          compiler_params=_cparams(("parallel",)), name=name)(x)


def _loss_head(y, target, name):
    S, D = y.shape
    tr = _row_tile(S, D)
    row = pl.BlockSpec((tr, D), lambda i: (i, 0))
    vec = pl.BlockSpec((1, D), lambda i: (0, 0))

    def body(y_ref, t_ref, sq_ref, dy_ref):
        @pl.when(pl.program_id(0) == 0)
        def _():
            sq_ref[...] = jnp.zeros_like(sq_ref)

        e = y_ref[...] - t_ref[...]
        sq_ref[...] += jnp.sum(e * e, axis=0, keepdims=True)
        dy_ref[...] = e / D

    return pl.pallas_call(body, grid=(S // tr,), in_specs=[row, row], out_specs=[vec, row],
                          out_shape=[jax.ShapeDtypeStruct((1, D), F32), jax.ShapeDtypeStruct((S, D), F32)],
                          compiler_params=_cparams(("arbitrary",)), name=name)(y, target)


def _sum_chips(recv, name):
    _, R, C = recv.shape
    tr = _row_tile(R, C)

    def body(r_ref, o_ref):
        acc = r_ref[0].astype(F32)
        for i in range(1, N_CHIPS):
            acc = acc + r_ref[i].astype(F32)
        o_ref[...] = acc

    return pl.pallas_call(body, grid=(R // tr,), in_specs=[pl.BlockSpec((N_CHIPS, tr, C), lambda i: (0, i, 0))],
                          out_specs=pl.BlockSpec((tr, C), lambda i: (i, 0)), out_shape=jax.ShapeDtypeStruct((R, C), F32),
                          compiler_params=_cparams(("parallel",)), name=name)(recv)


def _adamw(w, m, v, gs, name, comm=None):
    R, C = w.shape
    tr = _row_tile(R, 2 * C)
    row = pl.BlockSpec((tr, C), lambda i: (i, 0))
    n = len(gs)

    def body(*refs):
        w_ref, m_ref, v_ref = refs[:3]
        g_ref, d_ref, nm_ref, nv_ref = refs[3 + n:]
        g = refs[3][...]
        for r in refs[4:3 + n]:
            g = g + r[...]
        mm = ADAM_B1 * m_ref[...] + (1.0 - ADAM_B1) * g
        vv = ADAM_B2 * v_ref[...] + (1.0 - ADAM_B2) * (g * g)
        m_hat = mm / (1.0 - ADAM_B1 ** ADAM_STEP)
        v_hat = vv / (1.0 - ADAM_B2 ** ADAM_STEP)
        g_ref[...] = g
        d_ref[...] = -ADAM_LR * (m_hat / (jnp.sqrt(v_hat) + ADAM_EPS) + ADAM_WD * w_ref[...])
        nm_ref[...] = mm
        nv_ref[...] = vv

    sds = jax.ShapeDtypeStruct((R, C), F32)
    return _call(body, (R // tr,), ("parallel",), [row] * (3 + n), [row] * 4, [sds] * 4, [], [w, m, v, *gs], name, comm)


def _chip_peers():
    x, y, c = lax.axis_index("x"), lax.axis_index("y"), lax.axis_index("c")
    others = [(1 - x, y), (x, 1 - y), (1 - x, 1 - y)]
    return x, y, c, 2 * x + y, [(px, py, 2 * px + py) for px, py in others]


HALVE_MIN_BYTES = 1 << 20


def _gather_comm(srcs):
    n = len(srcs)
    shapes = [tuple(a.shape[1:] if l is not None else a.shape) for a, l in srcs]
    halved = [s[0] % 2 == 0 and int(np.prod(s)) * a.dtype.itemsize >= HALVE_MIN_BYTES for s, (a, _) in zip(shapes, srcs)]
    outs = [jax.ShapeDtypeStruct((N_CHIPS,) + s, a.dtype) for s, (a, _) in zip(shapes, srcs)]
    dma = pltpu.SemaphoreType.DMA
    sems = [dma((3 * n,)), dma((3 * n,)), dma((3 * n,)), dma((3 * n,)), dma((n,))]

    def rows(ref, t, half):
        if not halved[t]:
            return ref
        h = shapes[t][0] // 2
        return ref.at[pl.ds(half * h, h)]

    def copies(in_refs, out_refs, sem_refs):
        send, recv, fsend, frecv, lsem = sem_refs
        x, y, c, me, peers = _chip_peers()
        src = [in_refs[t] if l is None else in_refs[t].at[l] for t, (_, l) in enumerate(srcs)]
        local = [pltpu.make_async_copy(src[t], out_refs[t].at[me], lsem.at[t]) for t in range(n)]

        def ici(t, j, origin):
            px, py, _ = peers[j]
            return pltpu.make_async_remote_copy(src_ref=rows(src[t], t, c), dst_ref=rows(out_refs[t].at[origin], t, c),
                                                send_sem=send.at[3 * t + j], recv_sem=recv.at[3 * t + j],
                                                device_id=(px, py, c), device_id_type=MESH)

        def hand(t, j, half):
            blk = rows(out_refs[t].at[peers[j][2]], t, half)
            return pltpu.make_async_remote_copy(src_ref=blk, dst_ref=blk, send_sem=fsend.at[3 * t + j],
                                                recv_sem=frecv.at[3 * t + j], device_id=(x, y, 1 - c), device_id_type=MESH)

        return c, me, peers, local, ici, hand

    def start(in_refs, out_refs, sem_refs):
        c, me, peers, local, ici, hand = copies(in_refs, out_refs, sem_refs)
        for t in range(n):
            local[t].start()
            for j in range(3):
                ici(t, j, me).start()

    def finish(in_refs, out_refs, sem_refs):
        c, me, peers, local, ici, hand = copies(in_refs, out_refs, sem_refs)
        for t in range(n):
            for j in range(3):
                ici(t, j, peers[j][2]).wait_recv()
                if halved[t]:
                    hand(t, j, c).start()
        for t in range(n):
            for j in range(3):
                if halved[t]:
                    hand(t, j, 1 - c).wait_recv()
        for t in range(n):
            for j in range(3):
                ici(t, j, me).wait_send()
                if halved[t]:
                    hand(t, j, c).wait_send()
            local[t].wait()

    return _Comm([a for a, _ in srcs], outs, {}, sems, start, finish)


def _scatter_comm(items, bufs):
    n = len(items)
    dma = pltpu.SemaphoreType.DMA
    sems = [dma((3 * n,)), dma((3 * n,)), dma((n,))]

    def piece(ref, rows):
        return ref if rows is None else ref.at[pl.ds(rows[0], rows[1])]

    def copies(in_refs, out_refs, sem_refs):
        send, recv, lsem = sem_refs
        x, y, c, me, peers = _chip_peers()

        def local(t):
            _, bi, l, rows, cut = items[t]
            return pltpu.make_async_copy(piece(in_refs[t].at[me], rows if cut else None),
                                         piece(out_refs[bi].at[me, l], rows), lsem.at[t])

        def ici(t, j, origin):
            _, bi, l, rows, cut = items[t]
            px, py, pi = peers[j]
            return pltpu.make_async_remote_copy(src_ref=piece(in_refs[t].at[pi], rows if cut else None),
                                                dst_ref=piece(out_refs[bi].at[origin, l], rows),
                                                send_sem=send.at[3 * t + j], recv_sem=recv.at[3 * t + j],
                                                device_id=(px, py, c), device_id_type=MESH)

        return me, peers, local, ici

    def start(in_refs, out_refs, sem_refs):
        me, peers, local, ici = copies(in_refs, out_refs, sem_refs)
        for t in range(n):
            local(t).start()
            for j in range(3):
                ici(t, j, me).start()

    def finish(in_refs, out_refs, sem_refs):
        me, peers, local, ici = copies(in_refs, out_refs, sem_refs)
        for t in range(n):
            for j in range(3):
                ici(t, j, peers[j][2]).wait_recv()
        for t in range(n):
            for j in range(3):
                ici(t, j, me).wait_send()
            local(t).wait()

    return _Comm([it[0] for it in items] + list(bufs), [jax.ShapeDtypeStruct(b.shape, b.dtype) for b in bufs],
                 {n + k: k for k in range(len(bufs))}, sems, start, finish)


def _run_comm(comm, name):
    ni, no = len(comm.ins), len(comm.outs)

    def body(*refs):
        comm.start(refs[:ni], refs[ni:ni + no], refs[ni + no:])
        comm.finish(refs[:ni], refs[ni:ni + no], refs[ni + no:])

    hbm = pl.BlockSpec(memory_space=pl.ANY)
    return pl.pallas_call(body, in_specs=[hbm] * ni, out_specs=[hbm] * no, out_shape=list(comm.outs),
                          scratch_shapes=list(comm.sems), input_output_aliases=dict(comm.aliases), name=name)(*comm.ins)


def _swap_comm(arrs):
    n = len(arrs)

    def copies(in_refs, out_refs, sem_refs):
        x, y, c = lax.axis_index("x"), lax.axis_index("y"), lax.axis_index("c")
        return [pltpu.make_async_remote_copy(src_ref=in_refs[t], dst_ref=out_refs[t], send_sem=sem_refs[0].at[t],
                                             recv_sem=sem_refs[1].at[t], device_id=(x, y, 1 - c), device_id_type=MESH)
                for t in range(n)]

    def start(in_refs, out_refs, sem_refs):
        for cp in copies(in_refs, out_refs, sem_refs):
            cp.start()

    def finish(in_refs, out_refs, sem_refs):
        for cp in copies(in_refs, out_refs, sem_refs):
            cp.wait()

    return _Comm(list(arrs), [jax.ShapeDtypeStruct(a.shape, a.dtype) for a in arrs], {},
                 [pltpu.SemaphoreType.DMA((n,)), pltpu.SemaphoreType.DMA((n,))], start, finish)


def _allreduce_small(buf, name):
    R, C = buf.shape

    def body(b_ref, o_ref, recv, send_sems, recv_sems):
        x, y, c = lax.axis_index("x"), lax.axis_index("y"), lax.axis_index("c")
        me = 4 * x + 2 * y + c
        recv[me] = b_ref[...]
        cps = []
        for mask in range(1, N_DEV):
            px, py, pc = x ^ (mask >> 2 & 1), y ^ (mask >> 1 & 1), c ^ (mask & 1)
            cp = pltpu.make_async_remote_copy(src_ref=b_ref, dst_ref=recv.at[me], send_sem=send_sems.at[mask - 1],
                                              recv_sem=recv_sems.at[mask - 1], device_id=(px, py, pc), device_id_type=MESH)
            cp.start()
            cps.append((cp, 4 * px + 2 * py + pc, (px, py, pc), mask))
        for cp, pi, dev, mask in cps:
            pltpu.make_async_remote_copy(src_ref=b_ref, dst_ref=recv.at[pi], send_sem=send_sems.at[mask - 1],
                                         recv_sem=recv_sems.at[mask - 1], device_id=dev, device_id_type=MESH).wait_recv()
        for cp, _, _, _ in cps:
            cp.wait_send()
        acc = recv[0]
        for i in range(1, N_DEV):
            acc = acc + recv[i]
        o_ref[...] = acc

    vm = pl.BlockSpec(memory_space=pltpu.VMEM)
    return pl.pallas_call(
        body, in_specs=[vm], out_specs=vm, out_shape=jax.ShapeDtypeStruct((R, C), F32),
        scratch_shapes=[pltpu.VMEM((N_DEV, R, C), F32), pltpu.SemaphoreType.DMA((N_DEV - 1,)),
                        pltpu.SemaphoreType.DMA((N_DEV - 1,))],
        compiler_params=pltpu.CompilerParams(vmem_limit_bytes=VMEM_LIMIT), name=name)(buf)


def _pack(arrs):
    parts, offs, n = [], [], 0
    for a in arrs:
        f = a.reshape(-1).astype(F32)
        k = -(-f.shape[0] // LANES) * LANES
        parts.append(jnp.pad(f, (0, k - f.shape[0])))
        offs.append(n)
        n += k
    total = -(-n // (8 * LANES)) * (8 * LANES)
    if total > n:
        parts.append(jnp.zeros((total - n,), F32))
    return jnp.concatenate(parts).reshape(-1, LANES), offs


def _unpack(buf, offs, shapes):
    flat = buf.reshape(-1)
    return [flat[o:o + int(np.prod(s))].reshape(s) for o, s in zip(offs, shapes)]


def _ride(call, riders, name, made=None):
    rider = riders.get(name)
    if rider is None:
        return call(name, None)
    build, done = rider if isinstance(rider, tuple) else rider(made)
    out, res = call(name, build())
    done(res)
    return out


def _ffn_fwd(xm, g, w_in, cw, cb, w_out, t, riders):
    h2 = _rms_fwd(xm, g, BF16, f"ffn_norm{t}")
    u = _ride(lambda n, c: _mm_nn(h2, w_in, F32, n, out_panels=2, comm=c), riders, f"ffn_in{t}")
    act = _glu_fwd(u, cw, cb, f"glu{t}")
    w_out = w_out()
    xo = _ride(lambda n, c: _mm_nn(act, w_out, F32, n, res=xm, comm=c), riders, f"ffn_out{t}")
    return xo, (h2, u, act)


def _ffn_bwd(dxo, xm, g, w_in, cw, cb, w_out, saved, t, riders, split_dwin=False):
    h2, u, act = saved
    w_out = w_out()
    made = {}
    made['ffn_w_out'] =_ride(lambda n, c: _mm_tn(act, dxo, BF16, n, comm=c), riders, f"ffn_dwout{t}", made)
    dact = _ride(lambda n, c: _mm_nt(dxo, w_out, F32, n, comm=c), riders, f"ffn_dact{t}", made)
    du, dcw, dcb = _ride(lambda n, c: _glu_bwd(u, cw, cb, dact, n, comm=c), riders, f"glu_bwd{t}", made)
    dh2 = _ride(lambda n, c: _mm_nt(du, w_in, F32, n, comm=c), riders, f"ffn_dh{t}", made)
    if split_dwin:
        half = h2.shape[1] // 2
        made['ffn_w_in_a'] = _ride(lambda n, c: _mm_tn(h2[:, :half], du, BF16, n, out_panels=N_CHIPS, comm=c),
                                   riders, f"ffn_dwin{t}a", made)
        dw_in = (made['ffn_w_in_a'], _ride(lambda n, c: _mm_tn(h2[:, half:], du, BF16, n, out_panels=N_CHIPS, comm=c),
                                           riders, f"ffn_dwin{t}b", made))
    else:
        dw_in = _ride(lambda n, c: _mm_tn(h2, du, BF16, n, out_panels=N_CHIPS, comm=c), riders, f"ffn_dwin{t}", made)
    dxm, dg = _rms_bwd(xm, g, dh2, f"ffn_norm_bwd{t}", res=dxo)
    return dxm, dict(ffn_w_in=dw_in, ffn_conv_w=dcw, ffn_conv_b=dcb, ffn_w_out=made['ffn_w_out'], norm_ffn_g=dg)


def _pool_fwd(x, g, wg, scale, t, riders):
    h = _rms_fwd(x, g, F32, f"pool_norm{t}")
    pooled = _pool_win(h, True, BF16, f"pool_win{t}")
    yraw = _group_mm("nn", pooled, wg, F32, f"pool_mm{t}")
    xm = _scale_res(x, yraw, scale, f"pool_out{t}")
    return xm, (pooled, yraw)


def _pool_bwd(dxm, x, g, wg, scale, saved, t, riders):
    pooled, yraw = saved
    dyraw, dscale = _scale_bwd(dxm, yraw, scale, f"pool_out_bwd{t}")
    dpool = _group_mm("nt", dyraw, wg, F32, f"pool_dp{t}")
    dw = _group_mm("tn", pooled, dyraw, F32, f"pool_dw{t}").astype(BF16)
    dh = _pool_win(dpool, False, F32, f"pool_win_bwd{t}")
    dx, dgn = _rms_bwd(x, g, dh, f"pool_norm_bwd{t}", res=dxm)
    return dx, dict(pool_w=dw, pool_scale=dscale, norm_mix_g=dgn)


def _swa_fwd(x, g, w_qkv, q_gain, k_gain, sinks, w_o, pos, t, riders):
    S = x.shape[0]
    Hq, Hkv, hd = SWA_HEADS, SWA_KV_HEADS, SWA_HEAD_DIM
    nq, nkv = Hq * hd, Hkv * hd
    posq, posk, slopes = pos
    h = _rms_fwd(x, g, BF16, f"swa_norm{t}")
    qkv = _mm_nn(h, w_qkv, F32, f"swa_qkv{t}")
    q = qkv[:, :nq].reshape(S * Hq, hd)
    k = qkv[:, nq:nq + nkv].reshape(S * Hkv, hd)
    qn = _rms_fwd(q, q_gain, BF16, f"swa_qnorm{t}", scale=hd ** -0.5).reshape(S, nq)
    kT = _rms_fwd(k, k_gain, BF16, f"swa_knorm{t}").reshape(S, Hkv, hd).transpose(1, 0, 2)
    vT = qkv[:, nq + nkv:].astype(BF16).reshape(S, Hkv, hd).transpose(1, 0, 2)
    o, lse = _ride(lambda n, c: _swa_fwd_attn(qn, kT, vT, posq, posk, slopes, sinks.reshape(-1), n, comm=c), riders,
                   f"swa_attn{t}")
    xm = _mm_nn(o, w_o, F32, f"swa_o{t}", res=x)
    return xm, (h, q, k, qn, kT, vT, o, lse)


def _swa_bwd(dxm, x, g, w_qkv, q_gain, k_gain, sinks, w_o, pos, saved, t, riders):
    S = x.shape[0]
    Hq, Hkv, hd = SWA_HEADS, SWA_KV_HEADS, SWA_HEAD_DIM
    nq, nkv = Hq * hd, Hkv * hd
    posq, posk, slopes = pos
    h, q, k, qn, kT, vT, o, lse = saved
    do = _mm_nt(dxm, w_o, F32, f"swa_do{t}")
    dw_o = _mm_tn(o, dxm, BF16, f"swa_dwo{t}")
    dqn, dkp, dvp, dsink = _ride(
        lambda n, c: _swa_bwd_attn(qn, kT, vT, posq, posk, slopes, sinks.reshape(-1), o, lse, do, n, comm=c), riders,
        f"swa_attn_bwd{t}")
    dkn = _overlap3(dkp, f"swa_dk{t}").transpose(1, 0, 2).reshape(S * Hkv, hd)
    dv = _overlap3(dvp, f"swa_dv{t}").transpose(1, 0, 2).reshape(S, nkv)
    dq, dqg = _rms_bwd(q, q_gain, dqn.reshape(S * Hq, hd), f"swa_qnorm_bwd{t}", scale=hd ** -0.5)
    dk, dkg = _rms_bwd(k, k_gain, dkn, f"swa_knorm_bwd{t}")
    dqkv = jnp.concatenate([dq.reshape(S, nq), dk.reshape(S, nkv), dv], axis=1)
    dh = _mm_nt(dqkv, w_qkv, F32, f"swa_dh{t}")
    dw_qkv = _mm_tn(h, dqkv, BF16, f"swa_dwqkv{t}", out_panels=N_CHIPS)
    dx, dgn = _rms_bwd(x, g, dh, f"swa_norm_bwd{t}", res=dxm)
    return dx, dict(swa_w_qkv=dw_qkv, swa_q_gain=dqg, swa_k_gain=dkg, swa_sinks=dsink.reshape(1, Hq), swa_w_o=dw_o,
                    norm_mix_g=dgn)


def _mla_fwd(x, g, w_down, q_a_gain, kv_a_gain, w_uq, w_ukv, qn_gain, qr_gain, kn_gain, kr_gain, w_o, rope, t, riders):
    S = x.shape[0]
    H, dn, R, dv, qr_, kvr = MLA_HEADS, MLA_NOPE, MLA_ROPE, MLA_V, MLA_Q_RANK, MLA_KV_RANK
    sc = (dn + R) ** -0.5
    h = _rms_fwd(x, g, BF16, f"mla_norm{t}")
    d = _mm_nn(h, w_down, F32, f"mla_down{t}")
    cq_pre, ckv_pre, kpe_pre = d[:, :qr_], d[:, qr_:qr_ + kvr], d[:, qr_ + kvr:]
    cq = _rms_fwd(cq_pre, q_a_gain, BF16, f"mla_cq{t}")
    ckv = _rms_fwd(ckv_pre, kv_a_gain, BF16, f"mla_ckv{t}")
    q3 = _mm_nn(cq, w_uq, F32, f"mla_uq{t}").reshape(S, H, dn + R).transpose(1, 0, 2)
    kv3 = _mm_nn(ckv, w_ukv, F32, f"mla_ukv{t}").reshape(S, H, dn + dv).transpose(1, 0, 2)
    qn_in, qp_in = q3[..., :dn].reshape(H * S, dn), q3[..., dn:].reshape(H * S, R)
    kn_in, v = kv3[..., :dn].reshape(H * S, dn), kv3[..., dn:].astype(BF16)
    qn = _rms_fwd(qn_in, qn_gain, BF16, f"mla_qn{t}", scale=sc).reshape(H, S, dn)
    qp = _rms_fwd(qp_in, qr_gain, BF16, f"mla_qp{t}", scale=sc, rope=rope).reshape(H, S, R)
    kn = _rms_fwd(kn_in, kn_gain, BF16, f"mla_kn{t}").reshape(H, S, dn)
    kp = _rms_fwd(kpe_pre, kr_gain, BF16, f"mla_kp{t}", rope=rope)
    qf = jnp.concatenate([qn, qp], axis=-1)
    kf = jnp.concatenate([kn, jnp.broadcast_to(kp[None], (H, S, R))], axis=-1)
    vT = v.transpose(0, 2, 1)
    oT, lse = _ride(lambda n, c: _mla_fwd_attn(qf, kf, vT, n, comm=c), riders, f"mla_attn{t}")
    oT2 = oT.reshape(H * dv, S)
    xm = _mm_tn(oT2, w_o, F32, f"mla_o{t}", res=x)
    return xm, (h, cq_pre, ckv_pre, kpe_pre, cq, ckv, qn_in, qp_in, kn_in, v, qf, kf, oT, lse)


def _mla_bwd(dxm, x, g, w_down, q_a_gain, kv_a_gain, w_uq, w_ukv, qn_gain, qr_gain, kn_gain, kr_gain, w_o, rope,
             saved, t, riders):
    S = x.shape[0]
    H, dn, R, dv = MLA_HEADS, MLA_NOPE, MLA_ROPE, MLA_V
    sc = (dn + R) ** -0.5
    h, cq_pre, ckv_pre, kpe_pre, cq, ckv, qn_in, qp_in, kn_in, v, qf, kf, oT, lse = saved
    oT2 = oT.reshape(H * dv, S)
    doT = _mm_nt(w_o, dxm, F32, f"mla_do{t}").reshape(H, dv, S)
    dw_o = _mm_nn(oT2, dxm, BF16, f"mla_dwo{t}")
    dqT, dkf, dvv = _mla_bwd_attn(qf, kf, kf.transpose(0, 2, 1), v, oT, doT, lse, f"mla_attn_bwd{t}")
    dqf = dqT.transpose(0, 2, 1)
    dqn, dqp, dkn = dqf[..., :dn], dqf[..., dn:], dkf[..., :dn]
    dkp = _sum_heads(dkf[..., dn:], f"mla_dkp{t}")
    dqn_in, dqng = _rms_bwd(qn_in, qn_gain, dqn.reshape(H * S, dn), f"mla_qn_bwd{t}", scale=sc)
    dqp_in, dqrg = _rms_bwd(qp_in, qr_gain, dqp.reshape(H * S, R), f"mla_qp_bwd{t}", scale=sc, rope=rope)
    dkn_in, dkng = _rms_bwd(kn_in, kn_gain, dkn.reshape(H * S, dn), f"mla_kn_bwd{t}")
    dkpe_pre, dkrg = _rms_bwd(kpe_pre, kr_gain, dkp, f"mla_kp_bwd{t}", rope=rope)
    dq = jnp.concatenate([dqn_in.reshape(H, S, dn), dqp_in.reshape(H, S, R)], axis=-1).transpose(1, 0, 2).reshape(S, -1)
    dkv = jnp.concatenate([dkn_in.reshape(H, S, dn), dvv], axis=-1).transpose(1, 0, 2).reshape(S, -1)
    dcq = _mm_nt(dq, w_uq, F32, f"mla_dcq{t}")
    dw_uq = _mm_tn(cq, dq, BF16, f"mla_dwuq{t}", out_panels=N_CHIPS)
    dckv = _mm_nt(dkv, w_ukv, F32, f"mla_dckv{t}")
    dw_ukv = _mm_tn(ckv, dkv, BF16, f"mla_dwukv{t}", out_panels=N_CHIPS)
    dcq_pre, dqag = _rms_bwd(cq_pre, q_a_gain, dcq, f"mla_cq_bwd{t}")
    dckv_pre, dkvag = _rms_bwd(ckv_pre, kv_a_gain, dckv, f"mla_ckv_bwd{t}")
    dd = jnp.concatenate([dcq_pre, dckv_pre, dkpe_pre], axis=1)
    dh = _mm_nt(dd, w_down, F32, f"mla_dh{t}")
    dw_down = _mm_tn(h, dd, BF16, f"mla_dwdown{t}")
    dx, dgn = _rms_bwd(x, g, dh, f"mla_norm_bwd{t}", res=dxm)
    return dx, dict(mla_w_down=dw_down, mla_q_a_gain=dqag, mla_kv_a_gain=dkvag, mla_w_uq=dw_uq, mla_w_ukv=dw_ukv,
                    mla_qn_gain=dqng, mla_qr_gain=dqrg, mla_kn_gain=dkng, mla_kr_gain=dkrg, mla_w_o=dw_o, norm_mix_g=dgn)


def _chips_to_axis(gathered, axis):
    moved = jnp.moveaxis(gathered, 0, axis)
    shape = list(moved.shape)
    shape[axis:axis + 2] = [shape[axis] * shape[axis + 1]]
    return moved.reshape(shape)


def kernel(x, positions, norm_mix_g, norm_ffn_g, pool_w, pool_scale, swa_w_qkv, swa_q_gain, swa_k_gain, swa_sinks, swa_w_o, mla_w_down, mla_q_a_gain, mla_kv_a_gain, mla_w_uq, mla_w_ukv, mla_qn_gain, mla_qr_gain, mla_kn_gain, mla_kr_gain, mla_w_o, ffn_w_in, ffn_conv_w, ffn_conv_b, ffn_w_out, loss_target, m_norm_mix_g, m_norm_ffn_g, m_pool_w, m_pool_scale, m_swa_w_qkv, m_swa_q_gain, m_swa_k_gain, m_swa_sinks, m_swa_w_o, m_mla_w_down, m_mla_q_a_gain, m_mla_kv_a_gain, m_mla_w_uq, m_mla_w_ukv, m_mla_qn_gain, m_mla_qr_gain, m_mla_kn_gain, m_mla_kr_gain, m_mla_w_o, m_ffn_w_in, m_ffn_conv_w, m_ffn_conv_b, m_ffn_w_out, v_norm_mix_g, v_norm_ffn_g, v_pool_w, v_pool_scale, v_swa_w_qkv, v_swa_q_gain, v_swa_k_gain, v_swa_sinks, v_swa_w_o, v_mla_w_down, v_mla_q_a_gain, v_mla_kv_a_gain, v_mla_w_uq, v_mla_w_ukv, v_mla_qn_gain, v_mla_qr_gain, v_mla_kn_gain, v_mla_kr_gain, v_mla_w_o, v_ffn_w_in, v_ffn_conv_w, v_ffn_conv_b, v_ffn_w_out):
    args = dict(locals())
    W = {n: args[n] for n in WEIGHTS}
    M = {n: args["m_" + n] for n in WEIGHTS}
    V = {n: args["v_" + n] for n in WEIGHTS}
    xs = x[0]
    S, D = xs.shape
    chip = 2 * lax.axis_index("x") + lax.axis_index("y")

    big = ['pool_w', 'swa_w_qkv', 'swa_w_o', 'mla_w_down', 'mla_w_uq', 'mla_w_ukv', 'mla_w_o', 'ffn_w_in', 'ffn_w_out']
    small_sharded = [n for n in WEIGHTS if SMALL.get(n) is not None]
    mixer_w = {0: ['pool_w'], 1: ['swa_w_qkv', 'swa_w_o'], 2: ['mla_w_down', 'mla_w_uq', 'mla_w_ukv', 'mla_w_o']}
    Wb = {n: W[n].astype(BF16) for n in big}
    Wg = {}

    def mixer_keys(i):
        return [(n, i // 3) for n in mixer_w[i % 3]]

    def gather_rider(keys):
        return (lambda: _gather_comm([(Wb[n], l) for n, l in keys])), (lambda res: Wg.update(zip(keys, res)))

    assert DEPTH == 4
    keys0 = mixer_keys(0) + [('ffn_w_in', 0)]
    riders = {
        'ffn_in0': gather_rider([('ffn_w_out', 0)] + mixer_keys(1)),
        'ffn_out0': gather_rider([('ffn_w_out', 1)]),
        'swa_attn1': gather_rider([('ffn_w_in', 1)]),
        'ffn_in1': gather_rider(mixer_keys(2)),
        'ffn_out1': gather_rider([('ffn_w_out', 2)]),
        'mla_attn2': gather_rider([('ffn_w_in', 2), ('ffn_w_in', 3), ('ffn_w_out', 3)] + mixer_keys(3)),
    }
    first = _run_comm(_gather_comm([(Wb[n], l) for n, l in keys0] + [(W[n], None) for n in small_sharded]), "gather_first")
    Wg.update(zip(keys0, first))
    full = dict(W)
    for n, r in zip(small_sharded, first[len(keys0):]):
        full[n] = _chips_to_axis(r, SMALL[n])
    rows = lambda a: a.reshape((a.shape[0] * a.shape[1],) + a.shape[2:])

    posf = positions.astype(F32)
    slopes = jnp.asarray(2.0 ** (-8.0 * np.arange(1, SWA_HEADS + 1) / SWA_HEADS), dtype=F32)
    pos = (posf.reshape(S, 1), posf.reshape(1, S), slopes)
    inv = ROPE_THETA ** (-jnp.arange(0, MLA_ROPE, 2, dtype=F32) / MLA_ROPE)
    ang = posf[:, None] * inv[None, :]
    cos, sin = jnp.cos(ang), jnp.sin(ang)
    rope = (jnp.concatenate([cos, cos], axis=1), jnp.concatenate([-sin, sin], axis=1))

    margs_cache = {}

    def mixer_args(i):
        if i in margs_cache:
            return margs_cache[i]
        kind, j = i % 3, i // 3
        g = full['norm_mix_g'][i:i + 1]
        if kind == 0:
            byg = jnp.swapaxes(Wg[('pool_w', j)], 0, 1)
            a = (g, byg.reshape(byg.shape[0], -1, byg.shape[3]), full['pool_scale'][j:j + 1])
        elif kind == 1:
            a = (g, Wg[('swa_w_qkv', j)], full['swa_q_gain'][j:j + 1], full['swa_k_gain'][j:j + 1],
                 full['swa_sinks'][j:j + 1], rows(Wg[('swa_w_o', j)]), pos)
        else:
            a = (g, rows(Wg[('mla_w_down', j)]), full['mla_q_a_gain'][j:j + 1], full['mla_kv_a_gain'][j:j + 1],
                 Wg[('mla_w_uq', j)], Wg[('mla_w_ukv', j)], full['mla_qn_gain'][j:j + 1], full['mla_qr_gain'][j:j + 1],
                 full['mla_kn_gain'][j:j + 1], full['mla_kr_gain'][j:j + 1], rows(Wg[('mla_w_o', j)]), rope)
        margs_cache[i] = (kind, a)
        return kind, a

    def ffn_args(i):
        return (full['norm_ffn_g'][i:i + 1], Wg[('ffn_w_in', i)], full['ffn_conv_w'][i], full['ffn_conv_b'][i:i + 1],
                lambda: rows(Wg[('ffn_w_out', i)]))

    fwd = (_pool_fwd, _swa_fwd, _mla_fwd)
    bwd = (_pool_bwd, _swa_bwd, _mla_bwd)
    tape = []
    cur = xs
    for i in range(DEPTH):
        kind, margs = mixer_args(i)
        xm, msaved = fwd[kind](cur, *margs, i, riders)
        xo, fsaved = _ffn_fwd(xm, *ffn_args(i), i, riders)
        tape.append((cur, xm, msaved, fsaved))
        cur = xo
    sq, dcur = _loss_head(cur, loss_target[0], "loss_head")
    loss = lax.psum(0.5 / D * jnp.sum(sq), ("x", "y", "c"))

    def by_chip(n, gl):
        if n == 'pool_w':
            G, dg_ = gl.shape[0], gl.shape[1]
            return gl.reshape(G, N_CHIPS, dg_ // N_CHIPS, dg_).swapaxes(0, 1)
        if len(gl.shape) == 3:
            return gl
        return gl.reshape((N_CHIPS, gl.shape[0] // N_CHIPS) + gl.shape[1:])

    bufs = {n: lax.empty((N_CHIPS,) + tuple(W[n].shape), BF16) for n in big}

    def scatter_rider(group):
        def make(made):
            grp = group(made) if callable(group) else group
            names = list(dict.fromkeys(it[0] for it in grp))
            build = lambda: _scatter_comm([(gl, names.index(n), l, r, cut) for n, l, gl, r, cut in grp],
                                          [bufs[n] for n in names])
            return build, (lambda res: bufs.update(zip(names, res)))
        return make

    def own_w_out(i):
        return scatter_rider(lambda made: [('ffn_w_out', i, by_chip('ffn_w_out', made['ffn_w_out']), None, True)])

    grads = {n: [None] * W[n].shape[0] for n in WEIGHTS}
    half = W['ffn_w_in'].shape[1] // 2
    quarter = half // 2
    riders = {f'ffn_dwin{DEPTH - 1}': own_w_out(DEPTH - 1)}
    for i in reversed(range(DEPTH)):
        kind, margs = mixer_args(i)
        x_in, xm, msaved, fsaved = tape[i]
        dxm, fg = _ffn_bwd(dcur, xm, *ffn_args(i), fsaved, i, riders, split_dwin=(i == 0))
        dcur, mg = bwd[kind](dxm, x_in, *margs, msaved, i, riders)
        for n, gval in fg.items():
            grads[n][i] = gval
        for n, gval in mg.items():
            grads[n][i if n == 'norm_mix_g' else i // 3] = gval
        if i > 0:
            g_in = by_chip('ffn_w_in', fg['ffn_w_in'])
            piece = lambda q: ('ffn_w_in', i, g_in, (q * quarter, quarter), True)
            mixer_g = [(n, i // 3, by_chip(n, mg[n]), None, True) for n in mixer_w[kind]]
            on_attn = (i - 1) % 3 == 1
            riders = {f'ffn_dwout{i - 1}': scatter_rider([piece(0)]),
                      f'ffn_dact{i - 1}': scatter_rider([piece(1)]),
                      f'glu_bwd{i - 1}': scatter_rider([piece(2)]),
                      f'ffn_dh{i - 1}': scatter_rider([piece(3)] + ([] if on_attn else mixer_g)),
                      f'ffn_dwin{i - 1}' + ('a' if i == 1 else ''): own_w_out(i - 1)}
            if on_attn:
                riders[f'swa_attn_bwd{i - 1}'] = scatter_rider(mixer_g)
            if i == 1:
                riders['ffn_dwin0b'] = scatter_rider(
                    lambda made: [('ffn_w_in', 0, by_chip('ffn_w_in', made['ffn_w_in_a']), (0, half), False)])
    grad_x = dcur[None]
    last = [('ffn_w_in', 0, by_chip('ffn_w_in', grads['ffn_w_in'][0][1]), (half, half), False)]
    last += [(n, 0, by_chip(n, grads[n][0]), None, True) for n in mixer_w[0]]
    build, done = scatter_rider(last)(None)
    done(_run_comm(build(), "scatter_last"))

    partial = {n: _sum_chips(bufs[n].reshape(N_CHIPS, -1, bufs[n].shape[-1]), f"sum_{n}") for n in big}
    other = dict(zip(big, _run_comm(_swap_comm([partial[n] for n in big]), "swap_cores")))
    out = {}
    for n in big:
        C = partial[n].shape[1]
        res = _adamw(W[n].reshape(-1, C), M[n].reshape(-1, C), V[n].reshape(-1, C), [partial[n], other[n]], f"adamw_{n}")
        out[n] = [r.reshape(W[n].shape) for r in res]

    small = [n for n in WEIGHTS if n in SMALL]
    full_shapes = [tuple(full[n].shape) for n in small]
    sg = []
    for n in small:
        parts = grads[n]
        if n == 'ffn_conv_w':
            sg.append(jnp.stack(parts))
        else:
            sg.append(jnp.concatenate(parts, axis=0))
    buf, offs = _pack(sg)
    summed = _unpack(_allreduce_small(buf, "allreduce_small"), offs, full_shapes)
    gsm = []
    for n, gfull in zip(small, summed):
        ax = SMALL[n]
        if ax is not None:
            size = W[n].shape[ax]
            gfull = lax.dynamic_slice_in_dim(gfull, chip * size, size, axis=ax)
        gsm.append(gfull)
    gb, goffs = _pack(gsm)
    wb, _ = _pack([W[n] for n in small])
    mb, _ = _pack([M[n] for n in small])
    vb, _ = _pack([V[n] for n in small])
    res = _adamw(wb, mb, vb, [gb], "adamw_small")
    shapes = [tuple(W[n].shape) for n in small]
    unp = [_unpack(r, goffs, shapes) for r in res]
    for k, n in enumerate(small):
        out[n] = [u[k] for u in unp]

    return (loss, grad_x, *[out[n][0] for n in WEIGHTS], *[out[n][1] for n in WEIGHTS],
            *[out[n][2] for n in WEIGHTS], *[out[n][3] for n in WEIGHTS])
```

```python
import functools
import math

import numpy as np
import jax
import jax.numpy as jnp
from jax import lax
from jax.experimental import pallas as pl
from jax.experimental.pallas import tpu as pltpu

F32 = jnp.float32
BF16 = jnp.bfloat16

D_MODEL = 2048
SEQ = 4096
DEPTH = 4
EPS = 1e-6
POOL_WINDOWS = (2, 4, 8, 16)
SWA_HEADS = 32
SWA_KV_HEADS = 4
SWA_HEAD_DIM = 64
SWA_WINDOW = 128
MLA_HEADS = 16
MLA_NOPE = 128
MLA_ROPE = 64
MLA_V = 128
MLA_Q_RANK = 512
MLA_KV_RANK = 512
ROPE_THETA = 10000.0
D_FF = 5632
ADAM_LR = 0.001
ADAM_B1 = 0.9
ADAM_B2 = 0.999
ADAM_EPS = 1e-08
ADAM_WD = 0.01
ADAM_STEP = 10

N_CHIPS = 4
N_DEV = 8
MESH = pl.DeviceIdType.MESH
VMEM_LIMIT = 48 << 20
LANES = 128
NEG = -1e30

WEIGHTS = ['norm_mix_g', 'norm_ffn_g', 'pool_w', 'pool_scale', 'swa_w_qkv', 'swa_q_gain', 'swa_k_gain', 'swa_sinks',
           'swa_w_o', 'mla_w_down', 'mla_q_a_gain', 'mla_kv_a_gain', 'mla_w_uq', 'mla_w_ukv', 'mla_qn_gain',
           'mla_qr_gain', 'mla_kn_gain', 'mla_kr_gain', 'mla_w_o', 'ffn_w_in', 'ffn_conv_w', 'ffn_conv_b', 'ffn_w_out']
SMALL = {'norm_mix_g': None, 'norm_ffn_g': None, 'pool_scale': 1, 'swa_q_gain': None, 'swa_k_gain': None,
         'swa_sinks': None, 'mla_q_a_gain': 1, 'mla_kv_a_gain': 1, 'mla_qn_gain': None, 'mla_qr_gain': None,
         'mla_kn_gain': None, 'mla_kr_gain': None, 'ffn_conv_w': 2, 'ffn_conv_b': None}


def _cparams(sem):
    return pltpu.CompilerParams(dimension_semantics=sem, vmem_limit_bytes=VMEM_LIMIT)


def _tile(n, cands):
    for c in cands:
        if c <= n and n % c == 0:
            return c
    return n


WIDE = (1408, 1024, 768, 640, 512, 384, 256, 128)


def _dims(arr):
    if len(arr.shape) == 2:
        return arr.shape[0], arr.shape[1], arr.shape[1]
    return arr.shape[1], arr.shape[0] * arr.shape[2], arr.shape[2]


def _pspec(arr, tr, tc, fn):
    if len(arr.shape) == 2:
        return pl.BlockSpec((tr, tc), fn)
    per = arr.shape[2] // tc

    def im(*g):
        r, c = fn(*g)
        return (c // per, r, c % per)

    return pl.BlockSpec((None, tr, tc), im)


class _Comm:
    def __init__(self, ins, outs, aliases, sems, start, finish):
        self.ins, self.outs, self.aliases, self.sems, self.start, self.finish = ins, outs, aliases, sems, start, finish


def _call(body, grid, sem, in_specs, out_specs, out_shape, scratch, args, name, comm=None):
    if comm is None:
        return list(pl.pallas_call(body, grid=grid, in_specs=in_specs, out_specs=out_specs, out_shape=out_shape,
                                   scratch_shapes=scratch, compiler_params=_cparams(sem), name=name)(*args))
    ni, no, ns, nci, nco = len(args), len(out_shape), len(scratch), len(comm.ins), len(comm.outs)

    def wrapped(*refs):
        core_in, c_in = refs[:ni], refs[ni:ni + nci]
        core_out, c_out = refs[ni + nci:ni + nci + no], refs[ni + nci + no:ni + nci + no + nco]
        rest = refs[ni + nci + no + nco:]
        ids = [pl.program_id(d) for d in range(len(grid))]
        first = functools.reduce(lambda p, q: p & q, [i == 0 for i in ids])
        last = functools.reduce(lambda p, q: p & q, [i == n - 1 for i, n in zip(ids, grid)])

        @pl.when(first)
        def _():
            comm.start(c_in, c_out, rest[ns:])

        body(*core_in, *core_out, *rest[:ns])

        @pl.when(last)
        def _():
            comm.finish(c_in, c_out, rest[ns:])

    hbm = pl.BlockSpec(memory_space=pl.ANY)
    res = pl.pallas_call(
        wrapped, grid=grid, in_specs=list(in_specs) + [hbm] * nci, out_specs=list(out_specs) + [hbm] * nco,
        out_shape=list(out_shape) + list(comm.outs), scratch_shapes=list(scratch) + list(comm.sems),
        input_output_aliases={ni + i: no + o for i, o in comm.aliases.items()},
        compiler_params=_cparams(("arbitrary",) * len(grid)), name=name)(*args, *comm.ins)
    return list(res[:no]), list(res[no:])


MM_VMEM_BUDGET = 40 << 20
_CONTRACT = {"nn": (((1,), (0,)), ((), ())), "nt": (((1,), (1,)), ((), ())), "tn": (((0,), (0,)), ((), ()))}


def _mm(kind, a, b, out_dtype, name, res=None, out_panels=1, comm=None):
    ar, ac, aw = _dims(a)
    br, bc, bw = _dims(b)
    if kind == "nn":
        M, K, N, mw, kw, nw = ar, ac, bc, ar, aw, bw
    elif kind == "nt":
        M, K, N, mw, kw, nw = ar, ac, br, ar, math.gcd(aw, bw), br
    else:
        M, K, N, mw, kw, nw = ac, ar, bc, aw, ar, bw
    assert K == (br if kind != "nt" else bc)
    no = N // out_panels
    tm = _tile(mw, (1024, 512, 256, 128))
    tn = _tile(math.gcd(nw, no), WIDE if kind != "nt" else (1024, 512, 256, 128))
    fixed = tm * tn * (2 * jnp.dtype(out_dtype).itemsize + 4 + (8 if res is not None else 0))
    per_k = 2 * (tm * a.dtype.itemsize + tn * b.dtype.itemsize)
    tk = next((c for c in (kw, 4096, 2816, 2048, 1408, 1024, 768, 640, 512, 384, 256, 128)
               if c <= kw and kw % c == 0 and fixed + c * per_k <= MM_VMEM_BUDGET), _tile(kw, (128,)))
    nk = K // tk
    o_sds = jax.ShapeDtypeStruct((M, N) if out_panels == 1 else (out_panels, M, no), out_dtype)

    def body(*refs):
        a_ref, b_ref = refs[:2]

        def out(r):
            if res is not None:
                r = r + refs[2][...]
            return r.astype(out_dtype)

        prod = lax.dot_general(a_ref[...].astype(BF16), b_ref[...].astype(BF16), _CONTRACT[kind], preferred_element_type=F32)
        if nk == 1:
            refs[-1][...] = out(prod)
            return
        o_ref, acc = refs[-2:]
        k = pl.program_id(2)

        @pl.when(k == 0)
        def _():
            acc[...] = prod

        @pl.when(k > 0)
        def _():
            acc[...] += prod

        @pl.when(k == nk - 1)
        def _():
            o_ref[...] = out(acc[...])

    if kind == "nn":
        specs = [_pspec(a, tm, tk, lambda i, j, k: (i, k)), _pspec(b, tk, tn, lambda i, j, k: (k, j))]
    elif kind == "nt":
        specs = [_pspec(a, tm, tk, lambda i, j, k: (i, k)), _pspec(b, tn, tk, lambda i, j, k: (j, k))]
    else:
        specs = [_pspec(a, tk, tm, lambda i, j, k: (k, i)), _pspec(b, tk, tn, lambda i, j, k: (k, j))]
    args = [a, b]
    if res is not None:
        specs.append(_pspec(res, tm, tn, lambda i, j, k: (i, j)))
        args.append(res)
    got = _call(body, (M // tm, N // tn, nk), ("parallel", "parallel", "arbitrary"), specs,
                [_pspec(o_sds, tm, tn, lambda i, j, k: (i, j))], [o_sds], [pltpu.VMEM((tm, tn), F32)] if nk > 1 else [],
                args, name, comm)
    return got[0] if comm is None else (got[0][0], got[1])


def _mm_nn(a, b, out_dtype, name, **kw):
    return _mm("nn", a, b, out_dtype, name, **kw)


def _mm_nt(a, b, out_dtype, name, **kw):
    return _mm("nt", a, b, out_dtype, name, **kw)


def _mm_tn(a, b, out_dtype, name, **kw):
    return _mm("tn", a, b, out_dtype, name, **kw)


def _swap_halves(y):
    h = y.shape[-1] // 2
    return jnp.concatenate([y[:, h:], y[:, :h]], axis=1)


def _row_tile(R, d, limit=None):
    cap = max(8, (1 << 19) // d)
    cands = [c for c in (4096, 2048, 1024, 512, 256, 128, 64, 32, 16, 8) if c <= cap]
    if limit is not None:
        cands = [c for c in cands if limit % c == 0]
    return _tile(R, cands)


def _rms_fwd(x, g, out_dtype, name, scale=1.0, rope=None):
    R, d = x.shape
    tr = _row_tile(R, d, None if rope is None else rope[0].shape[0])

    def body(*refs):
        x_ref, g_ref = refs[:2]
        o_ref = refs[-1]
        xv = x_ref[...].astype(F32)
        y = xv * lax.rsqrt(jnp.mean(xv * xv, axis=-1, keepdims=True) + EPS)
        y = y * g_ref[...]
        if rope is not None:
            y = y * refs[2][...] + _swap_halves(y) * refs[3][...]
        if scale != 1.0:
            y = y * scale
        o_ref[...] = y.astype(o_ref.dtype)

    in_specs = [pl.BlockSpec((tr, d), lambda i: (i, 0)), pl.BlockSpec((1, d), lambda i: (0, 0))]
    args = [x, g]
    if rope is not None:
        nrt = rope[0].shape[0] // tr
        in_specs += [pl.BlockSpec((tr, d), lambda i: (i % nrt, 0))] * 2
        args += list(rope)
    return pl.pallas_call(
        body, grid=(R // tr,), in_specs=in_specs, out_specs=pl.BlockSpec((tr, d), lambda i: (i, 0)),
        out_shape=jax.ShapeDtypeStruct((R, d), out_dtype), compiler_params=_cparams(("parallel",)), name=name)(*args)


def _rms_bwd(x, g, dy, name, scale=1.0, rope=None, res=None):
    R, d = x.shape
    tr = _row_tile(R, d, None if rope is None else rope[0].shape[0])
    panels = dy.shape[0] if len(dy.shape) == 3 else 0

    def body(*refs):
        x_ref, g_ref, dy_ref = refs[:3]
        dx_ref, dg_ref = refs[-2:]
        i = pl.program_id(0)
        xv = x_ref[...].astype(F32)
        r = lax.rsqrt(jnp.mean(xv * xv, axis=-1, keepdims=True) + EPS)
        xhat = xv * r
        dyv = (jnp.concatenate([dy_ref[p] for p in range(panels)], axis=1) if panels else dy_ref[...]).astype(F32)
        if scale != 1.0:
            dyv = dyv * scale
        if rope is not None:
            dyv = dyv * refs[3][...] + _swap_halves(dyv * refs[4][...])

        @pl.when(i == 0)
        def _():
            dg_ref[...] = jnp.zeros_like(dg_ref)

        dg_ref[...] += jnp.sum(dyv * xhat, axis=0, keepdims=True)
        dxh = dyv * g_ref[...]
        dx = r * (dxh - xhat * jnp.mean(dxh * xhat, axis=-1, keepdims=True))
        if res is not None:
            dx = dx + refs[-3][...]
        dx_ref[...] = dx

    row = pl.BlockSpec((tr, d), lambda i: (i, 0))
    vec = pl.BlockSpec((1, d), lambda i: (0, 0))
    in_specs = [row, vec, pl.BlockSpec((panels, tr, d // panels), lambda i: (0, i, 0)) if panels else row]
    args = [x, g, dy]
    if rope is not None:
        nrt = rope[0].shape[0] // tr
        in_specs += [pl.BlockSpec((tr, d), lambda i: (i % nrt, 0))] * 2
        args += list(rope)
    if res is not None:
        in_specs.append(row)
        args.append(res)
    return pl.pallas_call(
        body, grid=(R // tr,), in_specs=in_specs, out_specs=[row, vec],
        out_shape=[jax.ShapeDtypeStruct((R, d), F32), jax.ShapeDtypeStruct((1, d), F32)],
        compiler_params=_cparams(("arbitrary",)), name=name)(*args)


PAD = 16


def _pool_win(x, fwd, out_dtype, name):
    G = len(POOL_WINDOWS)
    if fwd:
        S, D = x.shape
        dg = D // G
    else:
        _, S, dg = x.shape
    tc = _tile(dg, (128,))
    nt = dg // tc
    rc = _tile(S, (512,))

    def body(*refs):
        scr = refs[-1]
        o_refs = [refs[-2].at[gi] for gi in range(G)]
        x_refs = refs[:G] if fwd else [refs[0].at[gi] for gi in range(G)]
        zeros = jnp.zeros((PAD, tc), F32)
        scr[pl.ds(0, PAD), :] = zeros
        scr[pl.ds(PAD + S, PAD), :] = zeros
        for gi, w in enumerate(POOL_WINDOWS):
            left, right = w // 2, w - 1 - w // 2

            def count(r0):
                t = r0 + lax.broadcasted_iota(jnp.int32, (rc, 1), 0)
                return (jnp.minimum(t + right + 1, S) - jnp.maximum(t - left, 0)).astype(F32)

            for r0 in range(0, S, rc):
                xv = x_refs[gi][pl.ds(r0, rc), :]
                scr[pl.ds(PAD + r0, rc), :] = xv if fwd else xv / count(r0)
            lo, hi = (left, right) if fwd else (right, left)
            for r0 in range(0, S, rc):
                acc = scr[pl.ds(PAD + r0 - lo, rc), :]
                for o in range(-lo + 1, hi + 1):
                    acc = acc + scr[pl.ds(PAD + r0 + o, rc), :]
                xv = x_refs[gi][pl.ds(r0, rc), :]
                out = acc / count(r0) - xv if fwd else acc - xv
                o_refs[gi][pl.ds(r0, rc), :] = out.astype(out_dtype)

    panel = pl.BlockSpec((G, S, tc), lambda j: (0, 0, j))
    if fwd:
        in_specs = [pl.BlockSpec((S, tc), functools.partial(lambda j, gi: (0, gi * nt + j), gi=gi)) for gi in range(G)]
    else:
        in_specs = [panel]
    return pl.pallas_call(
        body, grid=(nt,), in_specs=in_specs, out_specs=panel, out_shape=jax.ShapeDtypeStruct((G, S, dg), out_dtype),
        scratch_shapes=[pltpu.VMEM((S + 2 * PAD, tc), F32)],
        compiler_params=_cparams(("parallel",)), name=name)(*([x] * G if fwd else [x]))


def _group_mm(kind, a, b, out_dtype, name):
    G = len(POOL_WINDOWS)
    if kind == "nt":
        S, dg = a.shape[0], a.shape[1] // G
    else:
        _, S, dg = a.shape
    tm = _tile(S, (1024, 512, 256, 128))
    nm = S // tm
    pan = pl.BlockSpec((None, tm, dg), lambda g, i: (g, i, 0))
    col = pl.BlockSpec((tm, dg), lambda g, i: (i, g))
    sq = pl.BlockSpec((None, dg, dg), lambda g, i: (g, 0, 0))

    def body(a_ref, b_ref, o_ref):
        prod = lax.dot_general(a_ref[...].astype(BF16), b_ref[...].astype(BF16), _CONTRACT[kind], preferred_element_type=F32)
        if kind != "tn":
            o_ref[...] = prod.astype(out_dtype)
            return
        i = pl.program_id(1)

        @pl.when(i == 0)
        def _():
            o_ref[...] = prod

        @pl.when(i > 0)
        def _():
            o_ref[...] += prod

    specs, out_spec, shape = {"nn": ([pan, sq], col, (S, G * dg)), "nt": ([col, sq], pan, (G, S, dg)),
                              "tn": ([pan, col], sq, (G, dg, dg))}[kind]
    assert kind != "tn" or out_dtype == F32
    return pl.pallas_call(body, grid=(G, nm), in_specs=specs, out_specs=out_spec,
                          out_shape=jax.ShapeDtypeStruct(shape, out_dtype),
                          compiler_params=_cparams(("parallel", "arbitrary")), name=name)(a, b)


def _scale_res(x, y, scale, name):
    S, D = x.shape
    tr = _row_tile(S, D)
    row = pl.BlockSpec((tr, D), lambda i: (i, 0))

    def body(x_ref, y_ref, s_ref, o_ref):
        o_ref[...] = x_ref[...] + y_ref[...] * s_ref[...]

    return pl.pallas_call(body, grid=(S // tr,), in_specs=[row, row, pl.BlockSpec((1, D), lambda i: (0, 0))],
                          out_specs=row, out_shape=jax.ShapeDtypeStruct((S, D), F32),
                          compiler_params=_cparams(("parallel",)), name=name)(x, y, scale)


def _scale_bwd(dy, y, scale, name):
    S, D = dy.shape
    tr = _row_tile(S, D)
    row = pl.BlockSpec((tr, D), lambda i: (i, 0))
    vec = pl.BlockSpec((1, D), lambda i: (0, 0))

    def body(dy_ref, y_ref, s_ref, o_ref, ds_ref):
        @pl.when(pl.program_id(0) == 0)
        def _():
            ds_ref[...] = jnp.zeros_like(ds_ref)

        d = dy_ref[...]
        ds_ref[...] += jnp.sum(d * y_ref[...], axis=0, keepdims=True)
        o_ref[...] = (d * s_ref[...]).astype(BF16)

    return pl.pallas_call(body, grid=(S // tr,), in_specs=[row, row, vec], out_specs=[row, vec],
                          out_shape=[jax.ShapeDtypeStruct((S, D), BF16), jax.ShapeDtypeStruct((1, D), F32)],
                          compiler_params=_cparams(("arbitrary",)), name=name)(dy, y, scale)


GPAD = 8


def _sigmoid(z):
    return 1.0 / (1.0 + jnp.exp(-z))


def _glu_fwd(u, cw, cb, name):
    _, S, F = u.shape
    tc = _tile(F, (128,))
    rc = _tile(S, (512,))

    def body(u_ref, w_ref, b_ref, o_ref, scr):
        zeros = jnp.zeros((GPAD, tc), F32)
        scr[pl.ds(0, GPAD), :] = zeros
        scr[pl.ds(GPAD + S, GPAD), :] = zeros
        for r0 in range(0, S, rc):
            scr[pl.ds(GPAD + r0, rc), :] = u_ref[0, pl.ds(r0, rc), :]
        w0, w1, w2, b = w_ref[0:1, :], w_ref[1:2, :], w_ref[2:3, :], b_ref[...]
        for r0 in range(0, S, rc):
            gc = (scr[pl.ds(GPAD + r0 - 1, rc), :] * w0 + scr[pl.ds(GPAD + r0, rc), :] * w1
                  + scr[pl.ds(GPAD + r0 + 1, rc), :] * w2 + b)
            o_ref[pl.ds(r0, rc), :] = (gc * _sigmoid(gc) * u_ref[1, pl.ds(r0, rc), :]).astype(BF16)

    return pl.pallas_call(
        body, grid=(F // tc,),
        in_specs=[pl.BlockSpec((2, S, tc), lambda j: (0, 0, j)), pl.BlockSpec((3, tc), lambda j: (0, j)),
                  pl.BlockSpec((1, tc), lambda j: (0, j))],
        out_specs=pl.BlockSpec((S, tc), lambda j: (0, j)), out_shape=jax.ShapeDtypeStruct((S, F), BF16),
        scratch_shapes=[pltpu.VMEM((S + 2 * GPAD, tc), F32)],
        compiler_params=_cparams(("parallel",)), name=name)(u, cw, cb)


def _glu_bwd(u, cw, cb, dact, name, comm=None):
    _, S, F = u.shape
    tc = _tile(F, (128,))
    rc = _tile(S, (512,))

    def body(u_ref, w_ref, b_ref, da_ref, du_ref, dw_ref, db_ref, scr_g, scr_d):
        zeros = jnp.zeros((GPAD, tc), F32)
        for scr in (scr_g, scr_d):
            scr[pl.ds(0, GPAD), :] = zeros
            scr[pl.ds(GPAD + S, GPAD), :] = zeros
        for r0 in range(0, S, rc):
            scr_g[pl.ds(GPAD + r0, rc), :] = u_ref[0, pl.ds(r0, rc), :]
        w0, w1, w2, b = w_ref[0:1, :], w_ref[1:2, :], w_ref[2:3, :], b_ref[...]
        sums = [jnp.zeros((1, tc), F32) for _ in range(4)]
        for r0 in range(0, S, rc):
            gp = scr_g[pl.ds(GPAD + r0 - 1, rc), :]
            g0 = scr_g[pl.ds(GPAD + r0, rc), :]
            gn = scr_g[pl.ds(GPAD + r0 + 1, rc), :]
            gc = gp * w0 + g0 * w1 + gn * w2 + b
            sig = _sigmoid(gc)
            da = da_ref[pl.ds(r0, rc), :]
            du_ref[1, pl.ds(r0, rc), :] = (da * (gc * sig)).astype(BF16)
            dgc = da * u_ref[1, pl.ds(r0, rc), :] * (sig * (1.0 + gc * (1.0 - sig)))
            scr_d[pl.ds(GPAD + r0, rc), :] = dgc
            for n, t in enumerate((dgc * gp, dgc * g0, dgc * gn, dgc)):
                sums[n] = sums[n] + jnp.sum(t, axis=0, keepdims=True)
        dw_ref[...] = jnp.concatenate(sums[:3], axis=0)
        db_ref[...] = sums[3]
        for r0 in range(0, S, rc):
            dg = (scr_d[pl.ds(GPAD + r0 + 1, rc), :] * w0 + scr_d[pl.ds(GPAD + r0, rc), :] * w1
                  + scr_d[pl.ds(GPAD + r0 - 1, rc), :] * w2)
            du_ref[0, pl.ds(r0, rc), :] = dg.astype(BF16)

    col = pl.BlockSpec((S, tc), lambda j: (0, j))
    return _call(
        body, (F // tc,), ("parallel",),
        [pl.BlockSpec((2, S, tc), lambda j: (0, 0, j)), pl.BlockSpec((3, tc), lambda j: (0, j)),
         pl.BlockSpec((1, tc), lambda j: (0, j)), col],
        [pl.BlockSpec((2, S, tc), lambda j: (0, 0, j)), pl.BlockSpec((3, tc), lambda j: (0, j)),
         pl.BlockSpec((1, tc), lambda j: (0, j))],
        [jax.ShapeDtypeStruct((2, S, F), BF16), jax.ShapeDtypeStruct((3, F), F32), jax.ShapeDtypeStruct((1, F), F32)],
        [pltpu.VMEM((S + 2 * GPAD, tc), F32), pltpu.VMEM((S + 2 * GPAD, tc), F32)], [u, cw, cb, dact], name, comm)


def _dot_nt(a, b):
    return lax.dot_general(a, b, (((1,), (1,)), ((), ())), preferred_element_type=F32)


def _dot_tn(a, b):
    return lax.dot_general(a, b, (((0,), (0,)), ((), ())), preferred_element_type=F32)


def _swa_specs(S, nq, G, hd, bq):
    prev = lambda j: jnp.maximum(j - 1, 0)
    nxt = lambda j: jnp.minimum(j + 1, nq - 1)
    kv = [pl.BlockSpec((None, bq, hd), functools.partial(lambda kh, j, f: (kh, f(j), 0), f=f))
          for f in (prev, lambda j: j, nxt)]
    pk = [pl.BlockSpec((1, bq), functools.partial(lambda kh, j, f: (0, f(j)), f=f)) for f in (prev, lambda j: j, nxt)]
    smem = pl.BlockSpec(memory_space=pltpu.SMEM)
    return ([pl.BlockSpec((bq, G * hd), lambda kh, j: (j, kh))] + kv + kv
            + [pl.BlockSpec((bq, 1), lambda kh, j: (j, 0))] + pk + [smem, smem])


def _swa_stack(x, G):
    w = x.shape[1] // G
    return jnp.concatenate([x[:, g * w:(g + 1) * w] for g in range(G)], axis=0)


def _swa_unstack(x, G):
    bq = x.shape[0] // G
    return jnp.concatenate([x[g * bq:(g + 1) * bq] for g in range(G)], axis=1)


def _swa_scores(j, kh, S, bq, G, qs, kspan, pq, pk, slopes):
    qi = j * bq + lax.broadcasted_iota(jnp.int32, (bq, 1), 0)
    ki = (j - 1) * bq + lax.broadcasted_iota(jnp.int32, (1, 3 * bq), 1)
    bias = jnp.where((jnp.abs(qi - ki) <= SWA_WINDOW) & (ki >= 0) & (ki < S), 0.0, NEG)
    dist = jnp.abs(pq - pk)
    raw = _dot_nt(qs, kspan)
    return [raw[g * bq:(g + 1) * bq] - slopes[kh * G + g] * dist + bias for g in range(G)]


def _swa_fwd_attn(qn, kT, vT, posq, posk, slopes, sinks, name, comm=None):
    Hkv, S, hd = kT.shape
    G = qn.shape[1] // (Hkv * hd)
    bq = SWA_WINDOW
    nq = S // bq

    def body(q_ref, k0, k1, k2, v0, v1, v2, pq, p0, p1, p2, slopes_ref, sinks_ref, o_ref, l_ref):
        kh, j = pl.program_id(0), pl.program_id(1)
        kspan = jnp.concatenate([k0[...], k1[...], k2[...]], axis=0)
        vspan = jnp.concatenate([v0[...], v1[...], v2[...]], axis=0)
        pk = jnp.concatenate([p0[...], p1[...], p2[...]], axis=1)
        ps, lses = [], []
        for g, s in enumerate(_swa_scores(j, kh, S, bq, G, _swa_stack(q_ref[...], G), kspan, pq[...], pk, slopes_ref)):
            sink = sinks_ref[kh * G + g]
            m = jnp.maximum(jnp.max(s, axis=-1, keepdims=True), sink)
            p = jnp.exp(s - m)
            den = jnp.sum(p, axis=-1, keepdims=True) + jnp.exp(sink - m)
            ps.append((p / den).astype(BF16))
            lses.append(m + jnp.log(den))
        o_ref[...] = _swa_unstack(jnp.dot(jnp.concatenate(ps, axis=0), vspan, preferred_element_type=F32), G)
        l_ref[...] = jnp.concatenate(lses, axis=1)

    return _call(
        body, (Hkv, nq), ("parallel", "parallel"), _swa_specs(S, nq, G, hd, bq),
        [pl.BlockSpec((bq, G * hd), lambda kh, j: (j, kh)), pl.BlockSpec((None, bq, G), lambda kh, j: (kh, j, 0))],
        [jax.ShapeDtypeStruct(qn.shape, F32), jax.ShapeDtypeStruct((Hkv, S, G), F32)], [],
        [qn, kT, kT, kT, vT, vT, vT, posq, posk, posk, posk, slopes, sinks], name, comm)


def _swa_bwd_attn(qn, kT, vT, posq, posk, slopes, sinks, o, lse, do, name, comm=None):
    Hkv, S, hd = kT.shape
    G = qn.shape[1] // (Hkv * hd)
    bq = SWA_WINDOW
    nq = S // bq

    def body(q_ref, k0, k1, k2, v0, v1, v2, pq, p0, p1, p2, slopes_ref, sinks_ref, o_ref, l_ref, do_ref,
             dq_ref, dk_ref, dv_ref, ds_ref):
        kh, j = pl.program_id(0), pl.program_id(1)
        kspan = jnp.concatenate([k0[...], k1[...], k2[...]], axis=0)
        vspan = jnp.concatenate([v0[...], v1[...], v2[...]], axis=0)
        pk = jnp.concatenate([p0[...], p1[...], p2[...]], axis=1)
        qs = _swa_stack(q_ref[...], G)
        dos = _swa_stack(do_ref[...], G)
        delta = jnp.sum(dos * _swa_stack(o_ref[...], G), axis=-1, keepdims=True)
        dos = dos.astype(BF16)
        dp = _dot_nt(dos, vspan)
        ps, dss, dsinks = [], [], []
        for g, s in enumerate(_swa_scores(j, kh, S, bq, G, qs, kspan, pq[...], pk, slopes_ref)):
            rows = slice(g * bq, (g + 1) * bq)
            lg = l_ref[:, g:g + 1]
            p = jnp.exp(s - lg)
            ps.append(p.astype(BF16))
            dss.append((p * (dp[rows] - delta[rows])).astype(BF16))
            dsinks.append(jnp.sum(-jnp.exp(sinks_ref[kh * G + g] - lg) * delta[rows], axis=0, keepdims=True))
        dsc = jnp.concatenate(dss, axis=0)
        dq_ref[...] = _swa_unstack(jnp.dot(dsc, kspan, preferred_element_type=F32), G)
        dk_ref[...] = _dot_tn(dsc, qs)
        dv_ref[...] = _dot_tn(jnp.concatenate(ps, axis=0), dos)

        @pl.when(j == 0)
        def _():
            ds_ref[...] = jnp.zeros_like(ds_ref)

        ds_ref[...] += jnp.concatenate(dsinks, axis=1)

    qblk = pl.BlockSpec((bq, G * hd), lambda kh, j: (j, kh))
    span = pl.BlockSpec((None, None, 3 * bq, hd), lambda kh, j: (kh, j, 0, 0))
    return _call(
        body, (Hkv, nq), ("parallel", "arbitrary"),
        _swa_specs(S, nq, G, hd, bq) + [qblk, pl.BlockSpec((None, bq, G), lambda kh, j: (kh, j, 0)), qblk],
        [qblk, span, span, pl.BlockSpec((None, 1, G), lambda kh, j: (kh, 0, 0))],
        [jax.ShapeDtypeStruct(qn.shape, F32), jax.ShapeDtypeStruct((Hkv, nq, 3 * bq, hd), F32),
         jax.ShapeDtypeStruct((Hkv, nq, 3 * bq, hd), F32), jax.ShapeDtypeStruct((Hkv, 1, G), F32)], [],
        [qn, kT, kT, kT, vT, vT, vT, posq, posk, posk, posk, slopes, sinks, o, lse, do], name, comm)


def _overlap3(spans, name):
    Hkv, nq, bq3, hd = spans.shape
    bq = bq3 // 3
    sp = spans.reshape(Hkv, nq, 3, bq, hd)

    def body(a_ref, b_ref, c_ref, o_ref):
        b = pl.program_id(1)
        acc = b_ref[...]
        acc = acc + jnp.where(b > 0, a_ref[...], 0.0)
        acc = acc + jnp.where(b < nq - 1, c_ref[...], 0.0)
        o_ref[...] = acc

    def part(f, slot):
        return pl.BlockSpec((None, None, None, bq, hd), lambda kh, b: (kh, f(b), slot, 0, 0))

    return pl.pallas_call(
        body, grid=(Hkv, nq),
        in_specs=[part(lambda b: jnp.maximum(b - 1, 0), 2), part(lambda b: b, 1), part(lambda b: jnp.minimum(b + 1, nq - 1), 0)],
        out_specs=pl.BlockSpec((None, bq, hd), lambda kh, b: (kh, b, 0)),
        out_shape=jax.ShapeDtypeStruct((Hkv, nq * bq, hd), F32),
        compiler_params=_cparams(("parallel", "parallel")), name=name)(sp, sp, sp)


def _mla_fwd_attn(q, k, vT, name, comm=None):
    H, S, dk = q.shape
    dv = vT.shape[1]
    tq = _tile(S, (1024, 512, 256, 128))
    tk = _tile(S, (1024, 512, 256, 128))
    cq = _tile(tq, (1024, 512, 256, 128))
    nk = S // tk

    def body(q_ref, k_ref, v_ref, o_ref, l_ref, m_s, l_s, acc):
        kk = pl.program_id(2)

        @pl.when(kk == 0)
        def _():
            m_s[...] = jnp.full_like(m_s, NEG)
            l_s[...] = jnp.zeros_like(l_s)
            acc[...] = jnp.zeros_like(acc)

        kb, vb = k_ref[...], v_ref[...]
        for c0 in range(0, tq, cq):
            cols = pl.ds(c0, cq)
            sT = _dot_nt(kb, q_ref[cols, :])
            m_old = m_s[:, cols]
            m_new = jnp.maximum(m_old, jnp.max(sT, axis=0, keepdims=True))
            a = jnp.exp(m_old - m_new)
            p = jnp.exp(sT - m_new)
            l_s[:, cols] = a * l_s[:, cols] + jnp.sum(p, axis=0, keepdims=True)
            acc[:, cols] = a * acc[:, cols] + jnp.dot(vb, p.astype(BF16), preferred_element_type=F32)
            m_s[:, cols] = m_new

        @pl.when(kk == nk - 1)
        def _():
            o_ref[...] = acc[...] / l_s[...]
            l_ref[...] = m_s[...] + jnp.log(l_s[...])

    return _call(
        body, (H, S // tq, nk), ("parallel", "parallel", "arbitrary"),
        [pl.BlockSpec((None, tq, dk), lambda h, i, kk: (h, i, 0)),
         pl.BlockSpec((None, tk, dk), lambda h, i, kk: (h, kk, 0)),
         pl.BlockSpec((None, dv, tk), lambda h, i, kk: (h, 0, kk))],
        [pl.BlockSpec((None, dv, tq), lambda h, i, kk: (h, 0, i)), pl.BlockSpec((None, 1, tq), lambda h, i, kk: (h, 0, i))],
        [jax.ShapeDtypeStruct((H, dv, S), F32), jax.ShapeDtypeStruct((H, 1, S), F32)],
        [pltpu.VMEM((1, tq), F32), pltpu.VMEM((1, tq), F32), pltpu.VMEM((dv, tq), F32)], [q, k, vT], name, comm)


def _mla_bwd_attn(q, k, kT, v, oT, doT, lse, name):
    H, S, dk = q.shape
    dv = v.shape[2]
    tq = _tile(S, (1024, 512, 256, 128))
    tk = _tile(S, (1024, 512, 256, 128))
    nq = S // tq

    def body(q_ref, k_ref, kT_ref, v_ref, o_ref, do_ref, l_ref, dq_ref, dk_ref, dv_ref, dk_acc, dv_acc):
        j, i = pl.program_id(1), pl.program_id(2)

        @pl.when(i == 0)
        def _():
            dk_acc[...] = jnp.zeros_like(dk_acc)
            dv_acc[...] = jnp.zeros_like(dv_acc)

        @pl.when((i == 0) & (j == 0))
        def _():
            dq_ref[...] = jnp.zeros_like(dq_ref)

        qb, dob = q_ref[...], do_ref[...]
        delta = jnp.sum(dob * o_ref[...], axis=0, keepdims=True)
        dob = dob.astype(BF16)
        pT = jnp.exp(_dot_nt(k_ref[...], qb) - l_ref[...])
        dpT = jnp.dot(v_ref[...], dob, preferred_element_type=F32)
        dsT = (pT * (dpT - delta)).astype(BF16)
        dk_acc[...] += jnp.dot(dsT, qb, preferred_element_type=F32)
        dv_acc[...] += _dot_nt(pT.astype(BF16), dob)
        cols = pl.ds(pl.multiple_of(i * tq, tq), tq)
        dq_ref[:, cols] += jnp.dot(kT_ref[...], dsT, preferred_element_type=F32)

        @pl.when(i == nq - 1)
        def _():
            dk_ref[...] = dk_acc[...]
            dv_ref[...] = dv_acc[...]

    qrow = lambda d: pl.BlockSpec((None, tq, d), lambda h, j, i: (h, i, 0))
    krow = lambda d: pl.BlockSpec((None, tk, d), lambda h, j, i: (h, j, 0))
    qcol = lambda d: pl.BlockSpec((None, d, tq), lambda h, j, i: (h, 0, i))
    return pl.pallas_call(
        body, grid=(H, S // tk, nq),
        in_specs=[qrow(dk), krow(dk), pl.BlockSpec((None, dk, tk), lambda h, j, i: (h, 0, j)), krow(dv), qcol(dv), qcol(dv),
                  qcol(1)],
        out_specs=[pl.BlockSpec((None, dk, S), lambda h, j, i: (h, 0, 0)), krow(dk), krow(dv)],
        out_shape=[jax.ShapeDtypeStruct((H, dk, S), F32), jax.ShapeDtypeStruct((H, S, dk), F32),
                   jax.ShapeDtypeStruct((H, S, dv), F32)],
        scratch_shapes=[pltpu.VMEM((tk, dk), F32), pltpu.VMEM((tk, dv), F32)],
        compiler_params=_cparams(("parallel", "arbitrary", "arbitrary")), name=name)(q, k, kT, v, oT, doT, lse)


def _sum_heads(x, name):
    H, S, d = x.shape
    ts = _tile(S, (1024, 512, 256, 128))

    def body(x_ref, o_ref):
        acc = x_ref[0]
        for h in range(1, H):
            acc = acc + x_ref[h]
        o_ref[...] = acc

    return pl.pallas_call(body, grid=(S // ts,), in_specs=[pl.BlockSpec((H, ts, d), lambda i: (0, i, 0))],
                          out_specs=pl.BlockSpec((ts, d), lambda i: (i, 0)), out_shape=jax.ShapeDtypeStruct((S, d), F32),
                          compiler_params=_cparams(("parallel",)), name=name)(x)


def _loss_head(y, target, name):
    S, D = y.shape
    tr = _row_tile(S, D)
    row = pl.BlockSpec((tr, D), lambda i: (i, 0))
    vec = pl.BlockSpec((1, D), lambda i: (0, 0))

    def body(y_ref, t_ref, sq_ref, dy_ref):
        @pl.when(pl.program_id(0) == 0)
        def _():
            sq_ref[...] = jnp.zeros_like(sq_ref)

        e = y_ref[...] - t_ref[...]
        sq_ref[...] += jnp.sum(e * e, axis=0, keepdims=True)
        dy_ref[...] = e / D

    return pl.pallas_call(body, grid=(S // tr,), in_specs=[row, row], out_specs=[vec, row],
                          out_shape=[jax.ShapeDtypeStruct((1, D), F32), jax.ShapeDtypeStruct((S, D), F32)],
                          compiler_params=_cparams(("arbitrary",)), name=name)(y, target)


def _sum_chips(recv, name):
    _, R, C = recv.shape
    tr = _row_tile(R, C)

    def body(r_ref, o_ref):
        acc = r_ref[0].astype(F32)
        for i in range(1, N_CHIPS):
            acc = acc + r_ref[i].astype(F32)
        o_ref[...] = acc

    return pl.pallas_call(body, grid=(R // tr,), in_specs=[pl.BlockSpec((N_CHIPS, tr, C), lambda i: (0, i, 0))],
                          out_specs=pl.BlockSpec((tr, C), lambda i: (i, 0)), out_shape=jax.ShapeDtypeStruct((R, C), F32),
                          compiler_params=_cparams(("parallel",)), name=name)(recv)


def _adamw(w, m, v, gs, name, comm=None):
    R, C = w.shape
    tr = _row_tile(R, 2 * C)
    row = pl.BlockSpec((tr, C), lambda i: (i, 0))
    n = len(gs)

    def body(*refs):
        w_ref, m_ref, v_ref = refs[:3]
        g_ref, d_ref, nm_ref, nv_ref = refs[3 + n:]
        g = refs[3][...]
        for r in refs[4:3 + n]:
            g = g + r[...]
        mm = ADAM_B1 * m_ref[...] + (1.0 - ADAM_B1) * g
        vv = ADAM_B2 * v_ref[...] + (1.0 - ADAM_B2) * (g * g)
        m_hat = mm / (1.0 - ADAM_B1 ** ADAM_STEP)
        v_hat = vv / (1.0 - ADAM_B2 ** ADAM_STEP)
        g_ref[...] = g
        d_ref[...] = -ADAM_LR * (m_hat / (jnp.sqrt(v_hat) + ADAM_EPS) + ADAM_WD * w_ref[...])
        nm_ref[...] = mm
        nv_ref[...] = vv

    sds = jax.ShapeDtypeStruct((R, C), F32)
    return _call(body, (R // tr,), ("parallel",), [row] * (3 + n), [row] * 4, [sds] * 4, [], [w, m, v, *gs], name, comm)


def _chip_peers():
    x, y, c = lax.axis_index("x"), lax.axis_index("y"), lax.axis_index("c")
    others = [(1 - x, y), (x, 1 - y), (1 - x, 1 - y)]
    return x, y, c, 2 * x + y, [(px, py, 2 * px + py) for px, py in others]


HALVE_MIN_BYTES = 1 << 20


def _gather_comm(srcs):
    n = len(srcs)
    shapes = [tuple(a.shape[1:] if l is not None else a.shape) for a, l in srcs]
    halved = [s[0] % 2 == 0 and int(np.prod(s)) * a.dtype.itemsize >= HALVE_MIN_BYTES for s, (a, _) in zip(shapes, srcs)]
    outs = [jax.ShapeDtypeStruct((N_CHIPS,) + s, a.dtype) for s, (a, _) in zip(shapes, srcs)]
    dma = pltpu.SemaphoreType.DMA
    sems = [dma((3 * n,)), dma((3 * n,)), dma((3 * n,)), dma((3 * n,)), dma((n,))]

    def rows(ref, t, half):
        if not halved[t]:
            return ref
        h = shapes[t][0] // 2
        return ref.at[pl.ds(half * h, h)]

    def copies(in_refs, out_refs, sem_refs):
        send, recv, fsend, frecv, lsem = sem_refs
        x, y, c, me, peers = _chip_peers()
        src = [in_refs[t] if l is None else in_refs[t].at[l] for t, (_, l) in enumerate(srcs)]
        local = [pltpu.make_async_copy(src[t], out_refs[t].at[me], lsem.at[t]) for t in range(n)]

        def ici(t, j, origin):
            px, py, _ = peers[j]
            return pltpu.make_async_remote_copy(src_ref=rows(src[t], t, c), dst_ref=rows(out_refs[t].at[origin], t, c),
                                                send_sem=send.at[3 * t + j], recv_sem=recv.at[3 * t + j],
                                                device_id=(px, py, c), device_id_type=MESH)

        def hand(t, j, half):
            blk = rows(out_refs[t].at[peers[j][2]], t, half)
            return pltpu.make_async_remote_copy(src_ref=blk, dst_ref=blk, send_sem=fsend.at[3 * t + j],
                                                recv_sem=frecv.at[3 * t + j], device_id=(x, y, 1 - c), device_id_type=MESH)

        return c, me, peers, local, ici, hand

    def start(in_refs, out_refs, sem_refs):
        c, me, peers, local, ici, hand = copies(in_refs, out_refs, sem_refs)
        for t in range(n):
            local[t].start()
            for j in range(3):
                ici(t, j, me).start()

    def finish(in_refs, out_refs, sem_refs):
        c, me, peers, local, ici, hand = copies(in_refs, out_refs, sem_refs)
        for t in range(n):
            for j in range(3):
                ici(t, j, peers[j][2]).wait_recv()
                if halved[t]:
                    hand(t, j, c).start()
        for t in range(n):
            for j in range(3):
                if halved[t]:
                    hand(t, j, 1 - c).wait_recv()
        for t in range(n):
            for j in range(3):
                ici(t, j, me).wait_send()
                if halved[t]:
                    hand(t, j, c).wait_send()
            local[t].wait()

    return _Comm([a for a, _ in srcs], outs, {}, sems, start, finish)


def _scatter_comm(items, bufs):
    n = len(items)
    dma = pltpu.SemaphoreType.DMA
    sems = [dma((3 * n,)), dma((3 * n,)), dma((n,))]

    def piece(ref, rows):
        return ref if rows is None else ref.at[pl.ds(rows[0], rows[1])]

    def copies(in_refs, out_refs, sem_refs):
        send, recv, lsem = sem_refs
        x, y, c, me, peers = _chip_peers()

        def local(t):
            _, bi, l, rows, cut = items[t]
            return pltpu.make_async_copy(piece(in_refs[t].at[me], rows if cut else None),
                                         piece(out_refs[bi].at[me, l], rows), lsem.at[t])

        def ici(t, j, origin):
            _, bi, l, rows, cut = items[t]
            px, py, pi = peers[j]
            return pltpu.make_async_remote_copy(src_ref=piece(in_refs[t].at[pi], rows if cut else None),
                                                dst_ref=piece(out_refs[bi].at[origin, l], rows),
                                                send_sem=send.at[3 * t + j], recv_sem=recv.at[3 * t + j],
                                                device_id=(px, py, c), device_id_type=MESH)

        return me, peers, local, ici

    def start(in_refs, out_refs, sem_refs):
        me, peers, local, ici = copies(in_refs, out_refs, sem_refs)
        for t in range(n):
            local(t).start()
            for j in range(3):
                ici(t, j, me).start()

    def finish(in_refs, out_refs, sem_refs):
        me, peers, local, ici = copies(in_refs, out_refs, sem_refs)
        for t in range(n):
            for j in range(3):
                ici(t, j, peers[j][2]).wait_recv()
        for t in range(n):
            for j in range(3):
                ici(t, j, me).wait_send()
            local(t).wait()

    return _Comm([it[0] for it in items] + list(bufs), [jax.ShapeDtypeStruct(b.shape, b.dtype) for b in bufs],
                 {n + k: k for k in range(len(bufs))}, sems, start, finish)


def _run_comm(comm, name):
    ni, no = len(comm.ins), len(comm.outs)

    def body(*refs):
        comm.start(refs[:ni], refs[ni:ni + no], refs[ni + no:])
        comm.finish(refs[:ni], refs[ni:ni + no], refs[ni + no:])

    hbm = pl.BlockSpec(memory_space=pl.ANY)
    return pl.pallas_call(body, in_specs=[hbm] * ni, out_specs=[hbm] * no, out_shape=list(comm.outs),
                          scratch_shapes=list(comm.sems), input_output_aliases=dict(comm.aliases), name=name)(*comm.ins)


def _swap_comm(arrs):
    n = len(arrs)

    def copies(in_refs, out_refs, sem_refs):
        x, y, c = lax.axis_index("x"), lax.axis_index("y"), lax.axis_index("c")
        return [pltpu.make_async_remote_copy(src_ref=in_refs[t], dst_ref=out_refs[t], send_sem=sem_refs[0].at[t],
                                             recv_sem=sem_refs[1].at[t], device_id=(x, y, 1 - c), device_id_type=MESH)
                for t in range(n)]

    def start(in_refs, out_refs, sem_refs):
        for cp in copies(in_refs, out_refs, sem_refs):
            cp.start()

    def finish(in_refs, out_refs, sem_refs):
        for cp in copies(in_refs, out_refs, sem_refs):
            cp.wait()

    return _Comm(list(arrs), [jax.ShapeDtypeStruct(a.shape, a.dtype) for a in arrs], {},
                 [pltpu.SemaphoreType.DMA((n,)), pltpu.SemaphoreType.DMA((n,))], start, finish)


def _allreduce_small(buf, name):
    R, C = buf.shape

    def body(b_ref, o_ref, recv, send_sems, recv_sems):
        x, y, c = lax.axis_index("x"), lax.axis_index("y"), lax.axis_index("c")
        me = 4 * x + 2 * y + c
        recv[me] = b_ref[...]
        cps = []
        for mask in range(1, N_DEV):
            px, py, pc = x ^ (mask >> 2 & 1), y ^ (mask >> 1 & 1), c ^ (mask & 1)
            cp = pltpu.make_async_remote_copy(src_ref=b_ref, dst_ref=recv.at[me], send_sem=send_sems.at[mask - 1],
                                              recv_sem=recv_sems.at[mask - 1], device_id=(px, py, pc), device_id_type=MESH)
            cp.start()
            cps.append((cp, 4 * px + 2 * py + pc, (px, py, pc), mask))
        for cp, pi, dev, mask in cps:
            pltpu.make_async_remote_copy(src_ref=b_ref, dst_ref=recv.at[pi], send_sem=send_sems.at[mask - 1],
                                         recv_sem=recv_sems.at[mask - 1], device_id=dev, device_id_type=MESH).wait_recv()
        for cp, _, _, _ in cps:
            cp.wait_send()
        acc = recv[0]
        for i in range(1, N_DEV):
            acc = acc + recv[i]
        o_ref[...] = acc

    vm = pl.BlockSpec(memory_space=pltpu.VMEM)
    return pl.pallas_call(
        body, in_specs=[vm], out_specs=vm, out_shape=jax.ShapeDtypeStruct((R, C), F32),
        scratch_shapes=[pltpu.VMEM((N_DEV, R, C), F32), pltpu.SemaphoreType.DMA((N_DEV - 1,)),
                        pltpu.SemaphoreType.DMA((N_DEV - 1,))],
        compiler_params=pltpu.CompilerParams(vmem_limit_bytes=VMEM_LIMIT), name=name)(buf)


def _pack(arrs):
    parts, offs, n = [], [], 0
    for a in arrs:
        f = a.reshape(-1).astype(F32)
        k = -(-f.shape[0] // LANES) * LANES
        parts.append(jnp.pad(f, (0, k - f.shape[0])))
        offs.append(n)
        n += k
    total = -(-n // (8 * LANES)) * (8 * LANES)
    if total > n:
        parts.append(jnp.zeros((total - n,), F32))
    return jnp.concatenate(parts).reshape(-1, LANES), offs


def _unpack(buf, offs, shapes):
    flat = buf.reshape(-1)
    return [flat[o:o + int(np.prod(s))].reshape(s) for o, s in zip(offs, shapes)]


def _ride(call, riders, name, made=None):
    rider = riders.get(name)
    if rider is None:
        return call(name, None)
    build, done = rider if isinstance(rider, tuple) else rider(made)
    out, res = call(name, build())
    done(res)
    return out


def _ffn_fwd(xm, g, w_in, cw, cb, w_out, t, riders):
    h2 = _rms_fwd(xm, g, BF16, f"ffn_norm{t}")
    u = _ride(lambda n, c: _mm_nn(h2, w_in, F32, n, out_panels=2, comm=c), riders, f"ffn_in{t}")
    act = _glu_fwd(u, cw, cb, f"glu{t}")
    w_out = w_out()
    xo = _ride(lambda n, c: _mm_nn(act, w_out, F32, n, res=xm, comm=c), riders, f"ffn_out{t}")
    return xo, (h2, u, act)


def _ffn_bwd(dxo, xm, g, w_in, cw, cb, w_out, saved, t, riders, split_dwin=False):
    h2, u, act = saved
    w_out = w_out()
    made = {}
    made['ffn_w_out'] =_ride(lambda n, c: _mm_tn(act, dxo, BF16, n, comm=c), riders, f"ffn_dwout{t}", made)
    dact = _ride(lambda n, c: _mm_nt(dxo, w_out, F32, n, comm=c), riders, f"ffn_dact{t}", made)
    du, dcw, dcb = _ride(lambda n, c: _glu_bwd(u, cw, cb, dact, n, comm=c), riders, f"glu_bwd{t}", made)
    dh2 = _ride(lambda n, c: _mm_nt(du, w_in, F32, n, comm=c), riders, f"ffn_dh{t}", made)
    if split_dwin:
        half = h2.shape[1] // 2
        made['ffn_w_in_a'] = _ride(lambda n, c: _mm_tn(h2[:, :half], du, BF16, n, out_panels=N_CHIPS, comm=c),
                                   riders, f"ffn_dwin{t}a", made)
        dw_in = (made['ffn_w_in_a'], _ride(lambda n, c: _mm_tn(h2[:, half:], du, BF16, n, out_panels=N_CHIPS, comm=c),
                                           riders, f"ffn_dwin{t}b", made))
    else:
        dw_in = _ride(lambda n, c: _mm_tn(h2, du, BF16, n, out_panels=N_CHIPS, comm=c), riders, f"ffn_dwin{t}", made)
    dxm, dg = _rms_bwd(xm, g, dh2, f"ffn_norm_bwd{t}", res=dxo)
    return dxm, dict(ffn_w_in=dw_in, ffn_conv_w=dcw, ffn_conv_b=dcb, ffn_w_out=made['ffn_w_out'], norm_ffn_g=dg)


def _pool_fwd(x, g, wg, scale, t, riders):
    h = _rms_fwd(x, g, F32, f"pool_norm{t}")
    pooled = _pool_win(h, True, BF16, f"pool_win{t}")
    yraw = _group_mm("nn", pooled, wg, F32, f"pool_mm{t}")
    xm = _scale_res(x, yraw, scale, f"pool_out{t}")
    return xm, (pooled, yraw)


def _pool_bwd(dxm, x, g, wg, scale, saved, t, riders):
    pooled, yraw = saved
    dyraw, dscale = _scale_bwd(dxm, yraw, scale, f"pool_out_bwd{t}")
    dpool = _group_mm("nt", dyraw, wg, F32, f"pool_dp{t}")
    dw = _group_mm("tn", pooled, dyraw, F32, f"pool_dw{t}").astype(BF16)
    dh = _pool_win(dpool, False, F32, f"pool_win_bwd{t}")
    dx, dgn = _rms_bwd(x, g, dh, f"pool_norm_bwd{t}", res=dxm)
    return dx, dict(pool_w=dw, pool_scale=dscale, norm_mix_g=dgn)


def _swa_fwd(x, g, w_qkv, q_gain, k_gain, sinks, w_o, pos, t, riders):
    S = x.shape[0]
    Hq, Hkv, hd = SWA_HEADS, SWA_KV_HEADS, SWA_HEAD_DIM
    nq, nkv = Hq * hd, Hkv * hd
    posq, posk, slopes = pos
    h = _rms_fwd(x, g, BF16, f"swa_norm{t}")
    qkv = _mm_nn(h, w_qkv, F32, f"swa_qkv{t}")
    q = qkv[:, :nq].reshape(S * Hq, hd)
    k = qkv[:, nq:nq + nkv].reshape(S * Hkv, hd)
    qn = _rms_fwd(q, q_gain, BF16, f"swa_qnorm{t}", scale=hd ** -0.5).reshape(S, nq)
    kT = _rms_fwd(k, k_gain, BF16, f"swa_knorm{t}").reshape(S, Hkv, hd).transpose(1, 0, 2)
    vT = qkv[:, nq + nkv:].astype(BF16).reshape(S, Hkv, hd).transpose(1, 0, 2)
    o, lse = _ride(lambda n, c: _swa_fwd_attn(qn, kT, vT, posq, posk, slopes, sinks.reshape(-1), n, comm=c), riders,
                   f"swa_attn{t}")
    xm = _mm_nn(o, w_o, F32, f"swa_o{t}", res=x)
    return xm, (h, q, k, qn, kT, vT, o, lse)


def _swa_bwd(dxm, x, g, w_qkv, q_gain, k_gain, sinks, w_o, pos, saved, t, riders):
    S = x.shape[0]
    Hq, Hkv, hd = SWA_HEADS, SWA_KV_HEADS, SWA_HEAD_DIM
    nq, nkv = Hq * hd, Hkv * hd
    posq, posk, slopes = pos
    h, q, k, qn, kT, vT, o, lse = saved
    do = _mm_nt(dxm, w_o, F32, f"swa_do{t}")
    dw_o = _mm_tn(o, dxm, BF16, f"swa_dwo{t}")
    dqn, dkp, dvp, dsink = _ride(
        lambda n, c: _swa_bwd_attn(qn, kT, vT, posq, posk, slopes, sinks.reshape(-1), o, lse, do, n, comm=c), riders,
        f"swa_attn_bwd{t}")
    dkn = _overlap3(dkp, f"swa_dk{t}").transpose(1, 0, 2).reshape(S * Hkv, hd)
    dv = _overlap3(dvp, f"swa_dv{t}").transpose(1, 0, 2).reshape(S, nkv)
    dq, dqg = _rms_bwd(q, q_gain, dqn.reshape(S * Hq, hd), f"swa_qnorm_bwd{t}", scale=hd ** -0.5)
    dk, dkg = _rms_bwd(k, k_gain, dkn, f"swa_knorm_bwd{t}")
    dqkv = jnp.concatenate([dq.reshape(S, nq), dk.reshape(S, nkv), dv], axis=1)
    dh = _mm_nt(dqkv, w_qkv, F32, f"swa_dh{t}")
    dw_qkv = _mm_tn(h, dqkv, BF16, f"swa_dwqkv{t}", out_panels=N_CHIPS)
    dx, dgn = _rms_bwd(x, g, dh, f"swa_norm_bwd{t}", res=dxm)
    return dx, dict(swa_w_qkv=dw_qkv, swa_q_gain=dqg, swa_k_gain=dkg, swa_sinks=dsink.reshape(1, Hq), swa_w_o=dw_o,
                    norm_mix_g=dgn)


def _mla_fwd(x, g, w_down, q_a_gain, kv_a_gain, w_uq, w_ukv, qn_gain, qr_gain, kn_gain, kr_gain, w_o, rope, t, riders):
    S = x.shape[0]
    H, dn, R, dv, qr_, kvr = MLA_HEADS, MLA_NOPE, MLA_ROPE, MLA_V, MLA_Q_RANK, MLA_KV_RANK
    sc = (dn + R) ** -0.5
    h = _rms_fwd(x, g, BF16, f"mla_norm{t}")
    d = _mm_nn(h, w_down, F32, f"mla_down{t}")
    cq_pre, ckv_pre, kpe_pre = d[:, :qr_], d[:, qr_:qr_ + kvr], d[:, qr_ + kvr:]
    cq = _rms_fwd(cq_pre, q_a_gain, BF16, f"mla_cq{t}")
    ckv = _rms_fwd(ckv_pre, kv_a_gain, BF16, f"mla_ckv{t}")
    q3 = _mm_nn(cq, w_uq, F32, f"mla_uq{t}").reshape(S, H, dn + R).transpose(1, 0, 2)
    kv3 = _mm_nn(ckv, w_ukv, F32, f"mla_ukv{t}").reshape(S, H, dn + dv).transpose(1, 0, 2)
    qn_in, qp_in = q3[..., :dn].reshape(H * S, dn), q3[..., dn:].reshape(H * S, R)
    kn_in, v = kv3[..., :dn].reshape(H * S, dn), kv3[..., dn:].astype(BF16)
    qn = _rms_fwd(qn_in, qn_gain, BF16, f"mla_qn{t}", scale=sc).reshape(H, S, dn)
    qp = _rms_fwd(qp_in, qr_gain, BF16, f"mla_qp{t}", scale=sc, rope=rope).reshape(H, S, R)
    kn = _rms_fwd(kn_in, kn_gain, BF16, f"mla_kn{t}").reshape(H, S, dn)
    kp = _rms_fwd(kpe_pre, kr_gain, BF16, f"mla_kp{t}", rope=rope)
    qf = jnp.concatenate([qn, qp], axis=-1)
    kf = jnp.concatenate([kn, jnp.broadcast_to(kp[None], (H, S, R))], axis=-1)
    vT = v.transpose(0, 2, 1)
    oT, lse = _ride(lambda n, c: _mla_fwd_attn(qf, kf, vT, n, comm=c), riders, f"mla_attn{t}")
    oT2 = oT.reshape(H * dv, S)
    xm = _mm_tn(oT2, w_o, F32, f"mla_o{t}", res=x)
    return xm, (h, cq_pre, ckv_pre, kpe_pre, cq, ckv, qn_in, qp_in, kn_in, v, qf, kf, oT, lse)


def _mla_bwd(dxm, x, g, w_down, q_a_gain, kv_a_gain, w_uq, w_ukv, qn_gain, qr_gain, kn_gain, kr_gain, w_o, rope,
             saved, t, riders):
    S = x.shape[0]
    H, dn, R, dv = MLA_HEADS, MLA_NOPE, MLA_ROPE, MLA_V
    sc = (dn + R) ** -0.5
    h, cq_pre, ckv_pre, kpe_pre, cq, ckv, qn_in, qp_in, kn_in, v, qf, kf, oT, lse = saved
    oT2 = oT.reshape(H * dv, S)
    doT = _mm_nt(w_o, dxm, F32, f"mla_do{t}").reshape(H, dv, S)
    dw_o = _mm_nn(oT2, dxm, BF16, f"mla_dwo{t}")
    dqT, dkf, dvv = _mla_bwd_attn(qf, kf, kf.transpose(0, 2, 1), v, oT, doT, lse, f"mla_attn_bwd{t}")
    dqf = dqT.transpose(0, 2, 1)
    dqn, dqp, dkn = dqf[..., :dn], dqf[..., dn:], dkf[..., :dn]
    dkp = _sum_heads(dkf[..., dn:], f"mla_dkp{t}")
    dqn_in, dqng = _rms_bwd(qn_in, qn_gain, dqn.reshape(H * S, dn), f"mla_qn_bwd{t}", scale=sc)
    dqp_in, dqrg = _rms_bwd(qp_in, qr_gain, dqp.reshape(H * S, R), f"mla_qp_bwd{t}", scale=sc, rope=rope)
    dkn_in, dkng = _rms_bwd(kn_in, kn_gain, dkn.reshape(H * S, dn), f"mla_kn_bwd{t}")
    dkpe_pre, dkrg = _rms_bwd(kpe_pre, kr_gain, dkp, f"mla_kp_bwd{t}", rope=rope)
    dq = jnp.concatenate([dqn_in.reshape(H, S, dn), dqp_in.reshape(H, S, R)], axis=-1).transpose(1, 0, 2).reshape(S, -1)
    dkv = jnp.concatenate([dkn_in.reshape(H, S, dn), dvv], axis=-1).transpose(1, 0, 2).reshape(S, -1)
    dcq = _mm_nt(dq, w_uq, F32, f"mla_dcq{t}")
    dw_uq = _mm_tn(cq, dq, BF16, f"mla_dwuq{t}", out_panels=N_CHIPS)
    dckv = _mm_nt(dkv, w_ukv, F32, f"mla_dckv{t}")
    dw_ukv = _mm_tn(ckv, dkv, BF16, f"mla_dwukv{t}", out_panels=N_CHIPS)
    dcq_pre, dqag = _rms_bwd(cq_pre, q_a_gain, dcq, f"mla_cq_bwd{t}")
    dckv_pre, dkvag = _rms_bwd(ckv_pre, kv_a_gain, dckv, f"mla_ckv_bwd{t}")
    dd = jnp.concatenate([dcq_pre, dckv_pre, dkpe_pre], axis=1)
    dh = _mm_nt(dd, w_down, F32, f"mla_dh{t}")
    dw_down = _mm_tn(h, dd, BF16, f"mla_dwdown{t}")
    dx, dgn = _rms_bwd(x, g, dh, f"mla_norm_bwd{t}", res=dxm)
    return dx, dict(mla_w_down=dw_down, mla_q_a_gain=dqag, mla_kv_a_gain=dkvag, mla_w_uq=dw_uq, mla_w_ukv=dw_ukv,
                    mla_qn_gain=dqng, mla_qr_gain=dqrg, mla_kn_gain=dkng, mla_kr_gain=dkrg, mla_w_o=dw_o, norm_mix_g=dgn)


def _chips_to_axis(gathered, axis):
    moved = jnp.moveaxis(gathered, 0, axis)
    shape = list(moved.shape)
    shape[axis:axis + 2] = [shape[axis] * shape[axis + 1]]
    return moved.reshape(shape)


def kernel(x, positions, norm_mix_g, norm_ffn_g, pool_w, pool_scale, swa_w_qkv, swa_q_gain, swa_k_gain, swa_sinks, swa_w_o, mla_w_down, mla_q_a_gain, mla_kv_a_gain, mla_w_uq, mla_w_ukv, mla_qn_gain, mla_qr_gain, mla_kn_gain, mla_kr_gain, mla_w_o, ffn_w_in, ffn_conv_w, ffn_conv_b, ffn_w_out, loss_target, m_norm_mix_g, m_norm_ffn_g, m_pool_w, m_pool_scale, m_swa_w_qkv, m_swa_q_gain, m_swa_k_gain, m_swa_sinks, m_swa_w_o, m_mla_w_down, m_mla_q_a_gain, m_mla_kv_a_gain, m_mla_w_uq, m_mla_w_ukv, m_mla_qn_gain, m_mla_qr_gain, m_mla_kn_gain, m_mla_kr_gain, m_mla_w_o, m_ffn_w_in, m_ffn_conv_w, m_ffn_conv_b, m_ffn_w_out, v_norm_mix_g, v_norm_ffn_g, v_pool_w, v_pool_scale, v_swa_w_qkv, v_swa_q_gain, v_swa_k_gain, v_swa_sinks, v_swa_w_o, v_mla_w_down, v_mla_q_a_gain, v_mla_kv_a_gain, v_mla_w_uq, v_mla_w_ukv, v_mla_qn_gain, v_mla_qr_gain, v_mla_kn_gain, v_mla_kr_gain, v_mla_w_o, v_ffn_w_in, v_ffn_conv_w, v_ffn_conv_b, v_ffn_w_out):
    args = dict(locals())
    W = {n: args[n] for n in WEIGHTS}
    M = {n: args["m_" + n] for n in WEIGHTS}
    V = {n: args["v_" + n] for n in WEIGHTS}
    xs = x[0]
    S, D = xs.shape
    chip = 2 * lax.axis_index("x") + lax.axis_index("y")

    big = ['pool_w', 'swa_w_qkv', 'swa_w_o', 'mla_w_down', 'mla_w_uq', 'mla_w_ukv', 'mla_w_o', 'ffn_w_in', 'ffn_w_out']
    small_sharded = [n for n in WEIGHTS if SMALL.get(n) is not None]
    mixer_w = {0: ['pool_w'], 1: ['swa_w_qkv', 'swa_w_o'], 2: ['mla_w_down', 'mla_w_uq', 'mla_w_ukv', 'mla_w_o']}
    Wb = {n: W[n].astype(BF16) for n in big}
    Wg = {}

    def mixer_keys(i):
        return [(n, i // 3) for n in mixer_w[i % 3]]

    def gather_rider(keys):
        return (lambda: _gather_comm([(Wb[n], l) for n, l in keys])), (lambda res: Wg.update(zip(keys, res)))

    assert DEPTH == 4
    keys0 = mixer_keys(0) + [('ffn_w_in', 0)]
    riders = {
        'ffn_in0': gather_rider([('ffn_w_out', 0)] + mixer_keys(1)),
        'ffn_out0': gather_rider([('ffn_w_out', 1)]),
        'swa_attn1': gather_rider([('ffn_w_in', 1)]),
        'ffn_in1': gather_rider(mixer_keys(2)),
        'ffn_out1': gather_rider([('ffn_w_out', 2)]),
        'mla_attn2': gather_rider([('ffn_w_in', 2), ('ffn_w_out', 3)] + mixer_keys(3)),
        'ffn_in2': gather_rider([('ffn_w_in', 3)]),
    }
    first = _run_comm(_gather_comm([(Wb[n], l) for n, l in keys0] + [(W[n], None) for n in small_sharded]), "gather_first")
    Wg.update(zip(keys0, first))
    full = dict(W)
    for n, r in zip(small_sharded, first[len(keys0):]):
        full[n] = _chips_to_axis(r, SMALL[n])
    rows = lambda a: a.reshape((a.shape[0] * a.shape[1],) + a.shape[2:])

    posf = positions.astype(F32)
    slopes = jnp.asarray(2.0 ** (-8.0 * np.arange(1, SWA_HEADS + 1) / SWA_HEADS), dtype=F32)
    pos = (posf.reshape(S, 1), posf.reshape(1, S), slopes)
    inv = ROPE_THETA ** (-jnp.arange(0, MLA_ROPE, 2, dtype=F32) / MLA_ROPE)
    ang = posf[:, None] * inv[None, :]
    cos, sin = jnp.cos(ang), jnp.sin(ang)
    rope = (jnp.concatenate([cos, cos], axis=1), jnp.concatenate([-sin, sin], axis=1))

    margs_cache = {}

    def mixer_args(i):
        if i in margs_cache:
            return margs_cache[i]
        kind, j = i % 3, i // 3
        g = full['norm_mix_g'][i:i + 1]
        if kind == 0:
            byg = jnp.swapaxes(Wg[('pool_w', j)], 0, 1)
            a = (g, byg.reshape(byg.shape[0], -1, byg.shape[3]), full['pool_scale'][j:j + 1])
        elif kind == 1:
            a = (g, Wg[('swa_w_qkv', j)], full['swa_q_gain'][j:j + 1], full['swa_k_gain'][j:j + 1],
                 full['swa_sinks'][j:j + 1], rows(Wg[('swa_w_o', j)]), pos)
        else:
            a = (g, rows(Wg[('mla_w_down', j)]), full['mla_q_a_gain'][j:j + 1], full['mla_kv_a_gain'][j:j + 1],
                 Wg[('mla_w_uq', j)], Wg[('mla_w_ukv', j)], full['mla_qn_gain'][j:j + 1], full['mla_qr_gain'][j:j + 1],
                 full['mla_kn_gain'][j:j + 1], full['mla_kr_gain'][j:j + 1], rows(Wg[('mla_w_o', j)]), rope)
        margs_cache[i] = (kind, a)
        return kind, a

    def ffn_args(i):
        return (full['norm_ffn_g'][i:i + 1], Wg[('ffn_w_in', i)], full['ffn_conv_w'][i], full['ffn_conv_b'][i:i + 1],
                lambda: rows(Wg[('ffn_w_out', i)]))

    fwd = (_pool_fwd, _swa_fwd, _mla_fwd)
    bwd = (_pool_bwd, _swa_bwd, _mla_bwd)
    tape = []
    cur = xs
    for i in range(DEPTH):
        kind, margs = mixer_args(i)
        xm, msaved = fwd[kind](cur, *margs, i, riders)
        xo, fsaved = _ffn_fwd(xm, *ffn_args(i), i, riders)
        tape.append((cur, xm, msaved, fsaved))
        cur = xo
    sq, dcur = _loss_head(cur, loss_target[0], "loss_head")
    loss = lax.psum(0.5 / D * jnp.sum(sq), ("x", "y", "c"))

    def by_chip(n, gl):
        if n == 'pool_w':
            G, dg_ = gl.shape[0], gl.shape[1]
            return gl.reshape(G, N_CHIPS, dg_ // N_CHIPS, dg_).swapaxes(0, 1)
        if len(gl.shape) == 3:
            return gl
        return gl.reshape((N_CHIPS, gl.shape[0] // N_CHIPS) + gl.shape[1:])

    bufs = {n: lax.empty((N_CHIPS,) + tuple(W[n].shape), BF16) for n in big}

    def scatter_rider(group):
        def make(made):
            grp = group(made) if callable(group) else group
            names = list(dict.fromkeys(it[0] for it in grp))
            build = lambda: _scatter_comm([(gl, names.index(n), l, r, cut) for n, l, gl, r, cut in grp],
                                          [bufs[n] for n in names])
            return build, (lambda res: bufs.update(zip(names, res)))
        return make

    def own_w_out(i):
        return scatter_rider(lambda made: [('ffn_w_out', i, by_chip('ffn_w_out', made['ffn_w_out']), None, True)])

    grads = {n: [None] * W[n].shape[0] for n in WEIGHTS}
    half = W['ffn_w_in'].shape[1] // 2
    quarter = half // 2
    riders = {f'ffn_dwin{DEPTH - 1}': own_w_out(DEPTH - 1)}
    for i in reversed(range(DEPTH)):
        kind, margs = mixer_args(i)
        x_in, xm, msaved, fsaved = tape[i]
        dxm, fg = _ffn_bwd(dcur, xm, *ffn_args(i), fsaved, i, riders, split_dwin=(i == 0))
        dcur, mg = bwd[kind](dxm, x_in, *margs, msaved, i, riders)
        for n, gval in fg.items():
            grads[n][i] = gval
        for n, gval in mg.items():
            grads[n][i if n == 'norm_mix_g' else i // 3] = gval
        if i > 0:
            g_in = by_chip('ffn_w_in', fg['ffn_w_in'])
            piece = lambda q: ('ffn_w_in', i, g_in, (q * quarter, quarter), True)
            mixer_g = [(n, i // 3, by_chip(n, mg[n]), None, True) for n in mixer_w[kind]]
            on_attn = (i - 1) % 3 == 1
            riders = {f'ffn_dwout{i - 1}': scatter_rider([piece(0)]),
                      f'ffn_dact{i - 1}': scatter_rider([piece(1)]),
                      f'glu_bwd{i - 1}': scatter_rider([piece(2)]),
                      f'ffn_dh{i - 1}': scatter_rider([piece(3)] + ([] if on_attn else mixer_g)),
                      f'ffn_dwin{i - 1}' + ('a' if i == 1 else ''): own_w_out(i - 1)}
            if on_attn:
                riders[f'swa_attn_bwd{i - 1}'] = scatter_rider(mixer_g)
            if i == 1:
                riders['ffn_dwin0b'] = scatter_rider(
                    lambda made: [('ffn_w_in', 0, by_chip('ffn_w_in', made['ffn_w_in_a']), (0, half), False)])
    grad_x = dcur[None]
    last = [('ffn_w_in', 0, by_chip('ffn_w_in', grads['ffn_w_in'][0][1]), (half, half), False)]
    last += [(n, 0, by_chip(n, grads[n][0]), None, True) for n in mixer_w[0]]
    build, done = scatter_rider(last)(None)
    done(_run_comm(build(), "scatter_last"))

    partial = {n: _sum_chips(bufs[n].reshape(N_CHIPS, -1, bufs[n].shape[-1]), f"sum_{n}") for n in big}
    other = dict(zip(big, _run_comm(_swap_comm([partial[n] for n in big]), "swap_cores")))
    out = {}
    for n in big:
        C = partial[n].shape[1]
        res = _adamw(W[n].reshape(-1, C), M[n].reshape(-1, C), V[n].reshape(-1, C), [partial[n], other[n]], f"adamw_{n}")
        out[n] = [r.reshape(W[n].shape) for r in res]

    small = [n for n in WEIGHTS if n in SMALL]
    full_shapes = [tuple(full[n].shape) for n in small]
    sg = []
    for n in small:
        parts = grads[n]
        if n == 'ffn_conv_w':
            sg.append(jnp.stack(parts))
        else:
            sg.append(jnp.concatenate(parts, axis=0))
    buf, offs = _pack(sg)
    summed = _unpack(_allreduce_small(buf, "allreduce_small"), offs, full_shapes)
    gsm = []
    for n, gfull in zip(small, summed):
        ax = SMALL[n]
        if ax is not None:
            size = W[n].shape[ax]
            gfull = lax.dynamic_slice_in_dim(gfull, chip * size, size, axis=ax)
        gsm.append(gfull)
    gb, goffs = _pack(gsm)
    wb, _ = _pack([W[n] for n in small])
    mb, _ = _pack([M[n] for n in small])
    vb, _ = _pack([V[n] for n in small])
    res = _adamw(wb, mb, vb, [gb], "adamw_small")
    shapes = [tuple(W[n].shape) for n in small]
    unp = [_unpack(r, goffs, shapes) for r in res]
    for k, n in enumerate(small):
        out[n] = [u[k] for u in unp]

    return (loss, grad_x, *[out[n][0] for n in WEIGHTS], *[out[n][1] for n in WEIGHTS],
            *[out[n][2] for n in WEIGHTS], *[out[n][3] for n in WEIGHTS])
```
